```python
import jax
import jax.numpy as jnp
from jax import lax
import numpy as np

D_MODEL = 1024
BATCH = 8
SEQ = 2048
DEPTH = 2

D_MIX = D_MODEL
HEAD_DIM = 64
A_WIDTH = D_MIX // 4
A_BLOCKS = A_WIDTH // HEAD_DIM
A_BLOCK = HEAD_DIM
CONV_W = 4
LRU_C = 8.0
B_WIDTH = 3 * D_MIX // 8
B_HEADS = B_WIDTH // HEAD_DIM
B_DK = HEAD_DIM
B_DV = HEAD_DIM
B_FDIM = B_HEADS * B_DK
CHUNK = 64
C_WIDTH = D_MIX - A_WIDTH - B_WIDTH
C_HEADS = C_WIDTH // HEAD_DIM
C_HEAD = HEAD_DIM
C_LORA_W = 64
C_LORA_A = 64
C_LORA_G = 128
C_COLS = 3 * C_WIDTH + C_LORA_W + C_LORA_A + C_LORA_G
P_SIZES = (A_WIDTH, A_WIDTH, B_FDIM, B_FDIM, B_WIDTH, B_WIDTH, C_COLS)
P_WIDTH = 2 * A_WIDTH + 2 * B_FDIM + 2 * B_WIDTH + C_COLS
D_FF = -(-8 * D_MODEL // (3 * 256)) * 256
NORM_EPS = 1e-6
GN_EPS = 64e-5

kernel_name = 'hybrid_rglru_hgrn2_rwkv7_block'


def rmsnorm(x, gain=None, eps=NORM_EPS):
    xf = x.astype(jnp.float32)
    y = xf * lax.rsqrt(jnp.mean(xf * xf, axis=-1, keepdims=True) + eps)
    if gain is not None:
        y = y * gain.astype(jnp.float32)
    return y.astype(x.dtype)


def _causal_conv(u, w, b):
    T = u.shape[1]
    up = jnp.pad(u, ((0, 0), (CONV_W - 1, 0), (0, 0)))
    out = b
    for j in range(CONV_W):
        out = out + w[j] * up[:, j:j + T]
    return out


def _lin_combine(left, right):
    a1, b1 = left
    a2, b2 = right
    return a1 * a2, a2 * b1 + b2


def rglru_mixer(xa, ya, conv_w, conv_b, rg_w, rg_b, ig_w, ig_b, lam):
    dt = xa.dtype
    Bn, T, _ = xa.shape
    u = _causal_conv(xa, conv_w, conv_b)
    ub = u.reshape(Bn, T, A_BLOCKS, A_BLOCK)
    r = jax.nn.sigmoid(jnp.einsum('btgi,gij->btgj', ub, rg_w).reshape(Bn, T, A_WIDTH) + rg_b)
    i = jax.nn.sigmoid(jnp.einsum('btgi,gij->btgj', ub, ig_w).reshape(Bn, T, A_WIDTH) + ig_b)
    log_a = -LRU_C * r.astype(jnp.float32) * jax.nn.softplus(-lam.astype(jnp.float32))
    a = jnp.exp(log_a)
    mult = jnp.sqrt(-jnp.expm1(2.0 * log_a))
    mult = mult.at[:, 0].set(1.0)
    b = mult * (i * u).astype(jnp.float32)
    _, h = lax.associative_scan(_lin_combine, (a, b), axis=1)
    y = h.astype(dt) * jax.nn.gelu(ya)
    return rmsnorm(y.reshape(Bn, T, A_BLOCKS, A_BLOCK)).reshape(Bn, T, A_WIDTH)


def hgrn2_mixer(q, f_logit, v, g, lb):
    dt = q.dtype
    Bn, T, _ = q.shape
    nC = T // CHUNK
    lb = lb.astype(jnp.float32)
    log_f = jnp.logaddexp(jnp.log(lb), jnp.log1p(-lb) + jax.nn.log_sigmoid(f_logit.astype(jnp.float32)))
    k = -jnp.expm1(log_f)

    def to_chunks(z, dh):
        return z.reshape(Bn, nC, CHUNK, B_HEADS, dh).transpose(1, 0, 3, 2, 4)

    qc = to_chunks(q.astype(jnp.float32) * (B_DK ** -0.5), B_DK)
    kc = to_chunks(k, B_DK)
    vc = to_chunks(v.astype(jnp.float32), B_DV)
    lfc = to_chunks(log_f, B_DK)
    mask = jnp.tril(jnp.ones((CHUNK, CHUNK), dtype=bool))[:, :, None]

    def step(S, inp):
        qb, kb, vb, lfb = inp
        bcum = jnp.cumsum(lfb, axis=2)
        o_inter = jnp.einsum('bhtk,bhkv->bhtv', qb * jnp.exp(bcum), S)
        diff = bcum[:, :, :, None, :] - bcum[:, :, None, :, :]
        decay = jnp.exp(jnp.where(mask, diff, -jnp.inf))
        A = jnp.einsum('bhtk,bhsk,bhtsk->bhts', qb, kb, decay)
        o_intra = jnp.einsum('bhts,bhsv->bhtv', A, vb)
        blast = bcum[:, :, -1:, :]
        k_dec = kb * jnp.exp(blast - bcum)
        S = jnp.exp(blast[:, :, 0, :])[..., None] * S + jnp.einsum('bhsk,bhsv->bhkv', k_dec, vb)
        return S, o_inter + o_intra

    S0 = jnp.zeros((Bn, B_HEADS, B_DK, B_DV), jnp.float32)
    _, o = lax.scan(step, S0, (qc, kc, vc, lfc))
    o = o.transpose(1, 0, 3, 2, 4).reshape(Bn, T, B_HEADS, B_DV)
    o = rmsnorm(o).reshape(Bn, T, B_WIDTH)
    return (o * jax.nn.silu(g.astype(jnp.float32))).astype(dt)


def rwkv7_mixer(pc, mu, w0, w_up, a0, a_up, g_up, k_k, k_a, r_k, lnx_w, lnx_b):
    dt = pc.dtype
    Bn, T, _ = pc.shape
    prev = jnp.pad(pc[:, :-1], ((0, 0), (1, 0), (0, 0)))
    ps = (pc + (prev - pc) * mu).astype(jnp.float32)
    r, k, v, wl, al, gl = jnp.split(
        ps, [C_WIDTH, 2 * C_WIDTH, 3 * C_WIDTH, 3 * C_WIDTH + C_LORA_W,
             3 * C_WIDTH + C_LORA_W + C_LORA_A], axis=-1)
    w = -jax.nn.softplus(-(w0 + jnp.tanh(wl) @ w_up)) - 0.5
    decay = jnp.exp(-jnp.exp(w))
    a = jax.nn.sigmoid(a0 + al @ a_up)
    g = jax.nn.sigmoid(gl) @ g_up

    def heads(z):
        return z.reshape(Bn, T, C_HEADS, C_HEAD)

    kk = heads(k * k_k)
    kk = kk * lax.rsqrt(jnp.sum(kk * kk, axis=-1, keepdims=True) + 1e-12)
    k = k * (1.0 + (a - 1.0) * k_a)
    rh, kh, vh, wh, ah = heads(r), heads(k), heads(v), heads(decay), heads(a)

    def step(S, inp):
        rt, wt, kt, vt, kkt, at = inp
        sa = jnp.einsum('bhvk,bhk->bhv', S, -kkt)
        S = S * wt[:, :, None, :] + sa[..., None] * (kkt * at)[:, :, None, :] + vt[..., None] * kt[:, :, None, :]
        return S, jnp.einsum('bhvk,bhk->bhv', S, rt)

    xs = tuple(jnp.moveaxis(z, 1, 0) for z in (rh, wh, kh, vh, kk, ah))
    S0 = jnp.zeros((Bn, C_HEADS, C_HEAD, C_HEAD), jnp.float32)
    _, y = lax.scan(step, S0, xs)
    y = jnp.moveaxis(y, 0, 1)
    mean = jnp.mean(y, axis=-1, keepdims=True)
    var = jnp.mean(jnp.square(y - mean), axis=-1, keepdims=True)
    y = ((y - mean) * lax.rsqrt(var + GN_EPS)).reshape(Bn, T, C_WIDTH) * lnx_w + lnx_b
    bonus = (jnp.sum(rh * kh * r_k, axis=-1, keepdims=True) * vh).reshape(Bn, T, C_WIDTH)
    return ((y + bonus) * g).astype(dt)


def swiglu(h, w_gate, w_up, w_down):
    return (jax.nn.silu(h @ w_gate) * (h @ w_up)) @ w_down


def setup_inputs(seed: int = 0) -> dict:
    key = jax.random.key(seed)
    ks = jax.random.split(key, 32)
    f32 = jnp.float32
    nrm = lambda k, shape, s: jax.random.normal(k, shape, f32) * s
    L = DEPTH
    u = jax.random.uniform(ks[10], (L, A_WIDTH), f32, 0.9, 0.999)
    s = u ** (1.0 / LRU_C)
    return {
        'x': nrm(ks[0], (BATCH, SEQ, D_MODEL), 1.0),
        'c': nrm(ks[1], (BATCH, D_MODEL), 1.0),
        'norm1_g': 1.0 + nrm(ks[2], (L, D_MODEL), 0.02),
        'norm2_g': 1.0 + nrm(ks[3], (L, D_MODEL), 0.02),
        'ada_w': nrm(ks[4], (L, D_MODEL, 6 * D_MODEL), 0.5 * D_MODEL ** -0.5),
        'ada_b': nrm(ks[5], (L, 6 * D_MODEL), 0.02),
        'w_in': nrm(ks[6], (L, D_MODEL, P_WIDTH), D_MODEL ** -0.5),
        'conv_w': nrm(ks[7], (L, CONV_W, A_WIDTH), CONV_W ** -0.5),
        'conv_b': nrm(ks[8], (L, A_WIDTH), 0.02),
        'rg_w': nrm(ks[9], (L, A_BLOCKS, A_BLOCK, A_BLOCK), A_BLOCK ** -0.5),
        'rg_b': nrm(ks[11], (L, A_WIDTH), 0.02),
        'ig_w': nrm(ks[12], (L, A_BLOCKS, A_BLOCK, A_BLOCK), A_BLOCK ** -0.5),
        'ig_b': nrm(ks[13], (L, A_WIDTH), 0.02),
        'lru_lam': jnp.log(s) - jnp.log1p(-s),
        'hgrn_lb': nrm(ks[14], (L, B_FDIM), 0.5),
        'rwkv_mu': jax.random.uniform(ks[15], (L, C_COLS), f32, 0.0, 1.0),
        'rwkv_w0': jax.random.uniform(ks[16], (L, C_WIDTH), f32, -6.0, -1.0),
        'rwkv_w_up': nrm(ks[17], (L, C_LORA_W, C_WIDTH), 0.1),
        'rwkv_a0': nrm(ks[18], (L, C_WIDTH), 0.1),
        'rwkv_a_up': nrm(ks[19], (L, C_LORA_A, C_WIDTH), C_LORA_A ** -0.5),
        'rwkv_g_up': nrm(ks[20], (L, C_LORA_G, C_WIDTH), C_LORA_G ** -0.5),
        'rwkv_k_k': 0.85 + nrm(ks[21], (L, C_WIDTH), 0.05),
        'rwkv_k_a': 1.0 + nrm(ks[22], (L, C_WIDTH), 0.05),
        'rwkv_r_k': nrm(ks[23], (L, C_HEADS, C_HEAD), 0.1),
        'rwkv_lnx_w': 1.0 + nrm(ks[24], (L, C_WIDTH), 0.02),
        'rwkv_lnx_b': nrm(ks[25], (L, C_WIDTH), 0.02),
        'mix_beta': 1.0 + nrm(ks[26], (L, D_MIX), 0.02),
        'w_out': nrm(ks[27], (L, D_MIX, D_MODEL), D_MIX ** -0.5),
        'ffn_w_gate': nrm(ks[28], (L, D_MODEL, D_FF), D_MODEL ** -0.5),
        'ffn_w_up': nrm(ks[29], (L, D_MODEL, D_FF), D_MODEL ** -0.5),
        'ffn_w_down': nrm(ks[30], (L, D_FF, D_MODEL), D_FF ** -0.5),
        'final_g': 1.0 + nrm(ks[31], (D_MODEL,), 0.02),
    }


def reference(x, c, norm1_g, norm2_g, ada_w, ada_b, w_in, conv_w, conv_b, rg_w, rg_b,
              ig_w, ig_b, lru_lam, hgrn_lb, rwkv_mu, rwkv_w0, rwkv_w_up, rwkv_a0, rwkv_a_up,
              rwkv_g_up, rwkv_k_k, rwkv_k_a, rwkv_r_k, rwkv_lnx_w, rwkv_lnx_b, mix_beta, w_out,
              ffn_w_gate, ffn_w_up, ffn_w_down, final_g):
    lb_all = jnp.cumsum(jax.nn.softmax(hgrn_lb.astype(jnp.float32), axis=0), axis=0)
    lb_all = lb_all - lb_all[0]
    cs = jax.nn.silu(c)
    offsets = []
    acc = 0
    for sz in P_SIZES[:-1]:
        acc += sz
        offsets.append(acc)
    for l in range(DEPTH):
        mod = (cs @ ada_w[l] + ada_b[l])[:, None, :]
        sh1, sc1, g1, sh2, sc2, g2 = jnp.split(mod, 6, axis=-1)
        h = rmsnorm(x, norm1_g[l]) * (1.0 + sc1) + sh1
        p = h @ w_in[l]
        pa_x, pa_y, pb_q, pb_f, pb_v, pb_g, pc = jnp.split(p, offsets, axis=-1)
        ya = rglru_mixer(pa_x, pa_y, conv_w[l], conv_b[l], rg_w[l], rg_b[l], ig_w[l], ig_b[l], lru_lam[l])
        yb = hgrn2_mixer(pb_q, pb_f, pb_v, pb_g, lb_all[l])
        yc = rwkv7_mixer(pc, rwkv_mu[l], rwkv_w0[l], rwkv_w_up[l], rwkv_a0[l], rwkv_a_up[l],
                         rwkv_g_up[l], rwkv_k_k[l], rwkv_k_a[l], rwkv_r_k[l], rwkv_lnx_w[l], rwkv_lnx_b[l])
        y = jnp.concatenate([ya, yb, yc], axis=-1) * mix_beta[l]
        x = x + g1 * (y @ w_out[l])
        h = rmsnorm(x, norm2_g[l]) * (1.0 + sc2) + sh2
        x = x + g2 * swiglu(h, ffn_w_gate[l], ffn_w_up[l], ffn_w_down[l])
    return rmsnorm(x, final_g)
```

```python
import functools

import numpy as np
import jax
import jax.numpy as jnp
from jax import lax
from jax.experimental import pallas as pl
from jax.experimental.pallas import tpu as pltpu

F32 = jnp.float32
BF16 = jnp.bfloat16

HEAD = 64
LANES = 128
CONV_W = 4
LRU_C = 8.0
NORM_EPS = 1e-6
GN_EPS = 64e-5
CHUNK = 64
N_LEVELS = 6
LORA_W = 64
LORA_A = 64
LORA_G = 128
VMEM_LIMIT = 56 * 1024 * 1024

NN = (((1,), (0,)), ((), ()))
NT = (((1,), (1,)), ((), ()))


def _split(x, n):
    if x.dtype == BF16:
        return [x]
    parts = []
    r = x
    for i in range(n):
        p = r.astype(BF16)
        parts.append(p)
        if i + 1 < n:
            r = r - p.astype(F32)
    return parts


def _mm(a, b, dims=NN, na=2, nb=2):
    pa = _split(a, na)
    pb = _split(b, nb)
    order = max(len(pa), len(pb))
    acc = None
    for i, x in enumerate(pa):
        for j, y in enumerate(pb):
            if i + j < order:
                d = lax.dot_general(x, y, dims, preferred_element_type=F32)
                acc = d if acc is None else acc + d
    return acc


def _segsum(x, g01):
    return _mm(x, g01, NN, na=2, nb=1)


def _stack(x, m0):
    return jnp.concatenate([jnp.where(m0, x, 0.0), jnp.where(m0, 0.0, x)], axis=0)


def _unstack(xs):
    c = xs.shape[0] // 2
    return xs[:c] + xs[c:]


def _sigmoid(x):
    return 1.0 / (1.0 + jnp.exp(-x))


def _expm1(x):
    u = jnp.exp(x)
    um1 = u - 1.0
    return jnp.where(u == 1.0, x, jnp.where(um1 == -1.0, -1.0, um1 * x / jnp.log(u)))


def _softplus(x):
    return jnp.maximum(x, 0.0) + jnp.log1p(jnp.exp(-jnp.abs(x)))


def _mod_kernel(c_ref, w_ref, b_ref, o_ref):
    c = c_ref[...]
    cs = c * _sigmoid(c)
    o_ref[0] = _mm(cs, w_ref[0]) + b_ref[0]


def _modulation(c, ada_w, ada_b):
    L, D, N = ada_w.shape
    B = c.shape[0]
    tn = N // 4
    return pl.pallas_call(
        _mod_kernel,
        grid=(L, N // tn),
        in_specs=[
            pl.BlockSpec((B, D), lambda l, j: (0, 0)),
            pl.BlockSpec((1, D, tn), lambda l, j: (l, 0, j)),
            pl.BlockSpec((1, 1, tn), lambda l, j: (l, 0, j)),
        ],
        out_specs=pl.BlockSpec((1, B, tn), lambda l, j: (l, 0, j)),
        out_shape=jax.ShapeDtypeStruct((L, B, N), F32),
        compiler_params=pltpu.CompilerParams(
            dimension_semantics=("parallel", "parallel"), vmem_limit_bytes=VMEM_LIMIT),
        name="modulation",
    )(c, ada_w, ada_b.reshape(L, 1, N))


def _inproj_kernel(widths, x_ref, mod_ref, g_ref, w_ref, pa_ref, pb_ref, pc_ref):
    x = x_ref[0]
    ms = jnp.mean(x * x, axis=-1, keepdims=True)
    h = x * lax.rsqrt(ms + NORM_EPS) * g_ref[...]
    h = h * (1.0 + mod_ref[0, 1:2, :]) + mod_ref[0, 0:1, :]
    hb = h.astype(BF16)
    off = 0
    for ref, wd in zip((pa_ref, pb_ref, pc_ref), widths):
        ref[0] = jnp.dot(hb, w_ref[:, off:off + wd], preferred_element_type=F32)
        off += wd


def _in_projection(x, mod_l, gain, w_bf, widths, tm):
    B, T, D = x.shape
    P = w_bf.shape[1]
    return pl.pallas_call(
        functools.partial(_inproj_kernel, widths),
        grid=(B, T // tm),
        in_specs=[
            pl.BlockSpec((1, tm, D), lambda b, t: (b, t, 0)),
            pl.BlockSpec((1, 6, D), lambda b, t: (b, 0, 0)),
            pl.BlockSpec((1, D), lambda b, t: (0, 0)),
            pl.BlockSpec((D, P), lambda b, t: (0, 0), pipeline_mode=pl.Buffered(1)),
        ],
        out_specs=[pl.BlockSpec((1, tm, wd), lambda b, t: (b, t, 0)) for wd in widths],
        out_shape=[jax.ShapeDtypeStruct((B, T, wd), F32) for wd in widths],
        compiler_params=pltpu.CompilerParams(
            dimension_semantics=("parallel", "parallel"), vmem_limit_bytes=VMEM_LIMIT),
        name="in_projection",
    )(x, mod_l, gain, w_bf)


def _rglru_kernel(tc, pa_ref, cw_ref, vec_ref, wg_ref, g_ref, o_ref):
    T = pa_ref.shape[1]
    W = o_ref.shape[2]
    conv_b = vec_ref[0:1, :]
    rg_b = vec_ref[1:2, :]
    ig_b = vec_ref[2:3, :]
    sp_lam = _softplus(-vec_ref[3:4, :])
    beta = vec_ref[4:5, :]
    row = lax.broadcasted_iota(jnp.int32, (tc, W), 0)
    g01 = g_ref[...]
    wg = wg_ref[...]

    def body(i, hprev):
        t0 = pl.multiple_of(i * tc, tc)
        xa = pa_ref[0, pl.ds(t0, tc), 0:W]
        ya = pa_ref[0, pl.ds(t0, tc), W:2 * W]
        tp = pl.multiple_of(jnp.maximum(t0 - 8, 0), 8)
        prev8 = jnp.where(i > 0, pa_ref[0, pl.ds(tp, 8), 0:W], 0.0)
        win = jnp.concatenate([prev8, xa], axis=0)
        u = conv_b + cw_ref[CONV_W - 1:CONV_W, :] * xa
        for j in range(1, CONV_W):
            u = u + cw_ref[CONV_W - 1 - j:CONV_W - j, :] * pltpu.roll(win, j, 0)[8:]
        z = jnp.dot(u.astype(BF16), wg, preferred_element_type=F32)
        r = _sigmoid(z[:, 0:W] + rg_b)
        ig = _sigmoid(z[:, W:2 * W] + ig_b)
        log_a = (-LRU_C) * r * sp_lam
        a = jnp.exp(log_a)
        mult = jnp.sqrt(-_expm1(2.0 * log_a))
        mult = jnp.where(row + t0 == 0, 1.0, mult)
        bv = mult * (ig * u)
        s = 1
        while s < tc:
            keep = row >= s
            a_s = jnp.where(keep, pltpu.roll(a, s, 0), 1.0)
            b_s = jnp.where(keep, pltpu.roll(bv, s, 0), 0.0)
            bv = a * b_s + bv
            a = a * a_s
            s *= 2
        h = a * hprev + bv
        y = h * jax.nn.gelu(ya)
        ms = _segsum(y * y, g01) * (1.0 / HEAD)
        o_ref[0, pl.ds(t0, tc), :] = (y * lax.rsqrt(ms + NORM_EPS) * beta).astype(o_ref.dtype)
        return h[tc - 1:tc, :]

    lax.fori_loop(0, T // tc, body, jnp.zeros((1, W), F32))


def _rglru(pa, conv_w, vecs, wg_bf, g01, tc=128):
    B, T, W2 = pa.shape
    W = W2 // 2
    return pl.pallas_call(
        functools.partial(_rglru_kernel, tc),
        grid=(B,),
        in_specs=[
            pl.BlockSpec((1, T, W2), lambda b: (b, 0, 0)),
            pl.BlockSpec(conv_w.shape, lambda b: (0, 0)),
            pl.BlockSpec(vecs.shape, lambda b: (0, 0)),
            pl.BlockSpec(wg_bf.shape, lambda b: (0, 0)),
            pl.BlockSpec(g01.shape, lambda b: (0, 0)),
        ],
        out_specs=pl.BlockSpec((1, T, W), lambda b: (b, 0, 0)),
        out_shape=jax.ShapeDtypeStruct((B, T, W), BF16),
        compiler_params=pltpu.CompilerParams(
            dimension_semantics=("parallel",), vmem_limit_bytes=VMEM_LIMIT),
        name="rglru",
    )(pa, conv_w, vecs, wg_bf, g01)


def _hgrn_kernel(layer, pb_ref, lb_ref, beta_ref, wst_ref, g_ref, o_ref, s_ref):
    C = CHUNK
    TB = pb_ref.shape[1]
    W = o_ref.shape[2]
    n_pairs = W // LANES

    @pl.when(pl.program_id(1) == 0)
    def _():
        s_ref[...] = jnp.zeros_like(s_ref)

    lbr = lb_ref[...]
    n_layers = lbr.shape[0]
    mx = lbr[0:1, :]
    for i in range(1, n_layers):
        mx = jnp.maximum(mx, lbr[i:i + 1, :])
    es = [jnp.exp(lbr[i:i + 1, :] - mx) for i in range(n_layers)]
    tot = es[0]
    for i in range(1, n_layers):
        tot = tot + es[i]
    sm = [e / tot for e in es]
    cum = sm[0]
    for i in range(1, layer + 1):
        cum = cum + sm[i]
    lb = cum - sm[0]
    log_lb = jnp.log(lb)
    log_1mlb = jnp.log1p(-lb)

    beta = beta_ref[...]
    g01 = g_ref[...]
    wst = wst_ref[...]
    lane = lax.broadcasted_iota(jnp.int32, (C, LANES), 1)
    m0 = lane < HEAD
    rowc = lax.broadcasted_iota(jnp.int32, (C, LANES), 0)
    r2 = lax.broadcasted_iota(jnp.int32, (2 * C, 2 * C), 0)
    c2 = lax.broadcasted_iota(jnp.int32, (2 * C, 2 * C), 1)
    eye = r2 == c2
    second = [((rowc >> (N_LEVELS - 1 - lev)) & 1) == 1 for lev in range(N_LEVELS)]
    same = [(r2 >> (N_LEVELS - lev)) == (c2 >> (N_LEVELS - lev)) for lev in range(N_LEVELS)]

    def body(i, carry):
        t0 = pl.multiple_of(i * C, C)
        q = pb_ref[0, pl.ds(t0, C), 0:W] * (HEAD ** -0.5)
        fl = pb_ref[0, pl.ds(t0, C), W:2 * W]
        v = pb_ref[0, pl.ds(t0, C), 2 * W:3 * W]
        g = pb_ref[0, pl.ds(t0, C), 3 * W:4 * W]
        lc = log_1mlb + (jnp.minimum(fl, 0.0) - jnp.log1p(jnp.exp(-jnp.abs(fl))))
        log_f = jnp.maximum(log_lb, lc) + jnp.log1p(jnp.exp(-jnp.abs(log_lb - lc)))
        kg = -_expm1(log_f)
        ex = jnp.exp(_mm(wst, log_f, NN, na=1, nb=3))
        q_in = q * ex[0:C]
        k_dec = kg * ex[C:2 * C]
        outs = []
        for p in range(n_pairs):
            sl = slice(p * LANES, (p + 1) * LANES)
            qp, kp = q[:, sl], kg[:, sl]
            amat = jnp.where(eye, _mm(_stack(qp, m0), _stack(kp, m0), NT), 0.0)
            for lev in range(N_LEVELS):
                e = ex[(2 + lev) * C:(3 + lev) * C, sl]
                qe = jnp.where(second[lev], qp * e, 0.0)
                ke = jnp.where(second[lev], 0.0, kp * e)
                amat = amat + jnp.where(
                    same[lev], _mm(_stack(qe, m0), _stack(ke, m0), NT), 0.0)
            v_s = _stack(v[:, sl], m0)
            st = s_ref[p]
            o_s = _mm(amat, v_s) + _mm(_stack(q_in[:, sl], m0), st, NT)
            s_ref[p] = st * ex[C - 1:C, sl] + _mm(v_s.T, _stack(k_dec[:, sl], m0))
            outs.append(_unstack(o_s))
        o = jnp.concatenate(outs, axis=1)
        ms = _segsum(o * o, g01) * (1.0 / HEAD)
        o = o * lax.rsqrt(ms + NORM_EPS)
        o = o * (g * _sigmoid(g)) * beta
        o_ref[0, pl.ds(t0, C), :] = o.astype(o_ref.dtype)
        return carry

    lax.fori_loop(0, TB // C, body, 0)


def _hgrn_weights():
    C = CHUNK
    w = np.zeros(((2 + N_LEVELS) * C, C), np.float32)
    for r in range(C):
        w[r, :r + 1] = 1.0
        w[C + r, r + 1:] = 1.0
        for lev in range(N_LEVELS):
            n = C >> lev
            m = (r // n) * n + n // 2 - 1
            if r % n >= n // 2:
                w[(2 + lev) * C + r, m + 1:r + 1] = 1.0
            else:
                w[(2 + lev) * C + r, r + 1:m + 1] = 1.0
    return w


def _hgrn(pb, hgrn_lb, beta_b, wst, g01, layer, tb):
    B, T, W4 = pb.shape
    W = W4 // 4
    return pl.pallas_call(
        functools.partial(_hgrn_kernel, layer),
        grid=(B, T // tb),
        in_specs=[
            pl.BlockSpec((1, tb, W4), lambda b, t: (b, t, 0)),
            pl.BlockSpec(hgrn_lb.shape, lambda b, t: (0, 0)),
            pl.BlockSpec(beta_b.shape, lambda b, t: (0, 0)),
            pl.BlockSpec(wst.shape, lambda b, t: (0, 0)),
            pl.BlockSpec(g01.shape, lambda b, t: (0, 0)),
        ],
        out_specs=pl.BlockSpec((1, tb, W), lambda b, t: (b, t, 0)),
        out_shape=jax.ShapeDtypeStruct((B, T, W), BF16),
        scratch_shapes=[pltpu.VMEM((W // LANES, LANES, LANES), F32)],
        compiler_params=pltpu.CompilerParams(
            dimension_semantics=("parallel", "arbitrary"), vmem_limit_bytes=VMEM_LIMIT),
        name="hgrn2",
    )(pb, hgrn_lb, beta_b, wst, g01)


def _rwkv_kernel(pc_ref, mu_ref, vec_ref, lora_ref, gup_ref, tri_ref, g_ref, o_ref,
                 s_ref, prev_ref):
    C = CHUNK
    TB = pc_ref.shape[1]
    W = o_ref.shape[2]
    n_pairs = W // LANES

    @pl.when(pl.program_id(1) == 0)
    def _():
        s_ref[...] = jnp.zeros_like(s_ref)
        prev_ref[...] = jnp.zeros_like(prev_ref)

    mu = mu_ref[...]
    w0 = vec_ref[0:1, :]
    a0 = vec_ref[1:2, :]
    k_k = vec_ref[2:3, :]
    k_a = vec_ref[3:4, :]
    r_k = vec_ref[4:5, :]
    lnx_w = vec_ref[5:6, :]
    lnx_b = vec_ref[6:7, :]
    beta = vec_ref[7:8, :]
    lora = _split(lora_ref[...], 2)
    gup = _split(gup_ref[...], 2)
    tri = tri_ref[...]
    g01 = g_ref[...]
    lane = lax.broadcasted_iota(jnp.int32, (C, LANES), 1)
    m0 = lane < HEAD
    r2 = lax.broadcasted_iota(jnp.int32, (2 * C, 2 * C), 0)
    c2 = lax.broadcasted_iota(jnp.int32, (2 * C, 2 * C), 1)
    strict = r2 > c2
    incl = r2 >= c2
    eye_f = jnp.where(r2 == c2, 1.0, 0.0)
    o_w = 3 * W
    o_g = o_w + LORA_W + LORA_A

    def mm_parts(x, parts):
        xs = _split(x, 2)
        return (lax.dot_general(xs[0], parts[0], NN, preferred_element_type=F32)
                + lax.dot_general(xs[0], parts[1], NN, preferred_element_type=F32)
                + lax.dot_general(xs[1], parts[0], NN, preferred_element_type=F32))

    def body(i, carry):
        t0 = pl.multiple_of(i * C, C)
        pc = pc_ref[0, pl.ds(t0, C), :]
        win = jnp.concatenate([prev_ref[...], pc], axis=0)
        prev = pltpu.roll(win, 1, 0)[8:]
        prev_ref[...] = pc[C - 8:C, :]
        ps = pc + (prev - pc) * mu
        r = ps[:, 0:W]
        k = ps[:, W:2 * W]
        v = ps[:, 2 * W:3 * W]
        z = ps[:, o_w:o_g]
        z = jnp.where(lane < LORA_W, jnp.tanh(z), z)
        lo = mm_parts(z, lora)
        w_raw = -_softplus(-(w0 + lo[:, 0:W])) - 0.5
        lw = -jnp.exp(w_raw)
        a_sig = _sigmoid(a0 + lo[:, W:2 * W])
        gate = mm_parts(_sigmoid(ps[:, o_g:o_g + LORA_G]), gup)
        kk = k * k_k
        kk = kk * lax.rsqrt(_segsum(kk * kk, g01) + 1e-12)
        k2 = k * (1.0 + (a_sig - 1.0) * k_a)
        gam = _mm(tri, lw, NN, na=1, nb=3)
        e_in = jnp.exp(gam)
        e_ex = jnp.exp(gam - lw)
        e_neg = jnp.exp(-gam)
        at = -kk * e_ex
        rt = r * e_in
        bt = kk * a_sig * e_neg
        kt = k2 * e_neg
        outs = []
        for p in range(n_pairs):
            sl = slice(p * LANES, (p + 1) * LANES)
            a_s = _stack(at[:, sl], m0)
            r_s = _stack(rt[:, sl], m0)
            b_s = _stack(bt[:, sl], m0)
            k_s = _stack(kt[:, sl], m0)
            v_s = _stack(v[:, sl], m0)
            lmat = jnp.where(strict, _mm(a_s, b_s, NT), 0.0)
            ak = jnp.where(strict, _mm(a_s, k_s, NT), 0.0)
            rb = jnp.where(incl, _mm(r_s, b_s, NT), 0.0)
            rk = jnp.where(incl, _mm(r_s, k_s, NT), 0.0)
            st = s_ref[p]
            rhs = _mm(a_s, st, NT) + _mm(ak, v_s)
            pw = lmat
            tinv = eye_f + lmat
            for _ in range(N_LEVELS - 1):
                pw = _mm(pw, pw)
                tinv = tinv + _mm(tinv, pw)
            u_s = _mm(tinv, rhs)
            y_s = _mm(r_s, st, NT) + _mm(rb, u_s) + _mm(rk, v_s)
            s_ref[p] = (st + _mm(u_s.T, b_s) + _mm(v_s.T, k_s)) * e_in[C - 1:C, sl]
            outs.append(_unstack(y_s))
        y = jnp.concatenate(outs, axis=1)
        mean = _segsum(y, g01) * (1.0 / HEAD)
        d = y - mean
        var = _segsum(d * d, g01) * (1.0 / HEAD)
        yn = d * lax.rsqrt(var + GN_EPS) * lnx_w + lnx_b
        bonus = _segsum(r * k2 * r_k, g01) * v
        o_ref[0, pl.ds(t0, C), :] = ((yn + bonus) * gate * beta).astype(o_ref.dtype)
        return carry

    lax.fori_loop(0, TB // C, body, 0)


def _rwkv(pc, mu, vecs, lora_w, g_up, tri, g01, tb):
    B, T, PC = pc.shape
    W = vecs.shape[1]
    return pl.pallas_call(
        _rwkv_kernel,
        grid=(B, T // tb),
        in_specs=[
            pl.BlockSpec((1, tb, PC), lambda b, t: (b, t, 0)),
            pl.BlockSpec(mu.shape, lambda b, t: (0, 0)),
            pl.BlockSpec(vecs.shape, lambda b, t: (0, 0)),
            pl.BlockSpec(lora_w.shape, lambda b, t: (0, 0)),
            pl.BlockSpec(g_up.shape, lambda b, t: (0, 0)),
            pl.BlockSpec(tri.shape, lambda b, t: (0, 0)),
            pl.BlockSpec(g01.shape, lambda b, t: (0, 0)),
        ],
        out_specs=pl.BlockSpec((1, tb, W), lambda b, t: (b, t, 0)),
        out_shape=jax.ShapeDtypeStruct((B, T, W), BF16),
        scratch_shapes=[pltpu.VMEM((W // LANES, LANES, LANES), F32),
                        pltpu.VMEM((8, PC), F32)],
        compiler_params=pltpu.CompilerParams(
            dimension_semantics=("parallel", "arbitrary"), vmem_limit_bytes=VMEM_LIMIT),
        name="rwkv7",
    )(pc, mu, vecs, lora_w, g_up, tri, g01)


def _ffn_kernel(final, widths, x_ref, ya_ref, yb_ref, yc_ref, mod_ref, g_ref, wo_ref,
                wg_ref, wu_ref, wd_ref, fg_ref, o_ref):
    x = x_ref[0]
    acc = None
    off = 0
    for ref, wd in zip((ya_ref, yb_ref, yc_ref), widths):
        d = jnp.dot(ref[0], wo_ref[off:off + wd, :], preferred_element_type=F32)
        acc = d if acc is None else acc + d
        off += wd
    x1 = x + mod_ref[0, 2:3, :] * acc
    ms = jnp.mean(x1 * x1, axis=-1, keepdims=True)
    h = x1 * lax.rsqrt(ms + NORM_EPS) * g_ref[...]
    h = h * (1.0 + mod_ref[0, 4:5, :]) + mod_ref[0, 3:4, :]
    hb = h.astype(BF16)
    gt = jnp.dot(hb, wg_ref[...], preferred_element_type=F32)
    up = jnp.dot(hb, wu_ref[...], preferred_element_type=F32)
    act = (gt * _sigmoid(gt) * up).astype(BF16)
    dn = jnp.dot(act, wd_ref[...], preferred_element_type=F32)
    x2 = x1 + mod_ref[0, 5:6, :] * dn
    if final:
        ms2 = jnp.mean(x2 * x2, axis=-1, keepdims=True)
        x2 = x2 * lax.rsqrt(ms2 + NORM_EPS) * fg_ref[...]
    o_ref[0] = x2


def _out_ffn(x, ya, yb, yc, mod_l, gain, wo, wg, wu, wd, final_g, final, tm):
    B, T, D = x.shape
    widths = (ya.shape[2], yb.shape[2], yc.shape[2])
    const = lambda shape: pl.BlockSpec(shape, lambda b, t: (0, 0), pipeline_mode=pl.Buffered(1))
    return pl.pallas_call(
        functools.partial(_ffn_kernel, final, widths),
        grid=(B, T // tm),
        in_specs=[
            pl.BlockSpec((1, tm, D), lambda b, t: (b, t, 0)),
            pl.BlockSpec((1, tm, widths[0]), lambda b, t: (b, t, 0)),
            pl.BlockSpec((1, tm, widths[1]), lambda b, t: (b, t, 0)),
            pl.BlockSpec((1, tm, widths[2]), lambda b, t: (b, t, 0)),
            pl.BlockSpec((1, 6, D), lambda b, t: (b, 0, 0)),
            pl.BlockSpec((1, D), lambda b, t: (0, 0)),
            const(wo.shape), const(wg.shape), const(wu.shape), const(wd.shape),
            pl.BlockSpec((1, D), lambda b, t: (0, 0)),
        ],
        out_specs=pl.BlockSpec((1, tm, D), lambda b, t: (b, t, 0)),
        out_shape=jax.ShapeDtypeStruct((B, T, D), F32),
        compiler_params=pltpu.CompilerParams(
            dimension_semantics=("parallel", "parallel"), vmem_limit_bytes=VMEM_LIMIT),
        name="out_ffn",
    )(x, ya, yb, yc, mod_l, gain, wo, wg, wu, wd, final_g)


def _block_diag(w):
    G, n, _ = w.shape
    eye = jnp.eye(G, dtype=w.dtype)
    return (eye[:, None, :, None] * w[:, :, None, :]).reshape(G * n, G * n)


def _seg_ones(width):
    idx = np.arange(width) // HEAD
    return jnp.asarray((idx[:, None] == idx[None, :]).astype(np.float32), dtype=BF16)


def kernel(x, c, norm1_g, norm2_g, ada_w, ada_b, w_in, conv_w, conv_b, rg_w, rg_b, ig_w, ig_b, lru_lam, hgrn_lb, rwkv_mu, rwkv_w0, rwkv_w_up, rwkv_a0, rwkv_a_up, rwkv_g_up, rwkv_k_k, rwkv_k_a, rwkv_r_k, rwkv_lnx_w, rwkv_lnx_b, mix_beta, w_out, ffn_w_gate, ffn_w_up, ffn_w_down, final_g):
    B, T, D = x.shape
    L = w_in.shape[0]
    WA = conv_w.shape[2]
    WB = hgrn_lb.shape[1]
    WC = rwkv_w0.shape[1]
    widths_p = (2 * WA, 4 * WB, rwkv_mu.shape[1])
    tm = 256

    mod = _modulation(c, ada_w, ada_b).reshape(L, B, 6, D)
    wst = jnp.asarray(_hgrn_weights(), dtype=BF16)
    tri = jnp.asarray(np.tril(np.ones((CHUNK, CHUNK), np.float32)), dtype=BF16)
    g_a, g_b, g_c = _seg_ones(WA), _seg_ones(WB), _seg_ones(WC)

    for l in range(L):
        pa, pb, pc = _in_projection(x, mod[l], norm1_g[l][None], w_in[l].astype(BF16),
                                    widths_p, tm)
        beta = mix_beta[l]
        vec_a = jnp.stack([conv_b[l], rg_b[l], ig_b[l], lru_lam[l], beta[:WA],
                           jnp.zeros_like(beta[:WA]), jnp.zeros_like(beta[:WA]),
                           jnp.zeros_like(beta[:WA])])
        wg_a = jnp.concatenate([_block_diag(rg_w[l]), _block_diag(ig_w[l])], axis=1).astype(BF16)
        ya = _rglru(pa, conv_w[l], vec_a, wg_a, g_a)
        yb = _hgrn(pb, hgrn_lb, beta[None, WA:WA + WB], wst, g_b, l, 256)
        vec_c = jnp.stack([rwkv_w0[l], rwkv_a0[l], rwkv_k_k[l], rwkv_k_a[l],
                           rwkv_r_k[l].reshape(-1), rwkv_lnx_w[l], rwkv_lnx_b[l],
                           beta[WA + WB:]])
        zw = jnp.zeros_like(rwkv_w_up[l])
        lora_w = jnp.concatenate(
            [jnp.concatenate([rwkv_w_up[l], zw], axis=1),
             jnp.concatenate([zw, rwkv_a_up[l]], axis=1)], axis=0)
        yc = _rwkv(pc, rwkv_mu[l][None], vec_c, lora_w, rwkv_g_up[l], tri, g_c, 256)
        x = _out_ffn(x, ya, yb, yc, mod[l], norm2_g[l][None], w_out[l].astype(BF16),
                     ffn_w_gate[l].astype(BF16), ffn_w_up[l].astype(BF16),
                     ffn_w_down[l].astype(BF16), final_g[None], l == L - 1, tm)
    return x
```

```python
import functools

import numpy as np
import jax
import jax.numpy as jnp
from jax import lax
from jax.experimental import pallas as pl
from jax.experimental.pallas import tpu as pltpu

F32 = jnp.float32
BF16 = jnp.bfloat16

HEAD = 64
LANES = 128
CONV_W = 4
LRU_C = 8.0
NORM_EPS = 1e-6
GN_EPS = 64e-5
CHUNK = 64
N_LEVELS = 6
RWKV_GROUP = 2
LORA_W = 64
LORA_A = 64
LORA_G = 128
VMEM_LIMIT = 56 * 1024 * 1024

NN = (((1,), (0,)), ((), ()))
NT = (((1,), (1,)), ((), ()))

P_ATT = (1, 1)
P_INV = (2, 2)
P_APP = (1, 1)
P_STATE = (2, 2)


def _split(x, n):
    if x.dtype == BF16:
        return [x]
    parts = []
    r = x
    for i in range(n):
        p = r.astype(BF16)
        parts.append(p)
        if i + 1 < n:
            r = r - p.astype(F32)
    return parts


def _mm(a, b, dims=NN, na=2, nb=2):
    pa = _split(a, na)
    pb = _split(b, nb)
    order = max(len(pa), len(pb))
    acc = None
    for i, x in enumerate(pa):
        for j, y in enumerate(pb):
            if i + j < order:
                d = lax.dot_general(x, y, dims, preferred_element_type=F32)
                acc = d if acc is None else acc + d
    return acc


def _segsum(x, g01):
    return _mm(x, g01, NN, na=2, nb=1)


def _stack(x, m0):
    return jnp.concatenate([jnp.where(m0, x, 0.0), jnp.where(m0, 0.0, x)], axis=0)


def _unstack(xs):
    c = xs.shape[0] // 2
    return xs[:c] + xs[c:]


def _sigmoid(x):
    return 1.0 / (1.0 + jnp.exp(-x))


def _expm1(x):
    u = jnp.exp(x)
    um1 = u - 1.0
    return jnp.where(u == 1.0, x, jnp.where(um1 == -1.0, -1.0, um1 * x / jnp.log(u)))


def _softplus(x):
    return jnp.maximum(x, 0.0) + jnp.log1p(jnp.exp(-jnp.abs(x)))


def _mod_kernel(c_ref, w_ref, b_ref, o_ref):
    c = c_ref[...]
    cs = c * _sigmoid(c)
    o_ref[0] = _mm(cs, w_ref[0]) + b_ref[0]


def _modulation(c, ada_w, ada_b):
    L, D, N = ada_w.shape
    B = c.shape[0]
    tn = N // 4
    return pl.pallas_call(
        _mod_kernel,
        grid=(L, N // tn),
        in_specs=[
            pl.BlockSpec((B, D), lambda l, j: (0, 0)),
            pl.BlockSpec((1, D, tn), lambda l, j: (l, 0, j)),
            pl.BlockSpec((1, 1, tn), lambda l, j: (l, 0, j)),
        ],
        out_specs=pl.BlockSpec((1, B, tn), lambda l, j: (l, 0, j)),
        out_shape=jax.ShapeDtypeStruct((L, B, N), F32),
        compiler_params=pltpu.CompilerParams(
            dimension_semantics=("parallel", "parallel"), vmem_limit_bytes=VMEM_LIMIT),
        name="modulation",
    )(c, ada_w, ada_b.reshape(L, 1, N))


def _inproj_kernel(widths, x_ref, mod_ref, g_ref, w_ref, pa_ref, pb_ref, pc_ref):
    x = x_ref[0]
    ms = jnp.mean(x * x, axis=-1, keepdims=True)
    h = x * lax.rsqrt(ms + NORM_EPS) * g_ref[...]
    h = h * (1.0 + mod_ref[0, 1:2, :]) + mod_ref[0, 0:1, :]
    hb = h.astype(BF16)
    off = 0
    for ref, wd in zip((pa_ref, pb_ref, pc_ref), widths):
        ref[0] = jnp.dot(hb, w_ref[:, off:off + wd], preferred_element_type=F32)
        off += wd


def _in_projection(x, mod_l, gain, w_bf, widths, tm):
    B, T, D = x.shape
    P = w_bf.shape[1]
    return pl.pallas_call(
        functools.partial(_inproj_kernel, widths),
        grid=(B, T // tm),
        in_specs=[
            pl.BlockSpec((1, tm, D), lambda b, t: (b, t, 0)),
            pl.BlockSpec((1, 6, D), lambda b, t: (b, 0, 0)),
            pl.BlockSpec((1, D), lambda b, t: (0, 0)),
            pl.BlockSpec((D, P), lambda b, t: (0, 0), pipeline_mode=pl.Buffered(1)),
        ],
        out_specs=[pl.BlockSpec((1, tm, wd), lambda b, t: (b, t, 0)) for wd in widths],
        out_shape=[jax.ShapeDtypeStruct((B, T, wd), F32) for wd in widths],
        compiler_params=pltpu.CompilerParams(
            dimension_semantics=("parallel", "parallel"), vmem_limit_bytes=VMEM_LIMIT),
        name="in_projection",
    )(x, mod_l, gain, w_bf)


def _rglru_kernel(tc, pa_ref, cw_ref, vec_ref, wg_ref, g_ref, o_ref):
    T = pa_ref.shape[1]
    W = o_ref.shape[2]
    conv_b = vec_ref[0:1, :]
    rg_b = vec_ref[1:2, :]
    ig_b = vec_ref[2:3, :]
    sp_lam = _softplus(-vec_ref[3:4, :])
    beta = vec_ref[4:5, :]
    row = lax.broadcasted_iota(jnp.int32, (tc, W), 0)
    g01 = g_ref[...]
    wg = wg_ref[...]

    def body(i, hprev):
        t0 = pl.multiple_of(i * tc, tc)
        xa = pa_ref[0, pl.ds(t0, tc), 0:W]
        ya = pa_ref[0, pl.ds(t0, tc), W:2 * W]
        tp = pl.multiple_of(jnp.maximum(t0 - 8, 0), 8)
        prev8 = jnp.where(i > 0, pa_ref[0, pl.ds(tp, 8), 0:W], 0.0)
        win = jnp.concatenate([prev8, xa], axis=0)
        u = conv_b + cw_ref[CONV_W - 1:CONV_W, :] * xa
        for j in range(1, CONV_W):
            u = u + cw_ref[CONV_W - 1 - j:CONV_W - j, :] * pltpu.roll(win, j, 0)[8:]
        z = jnp.dot(u.astype(BF16), wg, preferred_element_type=F32)
        r = _sigmoid(z[:, 0:W] + rg_b)
        ig = _sigmoid(z[:, W:2 * W] + ig_b)
        log_a = (-LRU_C) * r * sp_lam
        a = jnp.exp(log_a)
        mult = jnp.sqrt(-_expm1(2.0 * log_a))
        mult = jnp.where(row + t0 == 0, 1.0, mult)
        bv = mult * (ig * u)
        s = 1
        while s < tc:
            keep = row >= s
            a_s = jnp.where(keep, pltpu.roll(a, s, 0), 1.0)
            b_s = jnp.where(keep, pltpu.roll(bv, s, 0), 0.0)
            bv = a * b_s + bv
            a = a * a_s
            s *= 2
        h = a * hprev + bv
        y = h * jax.nn.gelu(ya)
        ms = _segsum(y * y, g01) * (1.0 / HEAD)
        o_ref[0, pl.ds(t0, tc), :] = (y * lax.rsqrt(ms + NORM_EPS) * beta).astype(o_ref.dtype)
        return h[tc - 1:tc, :]

    lax.fori_loop(0, T // tc, body, jnp.zeros((1, W), F32))


def _rglru(pa, conv_w, vecs, wg_bf, g01, tc=128):
    B, T, W2 = pa.shape
    W = W2 // 2
    return pl.pallas_call(
        functools.partial(_rglru_kernel, tc),
        grid=(B,),
        in_specs=[
            pl.BlockSpec((1, T, W2), lambda b: (b, 0, 0)),
            pl.BlockSpec(conv_w.shape, lambda b: (0, 0)),
            pl.BlockSpec(vecs.shape, lambda b: (0, 0)),
            pl.BlockSpec(wg_bf.shape, lambda b: (0, 0)),
            pl.BlockSpec(g01.shape, lambda b: (0, 0)),
        ],
        out_specs=pl.BlockSpec((1, T, W), lambda b: (b, 0, 0)),
        out_shape=jax.ShapeDtypeStruct((B, T, W), BF16),
        compiler_params=pltpu.CompilerParams(
            dimension_semantics=("parallel",), vmem_limit_bytes=VMEM_LIMIT),
        name="rglru",
    )(pa, conv_w, vecs, wg_bf, g01)


def _hgrn_kernel(layer, pb_ref, lb_ref, beta_ref, wst_ref, g_ref, o_ref, s_ref):
    C = CHUNK
    TB = pb_ref.shape[1]
    W = o_ref.shape[2]
    n_pairs = W // LANES

    @pl.when(pl.program_id(1) == 0)
    def _():
        s_ref[...] = jnp.zeros_like(s_ref)

    lbr = lb_ref[...]
    n_layers = lbr.shape[0]
    mx = lbr[0:1, :]
    for i in range(1, n_layers):
        mx = jnp.maximum(mx, lbr[i:i + 1, :])
    es = [jnp.exp(lbr[i:i + 1, :] - mx) for i in range(n_layers)]
    tot = es[0]
    for i in range(1, n_layers):
        tot = tot + es[i]
    sm = [e / tot for e in es]
    cum = sm[0]
    for i in range(1, layer + 1):
        cum = cum + sm[i]
    lb = cum - sm[0]
    log_lb = jnp.log(lb)
    log_1mlb = jnp.log1p(-lb)

    beta = beta_ref[...]
    g01 = g_ref[...]
    wst = wst_ref[...]
    lane = lax.broadcasted_iota(jnp.int32, (C, LANES), 1)
    m0 = lane < HEAD
    rowc = lax.broadcasted_iota(jnp.int32, (C, LANES), 0)
    r2 = lax.broadcasted_iota(jnp.int32, (2 * C, 2 * C), 0)
    c2 = lax.broadcasted_iota(jnp.int32, (2 * C, 2 * C), 1)
    eye = r2 == c2
    second = [((rowc >> (N_LEVELS - 1 - lev)) & 1) == 1 for lev in range(N_LEVELS)]
    same = [(r2 >> (N_LEVELS - lev)) == (c2 >> (N_LEVELS - lev)) for lev in range(N_LEVELS)]

    def body(i, carry):
        t0 = pl.multiple_of(i * C, C)
        q = pb_ref[0, pl.ds(t0, C), 0:W] * (HEAD ** -0.5)
        fl = pb_ref[0, pl.ds(t0, C), W:2 * W]
        v = pb_ref[0, pl.ds(t0, C), 2 * W:3 * W]
        g = pb_ref[0, pl.ds(t0, C), 3 * W:4 * W]
        lc = log_1mlb + (jnp.minimum(fl, 0.0) - jnp.log1p(jnp.exp(-jnp.abs(fl))))
        log_f = jnp.maximum(log_lb, lc) + jnp.log1p(jnp.exp(-jnp.abs(log_lb - lc)))
        kg = -_expm1(log_f)
        ex = jnp.exp(_mm(wst, log_f, NN, na=1, nb=3))
        q_in = q * ex[0:C]
        k_dec = kg * ex[C:2 * C]
        outs = []
        for p in range(n_pairs):
            sl = slice(p * LANES, (p + 1) * LANES)
            qp, kp = q[:, sl], kg[:, sl]
            amat = jnp.where(eye, _mm(_stack(qp, m0), _stack(kp, m0), NT), 0.0)
            for lev in range(N_LEVELS):
                e = ex[(2 + lev) * C:(3 + lev) * C, sl]
                qe = jnp.where(second[lev], qp * e, 0.0)
                ke = jnp.where(second[lev], 0.0, kp * e)
                amat = amat + jnp.where(
                    same[lev], _mm(_stack(qe, m0), _stack(ke, m0), NT), 0.0)
            v_s = _stack(v[:, sl], m0)
            st = s_ref[p]
            o_s = _mm(amat, v_s) + _mm(_stack(q_in[:, sl], m0), st, NT)
            s_ref[p] = st * ex[C - 1:C, sl] + _mm(v_s.T, _stack(k_dec[:, sl], m0))
            outs.append(_unstack(o_s))
        o = jnp.concatenate(outs, axis=1)
        ms = _segsum(o * o, g01) * (1.0 / HEAD)
        o = o * lax.rsqrt(ms + NORM_EPS)
        o = o * (g * _sigmoid(g)) * beta
        o_ref[0, pl.ds(t0, C), :] = o.astype(o_ref.dtype)
        return carry

    lax.fori_loop(0, TB // C, body, 0)


def _hgrn_weights():
    C = CHUNK
    w = np.zeros(((2 + N_LEVELS) * C, C), np.float32)
    for r in range(C):
        w[r, :r + 1] = 1.0
        w[C + r, r + 1:] = 1.0
        for lev in range(N_LEVELS):
            n = C >> lev
            m = (r // n) * n + n // 2 - 1
            if r % n >= n // 2:
                w[(2 + lev) * C + r, m + 1:r + 1] = 1.0
            else:
                w[(2 + lev) * C + r, r + 1:m + 1] = 1.0
    return w


def _hgrn(pb, hgrn_lb, beta_b, wst, g01, layer, tb):
    B, T, W4 = pb.shape
    W = W4 // 4
    return pl.pallas_call(
        functools.partial(_hgrn_kernel, layer),
        grid=(B, T // tb),
        in_specs=[
            pl.BlockSpec((1, tb, W4), lambda b, t: (b, t, 0)),
            pl.BlockSpec(hgrn_lb.shape, lambda b, t: (0, 0)),
            pl.BlockSpec(beta_b.shape, lambda b, t: (0, 0)),
            pl.BlockSpec(wst.shape, lambda b, t: (0, 0)),
            pl.BlockSpec(g01.shape, lambda b, t: (0, 0)),
        ],
        out_specs=pl.BlockSpec((1, tb, W), lambda b, t: (b, t, 0)),
        out_shape=jax.ShapeDtypeStruct((B, T, W), BF16),
        scratch_shapes=[pltpu.VMEM((W // LANES, LANES, LANES), F32)],
        compiler_params=pltpu.CompilerParams(
            dimension_semantics=("parallel", "arbitrary"), vmem_limit_bytes=VMEM_LIMIT),
        name="hgrn2",
    )(pb, hgrn_lb, beta_b, wst, g01)


def _rwkv_kernel(pc_ref, mu_ref, vec_ref, lora_ref, gup_ref, tri_ref, g_ref, o_ref,
                 s_ref, prev_ref, rp_ref, y0_ref, mx_ref, nx_ref, bonus_ref, gate_ref, y_ref):
    C = CHUNK
    R = tri_ref.shape[0]
    G = R // C
    TB = pc_ref.shape[1]
    W = o_ref.shape[2]
    n_pairs = W // LANES

    @pl.when(pl.program_id(1) == 0)
    def _():
        s_ref[...] = jnp.zeros_like(s_ref)
        prev_ref[...] = jnp.zeros_like(prev_ref)

    mu = mu_ref[...]
    w0 = vec_ref[0:1, :]
    a0 = vec_ref[1:2, :]
    k_k = vec_ref[2:3, :]
    k_a = vec_ref[3:4, :]
    r_k = vec_ref[4:5, :]
    lnx_w = vec_ref[5:6, :]
    lnx_b = vec_ref[6:7, :]
    beta = vec_ref[7:8, :]
    lora = _split(lora_ref[...], 2)
    gup = _split(gup_ref[...], 2)
    tri = tri_ref[...]
    g01 = g_ref[...]
    lane = lax.broadcasted_iota(jnp.int32, (C, LANES), 1)
    lane_r = lax.broadcasted_iota(jnp.int32, (R, LANES), 1)
    m0 = lane < HEAD
    r2 = lax.broadcasted_iota(jnp.int32, (2 * C, 2 * C), 0)
    c2 = lax.broadcasted_iota(jnp.int32, (2 * C, 2 * C), 1)
    strict = r2 > c2
    incl = r2 >= c2
    eye_f = jnp.where(r2 == c2, 1.0, 0.0)
    o_w = 3 * W
    o_g = o_w + LORA_W + LORA_A

    def mm_parts(x, parts):
        xs = _split(x, 2)
        return (lax.dot_general(xs[0], parts[0], NN, preferred_element_type=F32)
                + lax.dot_general(xs[0], parts[1], NN, preferred_element_type=F32)
                + lax.dot_general(xs[1], parts[0], NN, preferred_element_type=F32))

    def prep(i, carry):
        t0 = pl.multiple_of(i * R, R)
        pc = pc_ref[0, pl.ds(t0, R), :]
        win = jnp.concatenate([prev_ref[...], pc], axis=0)
        prev = pltpu.roll(win, 1, 0)[8:]
        prev_ref[...] = pc[R - 8:R, :]
        ps = pc + (prev - pc) * mu
        r = ps[:, 0:W]
        k = ps[:, W:2 * W]
        v = ps[:, 2 * W:3 * W]
        z = ps[:, o_w:o_g]
        z = jnp.where(lane_r < LORA_W, jnp.tanh(z), z)
        lo = mm_parts(z, lora)
        w_raw = -_softplus(-(w0 + lo[:, 0:W])) - 0.5
        lw = -jnp.exp(w_raw)
        a_sig = _sigmoid(a0 + lo[:, W:2 * W])
        gate = mm_parts(_sigmoid(ps[:, o_g:o_g + LORA_G]), gup)
        kk = k * k_k
        kk = kk * lax.rsqrt(_segsum(kk * kk, g01) + 1e-12)
        k2 = k * (1.0 + (a_sig - 1.0) * k_a)
        gam = _mm(tri, lw, NN, na=1, nb=3)
        e_in = jnp.exp(gam)
        e_ex = jnp.exp(gam - lw)
        e_neg = jnp.exp(-gam)
        at = -kk * e_ex
        rt = r * e_in
        bt = kk * a_sig * e_neg
        kt = k2 * e_neg
        bonus_ref[pl.ds(t0, R), :] = _segsum(r * k2 * r_k, g01) * v
        gate_ref[pl.ds(t0, R), :] = gate * beta
        items = [(g, p) for g in range(G) for p in range(n_pairs)]
        n = range(len(items))

        def stacked(x):
            return [_stack(x[g * C:(g + 1) * C, p * LANES:(p + 1) * LANES], m0)
                    for g, p in items]

        a_s, r_s, b_s, k_s, v_s = stacked(at), stacked(rt), stacked(bt), stacked(kt), stacked(v)
        att = [_mm(jnp.concatenate([a_s[j], r_s[j]], axis=0),
                   jnp.concatenate([b_s[j], k_s[j]], axis=0), NT, *P_ATT) for j in n]
        lmat = [jnp.where(strict, att[j][0:2 * C, 0:2 * C], 0.0) for j in n]
        ak = [jnp.where(strict, att[j][0:2 * C, 2 * C:4 * C], 0.0) for j in n]
        rb = [jnp.where(incl, att[j][2 * C:4 * C, 0:2 * C], 0.0) for j in n]
        rk = [jnp.where(incl, att[j][2 * C:4 * C, 2 * C:4 * C], 0.0) for j in n]
        akv = [_mm(ak[j], v_s[j], NN, *P_APP) for j in n]
        pw = [_mm(lmat[j], lmat[j], NN, *P_INV) for j in n]
        xs = [eye_f + lmat[j] for j in n]
        for _ in range(N_LEVELS - 2):
            z = [_mm(pw[j], jnp.concatenate([xs[j], pw[j]], axis=1), NN, *P_INV) for j in n]
            xs = [xs[j] + z[j][:, 0:LANES] for j in n]
            pw = [z[j][:, LANES:2 * LANES] for j in n]
        xs = [xs[j] + _mm(pw[j], xs[j], NN, *P_INV) for j in n]
        au = [_mm(xs[j], jnp.concatenate([a_s[j], akv[j]], axis=1), NN, *P_APP)
              for j in n]
        ry = [_mm(rb[j], au[j], NN, *P_APP) for j in n]
        rkv = [_mm(rk[j], v_s[j], NN, *P_APP) for j in n]
        mn = [_mm(au[j].T, b_s[j], NN, *P_APP) for j in n]
        vk = [_mm(v_s[j].T, k_s[j], NN, *P_APP) for j in n]
        for j, (g, p) in enumerate(items):
            idx = (i * G + g) * n_pairs + p
            glast = e_in[(g + 1) * C - 1:(g + 1) * C, p * LANES:(p + 1) * LANES]
            rp_ref[idx] = r_s[j] + ry[j][:, 0:LANES]
            y0_ref[idx] = ry[j][:, LANES:2 * LANES] + rkv[j]
            mx_ref[idx] = (eye_f + mn[j][0:LANES]) * glast
            nx_ref[idx] = (mn[j][LANES:2 * LANES] + vk[j]) * glast
        return carry

    lax.fori_loop(0, TB // R, prep, 0)

    def scan(i, carry):
        t0 = pl.multiple_of(i * C, C)
        pairs = range(n_pairs)
        st = [s_ref[p] for p in pairs]
        ys = [_mm(rp_ref[i * n_pairs + p], st[p], NT, *P_STATE) for p in pairs]
        sn = [_mm(st[p], mx_ref[i * n_pairs + p], NN, *P_STATE) for p in pairs]
        for p in pairs:
            s_ref[p] = sn[p] + nx_ref[i * n_pairs + p]
        y_ref[pl.ds(t0, C), :] = jnp.concatenate(
            [_unstack(ys[p] + y0_ref[i * n_pairs + p]) for p in pairs], axis=1)
        return carry

    lax.fori_loop(0, TB // C, scan, 0)

    y = y_ref[...]
    mean = _segsum(y, g01) * (1.0 / HEAD)
    d = y - mean
    var = _segsum(d * d, g01) * (1.0 / HEAD)
    yn = d * lax.rsqrt(var + GN_EPS) * lnx_w + lnx_b
    o_ref[0] = ((yn + bonus_ref[...]) * gate_ref[...]).astype(o_ref.dtype)


def _rwkv(pc, mu, vecs, lora_w, g_up, tri, g01, tb):
    B, T, PC = pc.shape
    W = vecs.shape[1]
    n_mats = (tb // CHUNK) * (W // LANES)
    mats = pltpu.VMEM((n_mats, LANES, LANES), F32)
    return pl.pallas_call(
        _rwkv_kernel,
        grid=(B, T // tb),
        in_specs=[
            pl.BlockSpec((1, tb, PC), lambda b, t: (b, t, 0)),
            pl.BlockSpec(mu.shape, lambda b, t: (0, 0)),
            pl.BlockSpec(vecs.shape, lambda b, t: (0, 0)),
            pl.BlockSpec(lora_w.shape, lambda b, t: (0, 0)),
            pl.BlockSpec(g_up.shape, lambda b, t: (0, 0)),
            pl.BlockSpec(tri.shape, lambda b, t: (0, 0)),
            pl.BlockSpec(g01.shape, lambda b, t: (0, 0)),
        ],
        out_specs=pl.BlockSpec((1, tb, W), lambda b, t: (b, t, 0)),
        out_shape=jax.ShapeDtypeStruct((B, T, W), BF16),
        scratch_shapes=[pltpu.VMEM((W // LANES, LANES, LANES), F32),
                        pltpu.VMEM((8, PC), F32),
                        mats, mats, mats, mats,
                        pltpu.VMEM((tb, W), F32), pltpu.VMEM((tb, W), F32),
                        pltpu.VMEM((tb, W), F32)],
        compiler_params=pltpu.CompilerParams(
            dimension_semantics=("parallel", "arbitrary"), vmem_limit_bytes=VMEM_LIMIT),
        name="rwkv7",
    )(pc, mu, vecs, lora_w, g_up, tri, g01)


def _ffn_kernel(final, widths, x_ref, ya_ref, yb_ref, yc_ref, mod_ref, g_ref, wo_ref,
                wg_ref, wu_ref, wd_ref, fg_ref, o_ref):
    x = x_ref[0]
    acc = None
    off = 0
    for ref, wd in zip((ya_ref, yb_ref, yc_ref), widths):
        d = jnp.dot(ref[0], wo_ref[off:off + wd, :], preferred_element_type=F32)
        acc = d if acc is None else acc + d
        off += wd
    x1 = x + mod_ref[0, 2:3, :] * acc
    ms = jnp.mean(x1 * x1, axis=-1, keepdims=True)
    h = x1 * lax.rsqrt(ms + NORM_EPS) * g_ref[...]
    h = h * (1.0 + mod_ref[0, 4:5, :]) + mod_ref[0, 3:4, :]
    hb = h.astype(BF16)
    gt = jnp.dot(hb, wg_ref[...], preferred_element_type=F32)
    up = jnp.dot(hb, wu_ref[...], preferred_element_type=F32)
    act = (gt * _sigmoid(gt) * up).astype(BF16)
    dn = jnp.dot(act, wd_ref[...], preferred_element_type=F32)
    x2 = x1 + mod_ref[0, 5:6, :] * dn
    if final:
        ms2 = jnp.mean(x2 * x2, axis=-1, keepdims=True)
        x2 = x2 * lax.rsqrt(ms2 + NORM_EPS) * fg_ref[...]
    o_ref[0] = x2


def _out_ffn(x, ya, yb, yc, mod_l, gain, wo, wg, wu, wd, final_g, final, tm):
    B, T, D = x.shape
    widths = (ya.shape[2], yb.shape[2], yc.shape[2])
    const = lambda shape: pl.BlockSpec(shape, lambda b, t: (0, 0), pipeline_mode=pl.Buffered(1))
    return pl.pallas_call(
        functools.partial(_ffn_kernel, final, widths),
        grid=(B, T // tm),
        in_specs=[
            pl.BlockSpec((1, tm, D), lambda b, t: (b, t, 0)),
            pl.BlockSpec((1, tm, widths[0]), lambda b, t: (b, t, 0)),
            pl.BlockSpec((1, tm, widths[1]), lambda b, t: (b, t, 0)),
            pl.BlockSpec((1, tm, widths[2]), lambda b, t: (b, t, 0)),
            pl.BlockSpec((1, 6, D), lambda b, t: (b, 0, 0)),
            pl.BlockSpec((1, D), lambda b, t: (0, 0)),
            const(wo.shape), const(wg.shape), const(wu.shape), const(wd.shape),
            pl.BlockSpec((1, D), lambda b, t: (0, 0)),
        ],
        out_specs=pl.BlockSpec((1, tm, D), lambda b, t: (b, t, 0)),
        out_shape=jax.ShapeDtypeStruct((B, T, D), F32),
        compiler_params=pltpu.CompilerParams(
            dimension_semantics=("parallel", "parallel"), vmem_limit_bytes=VMEM_LIMIT),
        name="out_ffn",
    )(x, ya, yb, yc, mod_l, gain, wo, wg, wu, wd, final_g)


def _block_diag(w):
    G, n, _ = w.shape
    eye = jnp.eye(G, dtype=w.dtype)
    return (eye[:, None, :, None] * w[:, :, None, :]).reshape(G * n, G * n)


def _seg_ones(width):
    idx = np.arange(width) // HEAD
    return jnp.asarray((idx[:, None] == idx[None, :]).astype(np.float32), dtype=BF16)


def kernel(x, c, norm1_g, norm2_g, ada_w, ada_b, w_in, conv_w, conv_b, rg_w, rg_b, ig_w, ig_b, lru_lam, hgrn_lb, rwkv_mu, rwkv_w0, rwkv_w_up, rwkv_a0, rwkv_a_up, rwkv_g_up, rwkv_k_k, rwkv_k_a, rwkv_r_k, rwkv_lnx_w, rwkv_lnx_b, mix_beta, w_out, ffn_w_gate, ffn_w_up, ffn_w_down, final_g):
    B, T, D = x.shape
    L = w_in.shape[0]
    WA = conv_w.shape[2]
    WB = hgrn_lb.shape[1]
    WC = rwkv_w0.shape[1]
    widths_p = (2 * WA, 4 * WB, rwkv_mu.shape[1])
    tm = 256

    mod = _modulation(c, ada_w, ada_b).reshape(L, B, 6, D)
    wst = jnp.asarray(_hgrn_weights(), dtype=BF16)
    tri = jnp.asarray(np.kron(np.eye(RWKV_GROUP, dtype=np.float32),
                              np.tril(np.ones((CHUNK, CHUNK), np.float32))), dtype=BF16)
    g_a, g_b, g_c = _seg_ones(WA), _seg_ones(WB), _seg_ones(WC)

    for l in range(L):
        pa, pb, pc = _in_projection(x, mod[l], norm1_g[l][None], w_in[l].astype(BF16),
                                    widths_p, tm)
        beta = mix_beta[l]
        vec_a = jnp.stack([conv_b[l], rg_b[l], ig_b[l], lru_lam[l], beta[:WA],
                           jnp.zeros_like(beta[:WA]), jnp.zeros_like(beta[:WA]),
                           jnp.zeros_like(beta[:WA])])
        wg_a = jnp.concatenate([_block_diag(rg_w[l]), _block_diag(ig_w[l])], axis=1).astype(BF16)
        ya = _rglru(pa, conv_w[l], vec_a, wg_a, g_a)
        yb = _hgrn(pb, hgrn_lb, beta[None, WA:WA + WB], wst, g_b, l, 256)
        vec_c = jnp.stack([rwkv_w0[l], rwkv_a0[l], rwkv_k_k[l], rwkv_k_a[l],
                           rwkv_r_k[l].reshape(-1), rwkv_lnx_w[l], rwkv_lnx_b[l],
                           beta[WA + WB:]])
        zw = jnp.zeros_like(rwkv_w_up[l])
        lora_w = jnp.concatenate(
            [jnp.concatenate([rwkv_w_up[l], zw], axis=1),
             jnp.concatenate([zw, rwkv_a_up[l]], axis=1)], axis=0)
        yc = _rwkv(pc, rwkv_mu[l][None], vec_c, lora_w, rwkv_g_up[l], tri, g_c, 256)
        x = _out_ffn(x, ya, yb, yc, mod[l], norm2_g[l][None], w_out[l].astype(BF16),
                     ffn_w_gate[l].astype(BF16), ffn_w_up[l].astype(BF16),
                     ffn_w_down[l].astype(BF16), final_g[None], l == L - 1, tm)
    return x
```

```python
import functools

import numpy as np
import jax
import jax.numpy as jnp
from jax import lax
from jax.experimental import pallas as pl
from jax.experimental.pallas import tpu as pltpu

F32 = jnp.float32
BF16 = jnp.bfloat16

HEAD = 64
LANES = 128
CONV_W = 4
LRU_C = 8.0
NORM_EPS = 1e-6
GN_EPS = 64e-5
CHUNK = 64
N_LEVELS = 6
RWKV_GROUP = 4
HGRN_GROUP = 2
LORA_W = 64
LORA_A = 64
LORA_G = 128
VMEM_LIMIT = 56 * 1024 * 1024

NN = (((1,), (0,)), ((), ()))
NT = (((1,), (1,)), ((), ()))

P_ATT = (1, 1)
P_INV = (1, 1)
P_APP = (1, 1)
P_STATE = (1, 1)
P_HATT = (1, 1)
P_HAPP = (1, 1)


def _split(x, n):
    if x.dtype == BF16:
        return [x]
    parts = []
    r = x
    for i in range(n):
        p = r.astype(BF16)
        parts.append(p)
        if i + 1 < n:
            r = r - p.astype(F32)
    return parts


def _mm(a, b, dims=NN, na=2, nb=2):
    pa = _split(a, na)
    pb = _split(b, nb)
    order = max(len(pa), len(pb))
    acc = None
    for i, x in enumerate(pa):
        for j, y in enumerate(pb):
            if i + j < order:
                d = lax.dot_general(x, y, dims, preferred_element_type=F32)
                acc = d if acc is None else acc + d
    return acc


def _segsum(x, g01):
    xb = x.astype(BF16)
    return jnp.concatenate(
        [jnp.dot(xb[:, o:o + LANES], g01, preferred_element_type=F32)
         for o in range(0, x.shape[1], LANES)], axis=1)


def _stack(x, m0):
    return jnp.concatenate([jnp.where(m0, x, 0.0), jnp.where(m0, 0.0, x)], axis=0)


def _unstack(xs):
    c = xs.shape[0] // 2
    return xs[:c] + xs[c:]


def _sigmoid(x):
    return 0.5 * jnp.tanh(0.5 * x) + 0.5


def _log1p(u):
    w = 1.0 + u
    return jnp.where(w == 1.0, u, jnp.log(w) * u / (w - 1.0))


def _expm1(x):
    u = jnp.exp(x)
    um1 = u - 1.0
    return jnp.where(u == 1.0, x, jnp.where(um1 == -1.0, -1.0, um1 * x / jnp.log(u)))


def _softplus(x):
    return jnp.maximum(x, 0.0) + _log1p(jnp.exp(-jnp.abs(x)))


def _mod_kernel(c_ref, w_ref, b_ref, o_ref):
    c = c_ref[...]
    cs = c * _sigmoid(c)
    o_ref[0] = _mm(cs, w_ref[0]) + b_ref[0]


def _modulation(c, ada_w, ada_b):
    L, D, N = ada_w.shape
    B = c.shape[0]
    tn = N // 4
    return pl.pallas_call(
        _mod_kernel,
        grid=(L, N // tn),
        in_specs=[
            pl.BlockSpec((B, D), lambda l, j: (0, 0)),
            pl.BlockSpec((1, D, tn), lambda l, j: (l, 0, j)),
            pl.BlockSpec((1, 1, tn), lambda l, j: (l, 0, j)),
        ],
        out_specs=pl.BlockSpec((1, B, tn), lambda l, j: (l, 0, j)),
        out_shape=jax.ShapeDtypeStruct((L, B, N), F32),
        compiler_params=pltpu.CompilerParams(
            dimension_semantics=("parallel", "parallel"), vmem_limit_bytes=VMEM_LIMIT),
        name="modulation",
    )(c, ada_w, ada_b.reshape(L, 1, N))


def _inproj_kernel(widths, x_ref, mod_ref, g_ref, w_ref, pa_ref, pb_ref, pc_ref):
    x = x_ref[0]
    ms = jnp.mean(x * x, axis=-1, keepdims=True)
    h = x * lax.rsqrt(ms + NORM_EPS) * g_ref[...]
    h = h * (1.0 + mod_ref[0, 1:2, :]) + mod_ref[0, 0:1, :]
    hb = h.astype(BF16)
    off = 0
    for ref, wd in zip((pa_ref, pb_ref, pc_ref), widths):
        ref[0] = jnp.dot(hb, w_ref[:, off:off + wd], preferred_element_type=F32)
        off += wd


def _in_projection(x, mod_l, gain, w_bf, widths, tm):
    B, T, D = x.shape
    P = w_bf.shape[1]
    return pl.pallas_call(
        functools.partial(_inproj_kernel, widths),
        grid=(B, T // tm),
        in_specs=[
            pl.BlockSpec((1, tm, D), lambda b, t: (b, t, 0)),
            pl.BlockSpec((1, 6, D), lambda b, t: (b, 0, 0)),
            pl.BlockSpec((1, D), lambda b, t: (0, 0)),
            pl.BlockSpec((D, P), lambda b, t: (0, 0), pipeline_mode=pl.Buffered(1)),
        ],
        out_specs=[pl.BlockSpec((1, tm, wd), lambda b, t: (b, t, 0)) for wd in widths],
        out_shape=[jax.ShapeDtypeStruct((B, T, wd), F32) for wd in widths],
        compiler_params=pltpu.CompilerParams(
            dimension_semantics=("parallel", "parallel"), vmem_limit_bytes=VMEM_LIMIT),
        name="in_projection",
    )(x, mod_l, gain, w_bf)


def _rglru_kernel(tc, pa_ref, cw_ref, vec_ref, wg_ref, g_ref, o_ref):
    T = pa_ref.shape[1]
    W = o_ref.shape[2]
    conv_b = vec_ref[0:1, :]
    rg_b = vec_ref[1:2, :]
    ig_b = vec_ref[2:3, :]
    sp_lam = _softplus(-vec_ref[3:4, :])
    beta = vec_ref[4:5, :]
    row = lax.broadcasted_iota(jnp.int32, (tc, W), 0)
    g01 = g_ref[...]
    wg = wg_ref[...]

    def body(i, hprev):
        t0 = pl.multiple_of(i * tc, tc)
        xa = pa_ref[0, pl.ds(t0, tc), 0:W]
        ya = pa_ref[0, pl.ds(t0, tc), W:2 * W]
        tp = pl.multiple_of(jnp.maximum(t0 - 8, 0), 8)
        prev8 = jnp.where(i > 0, pa_ref[0, pl.ds(tp, 8), 0:W], 0.0)
        win = jnp.concatenate([prev8, xa], axis=0)
        u = conv_b + cw_ref[CONV_W - 1:CONV_W, :] * xa
        for j in range(1, CONV_W):
            u = u + cw_ref[CONV_W - 1 - j:CONV_W - j, :] * pltpu.roll(win, j, 0)[8:]
        z = jnp.dot(u.astype(BF16), wg, preferred_element_type=F32)
        r = _sigmoid(z[:, 0:W] + rg_b)
        ig = _sigmoid(z[:, W:2 * W] + ig_b)
        log_a = (-LRU_C) * r * sp_lam
        a = jnp.exp(log_a)
        mult = jnp.sqrt(-_expm1(2.0 * log_a))
        mult = jnp.where(row + t0 == 0, 1.0, mult)
        bv = mult * (ig * u)
        s = 1
        while s < tc:
            keep = row >= s
            a_s = jnp.where(keep, pltpu.roll(a, s, 0), 1.0)
            b_s = jnp.where(keep, pltpu.roll(bv, s, 0), 0.0)
            bv = a * b_s + bv
            a = a * a_s
            s *= 2
        h = a * hprev + bv
        y = h * jax.nn.gelu(ya)
        ms = _segsum(y * y, g01) * (1.0 / HEAD)
        o_ref[0, pl.ds(t0, tc), :] = (y * lax.rsqrt(ms + NORM_EPS) * beta).astype(o_ref.dtype)
        return h[tc - 1:tc, :]

    lax.fori_loop(0, T // tc, body, jnp.zeros((1, W), F32))


def _rglru(pa, conv_w, vecs, wg_bf, g01, tc=128):
    B, T, W2 = pa.shape
    W = W2 // 2
    return pl.pallas_call(
        functools.partial(_rglru_kernel, tc),
        grid=(B,),
        in_specs=[
            pl.BlockSpec((1, T, W2), lambda b: (b, 0, 0)),
            pl.BlockSpec(conv_w.shape, lambda b: (0, 0)),
            pl.BlockSpec(vecs.shape, lambda b: (0, 0)),
            pl.BlockSpec(wg_bf.shape, lambda b: (0, 0)),
            pl.BlockSpec(g01.shape, lambda b: (0, 0)),
        ],
        out_specs=pl.BlockSpec((1, T, W), lambda b: (b, 0, 0)),
        out_shape=jax.ShapeDtypeStruct((B, T, W), BF16),
        compiler_params=pltpu.CompilerParams(
            dimension_semantics=("parallel",), vmem_limit_bytes=VMEM_LIMIT),
        name="rglru",
    )(pa, conv_w, vecs, wg_bf, g01)


def _hgrn_kernel(layer, pb_ref, lb_ref, beta_ref, wst_ref, g_ref, o_ref,
                 s_ref, qin_ref, oi_ref, kv_ref, dec_ref):
    C = CHUNK
    R = wst_ref.shape[1]
    G = R // C
    TB = pb_ref.shape[1]
    W = o_ref.shape[2]
    n_pairs = W // LANES

    @pl.when(pl.program_id(1) == 0)
    def _():
        s_ref[...] = jnp.zeros_like(s_ref)

    lbr = lb_ref[...]
    n_layers = lbr.shape[0]
    mx = lbr[0:1, :]
    for i in range(1, n_layers):
        mx = jnp.maximum(mx, lbr[i:i + 1, :])
    es = [jnp.exp(lbr[i:i + 1, :] - mx) for i in range(n_layers)]
    tot = es[0]
    for i in range(1, n_layers):
        tot = tot + es[i]
    sm = [e / tot for e in es]
    cum = sm[0]
    for i in range(1, layer + 1):
        cum = cum + sm[i]
    lb = cum - sm[0]
    log_lb = jnp.log(lb)
    log_1mlb = _log1p(-lb)

    beta = beta_ref[...]
    g01 = g_ref[...]
    wst = wst_ref[...]
    lane = lax.broadcasted_iota(jnp.int32, (C, LANES), 1)
    m0 = lane < HEAD
    r2 = lax.broadcasted_iota(jnp.int32, (2 * C, 2 * C), 0)
    c2 = lax.broadcasted_iota(jnp.int32, (2 * C, 2 * C), 1)
    eye = r2 == c2
    level = []
    for lev in range(N_LEVELS):
        rh = r2 >> (N_LEVELS - 1 - lev)
        ch = c2 >> (N_LEVELS - 1 - lev)
        level.append((rh == ch + 1) & ((ch & 1) == 0))

    def prep(i, carry):
        t0 = pl.multiple_of(i * R, R)
        q = pb_ref[0, pl.ds(t0, R), 0:W] * (HEAD ** -0.5)
        fl = pb_ref[0, pl.ds(t0, R), W:2 * W]
        v = pb_ref[0, pl.ds(t0, R), 2 * W:3 * W]
        lc = log_1mlb + (jnp.minimum(fl, 0.0) - _log1p(jnp.exp(-jnp.abs(fl))))
        log_f = jnp.maximum(log_lb, lc) + _log1p(jnp.exp(-jnp.abs(log_lb - lc)))
        kg = -_expm1(log_f)
        ex = jnp.exp(_mm(wst, log_f, NN, na=1, nb=2))
        qin_ref[pl.ds(t0, R), :] = q * ex[0:R]
        k_dec = kg * ex[R:2 * R]
        items = [(g, p) for g in range(G) for p in range(n_pairs)]
        n = range(len(items))

        def piece(x, g, p, base=0):
            return x[base + g * C:base + (g + 1) * C, p * LANES:(p + 1) * LANES]

        q_s = [_stack(piece(q, g, p), m0) for g, p in items]
        k_s = [_stack(piece(kg, g, p), m0) for g, p in items]
        v_s = [_stack(piece(v, g, p), m0) for g, p in items]
        amat = [jnp.where(eye, _mm(q_s[j], k_s[j], NT, *P_HATT), 0.0) for j in n]
        for lev in range(N_LEVELS):
            es = [piece(ex, g, p, (2 + lev) * R) for g, p in items]
            es = [jnp.concatenate([e, e], axis=0) for e in es]
            amat = [jnp.where(level[lev], _mm(q_s[j] * es[j], k_s[j] * es[j], NT, *P_HATT),
                              amat[j]) for j in n]
        oi = [_unstack(_mm(amat[j], v_s[j], NN, *P_HAPP)) for j in n]
        kv = [_mm(v_s[j].T, _stack(piece(k_dec, g, p), m0), NN, *P_HAPP)
              for j, (g, p) in enumerate(items)]
        for g in range(G):
            oi_ref[pl.ds(t0 + g * C, C), :] = jnp.concatenate(
                [oi[g * n_pairs + p] for p in range(n_pairs)], axis=1)
            dec_ref[i * G + g] = jnp.broadcast_to(ex[(g + 1) * C - 1:(g + 1) * C, :], (8, W))
        for j, (g, p) in enumerate(items):
            kv_ref[(i * G + g) * n_pairs + p] = kv[j]
        return carry

    lax.fori_loop(0, TB // R, prep, 0)

    def scan(i, carry):
        t0 = pl.multiple_of(i * C, C)
        dec = dec_ref[i]
        outs = []
        for p in range(n_pairs):
            sl = slice(p * LANES, (p + 1) * LANES)
            st = s_ref[p]
            q_s = _stack(qin_ref[pl.ds(t0, C), sl], m0)
            outs.append(_unstack(_mm(q_s, st, NT, *P_HAPP)))
            s_ref[p] = st * dec[0:1, sl] + kv_ref[i * n_pairs + p]
        oi_ref[pl.ds(t0, C), :] = oi_ref[pl.ds(t0, C), :] + jnp.concatenate(outs, axis=1)
        return carry

    lax.fori_loop(0, TB // C, scan, 0)

    o = oi_ref[...]
    g = pb_ref[0, :, 3 * W:4 * W]
    ms = _segsum(o * o, g01) * (1.0 / HEAD)
    o = o * lax.rsqrt(ms + NORM_EPS) * (g * _sigmoid(g)) * beta
    o_ref[0] = o.astype(o_ref.dtype)


def _hgrn_weights(group):
    C = CHUNK
    R = group * C
    w = np.zeros(((2 + N_LEVELS) * R, R), np.float32)
    for g in range(group):
        o = g * C
        for r in range(C):
            w[o + r, o:o + r + 1] = 1.0
            w[R + o + r, o + r + 1:o + C] = 1.0
            for lev in range(N_LEVELS):
                n = C >> lev
                m = (r // n) * n + n // 2 - 1
                if r % n >= n // 2:
                    w[(2 + lev) * R + o + r, o + m + 1:o + r + 1] = 1.0
                else:
                    w[(2 + lev) * R + o + r, o + r + 1:o + m + 1] = 1.0
    return w


def _hgrn(pb, hgrn_lb, beta_b, wst, g01, layer, tb):
    B, T, W4 = pb.shape
    W = W4 // 4
    return pl.pallas_call(
        functools.partial(_hgrn_kernel, layer),
        grid=(B, T // tb),
        in_specs=[
            pl.BlockSpec((1, tb, W4), lambda b, t: (b, t, 0)),
            pl.BlockSpec(hgrn_lb.shape, lambda b, t: (0, 0)),
            pl.BlockSpec(beta_b.shape, lambda b, t: (0, 0)),
            pl.BlockSpec(wst.shape, lambda b, t: (0, 0)),
            pl.BlockSpec(g01.shape, lambda b, t: (0, 0)),
        ],
        out_specs=pl.BlockSpec((1, tb, W), lambda b, t: (b, t, 0)),
        out_shape=jax.ShapeDtypeStruct((B, T, W), BF16),
        scratch_shapes=[pltpu.VMEM((W // LANES, LANES, LANES), F32),
                        pltpu.VMEM((tb, W), F32), pltpu.VMEM((tb, W), F32),
                        pltpu.VMEM(((tb // CHUNK) * (W // LANES), LANES, LANES), F32),
                        pltpu.VMEM((tb // CHUNK, 8, W), F32)],
        compiler_params=pltpu.CompilerParams(
            dimension_semantics=("parallel", "arbitrary"), vmem_limit_bytes=VMEM_LIMIT),
        name="hgrn2",
    )(pb, hgrn_lb, beta_b, wst, g01)


def _rwkv_kernel(pc_ref, mu_ref, vec_ref, lora_ref, gup_ref, tri_ref, g_ref, o_ref,
                 s_ref, prev_ref, rp_ref, y0_ref, mx_ref, nx_ref, bonus_ref, gate_ref, y_ref):
    C = CHUNK
    R = tri_ref.shape[0]
    G = R // C
    TB = pc_ref.shape[1]
    W = o_ref.shape[2]
    n_pairs = W // LANES

    @pl.when(pl.program_id(1) == 0)
    def _():
        s_ref[...] = jnp.zeros_like(s_ref)
        prev_ref[...] = jnp.zeros_like(prev_ref)

    mu = mu_ref[...]
    w0 = vec_ref[0:1, :]
    a0 = vec_ref[1:2, :]
    k_k = vec_ref[2:3, :]
    k_a = vec_ref[3:4, :]
    r_k = vec_ref[4:5, :]
    lnx_w = vec_ref[5:6, :]
    lnx_b = vec_ref[6:7, :]
    beta = vec_ref[7:8, :]
    lora = _split(lora_ref[...], 2)
    gup = _split(gup_ref[...], 2)
    tri = tri_ref[...]
    g01 = g_ref[...]
    lane = lax.broadcasted_iota(jnp.int32, (C, LANES), 1)
    lane_r = lax.broadcasted_iota(jnp.int32, (R, LANES), 1)
    m0 = lane < HEAD
    r2 = lax.broadcasted_iota(jnp.int32, (2 * C, 2 * C), 0)
    c2 = lax.broadcasted_iota(jnp.int32, (2 * C, 2 * C), 1)
    strict = r2 > c2
    incl = r2 >= c2
    eye_f = jnp.where(r2 == c2, 1.0, 0.0)
    o_w = 3 * W
    o_g = o_w + LORA_W + LORA_A

    def mm_parts(x, parts):
        xs = _split(x, 2)
        return (lax.dot_general(xs[0], parts[0], NN, preferred_element_type=F32)
                + lax.dot_general(xs[0], parts[1], NN, preferred_element_type=F32)
                + lax.dot_general(xs[1], parts[0], NN, preferred_element_type=F32))

    def prep(i, carry):
        t0 = pl.multiple_of(i * R, R)
        pc = pc_ref[0, pl.ds(t0, R), :]
        win = jnp.concatenate([prev_ref[...], pc], axis=0)
        prev = pltpu.roll(win, 1, 0)[8:]
        prev_ref[...] = pc[R - 8:R, :]
        ps = pc + (prev - pc) * mu
        r = ps[:, 0:W]
        k = ps[:, W:2 * W]
        v = ps[:, 2 * W:3 * W]
        z = ps[:, o_w:o_g]
        z = jnp.where(lane_r < LORA_W, jnp.tanh(z), z)
        lo = mm_parts(z, lora)
        w_raw = -_softplus(-(w0 + lo[:, 0:W])) - 0.5
        lw = -jnp.exp(w_raw)
        a_sig = _sigmoid(a0 + lo[:, W:2 * W])
        gate = mm_parts(_sigmoid(ps[:, o_g:o_g + LORA_G]), gup)
        kk = k * k_k
        kk = kk * lax.rsqrt(_segsum(kk * kk, g01) + 1e-12)
        k2 = k * (1.0 + (a_sig - 1.0) * k_a)
        gam = _mm(tri, lw, NN, na=1, nb=3)
        e_in = jnp.exp(gam)
        e_ex = jnp.exp(gam - lw)
        e_neg = jnp.exp(-gam)
        at = -kk * e_ex
        rt = r * e_in
        bt = kk * a_sig * e_neg
        kt = k2 * e_neg
        bonus_ref[pl.ds(t0, R), :] = _segsum(r * k2 * r_k, g01) * v
        gate_ref[pl.ds(t0, R), :] = gate * beta
        items = [(g, p) for g in range(G) for p in range(n_pairs)]
        n = range(len(items))

        def stacked(x):
            return [_stack(x[g * C:(g + 1) * C, p * LANES:(p + 1) * LANES], m0)
                    for g, p in items]

        a_s, r_s, b_s, k_s, v_s = stacked(at), stacked(rt), stacked(bt), stacked(kt), stacked(v)
        att = [_mm(jnp.concatenate([a_s[j], r_s[j]], axis=0),
                   jnp.concatenate([b_s[j], k_s[j]], axis=0), NT, *P_ATT) for j in n]
        lmat = [jnp.where(strict, att[j][0:2 * C, 0:2 * C], 0.0) for j in n]
        ak = [jnp.where(strict, att[j][0:2 * C, 2 * C:4 * C], 0.0) for j in n]
        rb = [jnp.where(incl, att[j][2 * C:4 * C, 0:2 * C], 0.0) for j in n]
        rk = [jnp.where(incl, att[j][2 * C:4 * C, 2 * C:4 * C], 0.0) for j in n]
        akv = [_mm(ak[j], v_s[j], NN, *P_APP) for j in n]
        pw = [_mm(lmat[j], lmat[j], NN, *P_INV) for j in n]
        xs = [eye_f + lmat[j] for j in n]
        for _ in range(N_LEVELS - 2):
            z = [_mm(pw[j], jnp.concatenate([xs[j], pw[j]], axis=1), NN, *P_INV) for j in n]
            xs = [xs[j] + z[j][:, 0:LANES] for j in n]
            pw = [z[j][:, LANES:2 * LANES] for j in n]
        xs = [xs[j] + _mm(pw[j], xs[j], NN, *P_INV) for j in n]
        au = [_mm(xs[j], jnp.concatenate([a_s[j], akv[j]], axis=1), NN, *P_APP)
              for j in n]
        ry = [_mm(rb[j], au[j], NN, *P_APP) for j in n]
        rkv = [_mm(rk[j], v_s[j], NN, *P_APP) for j in n]
        mn = [_mm(au[j].T, b_s[j], NN, *P_APP) for j in n]
        vk = [_mm(v_s[j].T, k_s[j], NN, *P_APP) for j in n]
        for j, (g, p) in enumerate(items):
            idx = (i * G + g) * n_pairs + p
            glast = e_in[(g + 1) * C - 1:(g + 1) * C, p * LANES:(p + 1) * LANES]
            rp_ref[idx] = r_s[j] + ry[j][:, 0:LANES]
            y0_ref[idx] = ry[j][:, LANES:2 * LANES] + rkv[j]
            mx_ref[idx] = (eye_f + mn[j][0:LANES]) * glast
            nx_ref[idx] = (mn[j][LANES:2 * LANES] + vk[j]) * glast
        return carry

    lax.fori_loop(0, TB // R, prep, 0)

    def scan(i, carry):
        t0 = pl.multiple_of(i * C, C)
        pairs = range(n_pairs)
        st = [s_ref[p] for p in pairs]
        ys = [_mm(rp_ref[i * n_pairs + p], st[p], NT, *P_STATE) for p in pairs]
        sn = [_mm(st[p], mx_ref[i * n_pairs + p], NN, *P_STATE) for p in pairs]
        for p in pairs:
            s_ref[p] = sn[p] + nx_ref[i * n_pairs + p]
        y_ref[pl.ds(t0, C), :] = jnp.concatenate(
            [_unstack(ys[p] + y0_ref[i * n_pairs + p]) for p in pairs], axis=1)
        return carry

    lax.fori_loop(0, TB // C, scan, 0)

    y = y_ref[...]
    mean = _segsum(y, g01) * (1.0 / HEAD)
    d = y - mean
    var = _segsum(d * d, g01) * (1.0 / HEAD)
    yn = d * lax.rsqrt(var + GN_EPS) * lnx_w + lnx_b
    o_ref[0] = ((yn + bonus_ref[...]) * gate_ref[...]).astype(o_ref.dtype)


def _rwkv(pc, mu, vecs, lora_w, g_up, tri, g01, tb):
    B, T, PC = pc.shape
    W = vecs.shape[1]
    n_mats = (tb // CHUNK) * (W // LANES)
    mats = pltpu.VMEM((n_mats, LANES, LANES), F32)
    return pl.pallas_call(
        _rwkv_kernel,
        grid=(B, T // tb),
        in_specs=[
            pl.BlockSpec((1, tb, PC), lambda b, t: (b, t, 0)),
            pl.BlockSpec(mu.shape, lambda b, t: (0, 0)),
            pl.BlockSpec(vecs.shape, lambda b, t: (0, 0)),
            pl.BlockSpec(lora_w.shape, lambda b, t: (0, 0)),
            pl.BlockSpec(g_up.shape, lambda b, t: (0, 0)),
            pl.BlockSpec(tri.shape, lambda b, t: (0, 0)),
            pl.BlockSpec(g01.shape, lambda b, t: (0, 0)),
        ],
        out_specs=pl.BlockSpec((1, tb, W), lambda b, t: (b, t, 0)),
        out_shape=jax.ShapeDtypeStruct((B, T, W), BF16),
        scratch_shapes=[pltpu.VMEM((W // LANES, LANES, LANES), F32),
                        pltpu.VMEM((8, PC), F32),
                        mats, mats, mats, mats,
                        pltpu.VMEM((tb, W), F32), pltpu.VMEM((tb, W), F32),
                        pltpu.VMEM((tb, W), F32)],
        compiler_params=pltpu.CompilerParams(
            dimension_semantics=("parallel", "arbitrary"), vmem_limit_bytes=VMEM_LIMIT),
        name="rwkv7",
    )(pc, mu, vecs, lora_w, g_up, tri, g01)


def _ffn_kernel(final, widths, x_ref, ya_ref, yb_ref, yc_ref, mod_ref, g_ref, wo_ref,
                wg_ref, wu_ref, wd_ref, fg_ref, o_ref):
    x = x_ref[0]
    acc = None
    off = 0
    for ref, wd in zip((ya_ref, yb_ref, yc_ref), widths):
        d = jnp.dot(ref[0], wo_ref[off:off + wd, :], preferred_element_type=F32)
        acc = d if acc is None else acc + d
        off += wd
    x1 = x + mod_ref[0, 2:3, :] * acc
    ms = jnp.mean(x1 * x1, axis=-1, keepdims=True)
    h = x1 * lax.rsqrt(ms + NORM_EPS) * g_ref[...]
    h = h * (1.0 + mod_ref[0, 4:5, :]) + mod_ref[0, 3:4, :]
    hb = h.astype(BF16)
    gt = jnp.dot(hb, wg_ref[...], preferred_element_type=F32)
    up = jnp.dot(hb, wu_ref[...], preferred_element_type=F32)
    act = (gt * _sigmoid(gt) * up).astype(BF16)
    dn = jnp.dot(act, wd_ref[...], preferred_element_type=F32)
    x2 = x1 + mod_ref[0, 5:6, :] * dn
    if final:
        ms2 = jnp.mean(x2 * x2, axis=-1, keepdims=True)
        x2 = x2 * lax.rsqrt(ms2 + NORM_EPS) * fg_ref[...]
    o_ref[0] = x2


def _out_ffn(x, ya, yb, yc, mod_l, gain, wo, wg, wu, wd, final_g, final, tm):
    B, T, D = x.shape
    widths = (ya.shape[2], yb.shape[2], yc.shape[2])
    const = lambda shape: pl.BlockSpec(shape, lambda b, t: (0, 0), pipeline_mode=pl.Buffered(1))
    return pl.pallas_call(
        functools.partial(_ffn_kernel, final, widths),
        grid=(B, T // tm),
        in_specs=[
            pl.BlockSpec((1, tm, D), lambda b, t: (b, t, 0)),
            pl.BlockSpec((1, tm, widths[0]), lambda b, t: (b, t, 0)),
            pl.BlockSpec((1, tm, widths[1]), lambda b, t: (b, t, 0)),
            pl.BlockSpec((1, tm, widths[2]), lambda b, t: (b, t, 0)),
            pl.BlockSpec((1, 6, D), lambda b, t: (b, 0, 0)),
            pl.BlockSpec((1, D), lambda b, t: (0, 0)),
            const(wo.shape), const(wg.shape), const(wu.shape), const(wd.shape),
            pl.BlockSpec((1, D), lambda b, t: (0, 0)),
        ],
        out_specs=pl.BlockSpec((1, tm, D), lambda b, t: (b, t, 0)),
        out_shape=jax.ShapeDtypeStruct((B, T, D), F32),
        compiler_params=pltpu.CompilerParams(
            dimension_semantics=("parallel", "parallel"), vmem_limit_bytes=VMEM_LIMIT),
        name="out_ffn",
    )(x, ya, yb, yc, mod_l, gain, wo, wg, wu, wd, final_g)


def _block_diag(w):
    G, n, _ = w.shape
    eye = jnp.eye(G, dtype=w.dtype)
    return (eye[:, None, :, None] * w[:, :, None, :]).reshape(G * n, G * n)


def _seg_ones():
    idx = np.arange(LANES) // HEAD
    return jnp.asarray((idx[:, None] == idx[None, :]).astype(np.float32), dtype=BF16)


def kernel(x, c, norm1_g, norm2_g, ada_w, ada_b, w_in, conv_w, conv_b, rg_w, rg_b, ig_w, ig_b, lru_lam, hgrn_lb, rwkv_mu, rwkv_w0, rwkv_w_up, rwkv_a0, rwkv_a_up, rwkv_g_up, rwkv_k_k, rwkv_k_a, rwkv_r_k, rwkv_lnx_w, rwkv_lnx_b, mix_beta, w_out, ffn_w_gate, ffn_w_up, ffn_w_down, final_g):
    B, T, D = x.shape
    L = w_in.shape[0]
    WA = conv_w.shape[2]
    WB = hgrn_lb.shape[1]
    WC = rwkv_w0.shape[1]
    widths_p = (2 * WA, 4 * WB, rwkv_mu.shape[1])
    tm = 256

    mod = _modulation(c, ada_w, ada_b).reshape(L, B, 6, D)
    wst = jnp.asarray(_hgrn_weights(HGRN_GROUP), dtype=BF16)
    tri = jnp.asarray(np.kron(np.eye(RWKV_GROUP, dtype=np.float32),
                              np.tril(np.ones((CHUNK, CHUNK), np.float32))), dtype=BF16)
    g_a = g_b = g_c = _seg_ones()

    for l in range(L):
        pa, pb, pc = _in_projection(x, mod[l], norm1_g[l][None], w_in[l].astype(BF16),
                                    widths_p, tm)
        beta = mix_beta[l]
        vec_a = jnp.stack([conv_b[l], rg_b[l], ig_b[l], lru_lam[l], beta[:WA],
                           jnp.zeros_like(beta[:WA]), jnp.zeros_like(beta[:WA]),
                           jnp.zeros_like(beta[:WA])])
        wg_a = jnp.concatenate([_block_diag(rg_w[l]), _block_diag(ig_w[l])], axis=1).astype(BF16)
        ya = _rglru(pa, conv_w[l], vec_a, wg_a, g_a)
        yb = _hgrn(pb, hgrn_lb, beta[None, WA:WA + WB], wst, g_b, l, 256)
        vec_c = jnp.stack([rwkv_w0[l], rwkv_a0[l], rwkv_k_k[l], rwkv_k_a[l],
                           rwkv_r_k[l].reshape(-1), rwkv_lnx_w[l], rwkv_lnx_b[l],
                           beta[WA + WB:]])
        zw = jnp.zeros_like(rwkv_w_up[l])
        lora_w = jnp.concatenate(
            [jnp.concatenate([rwkv_w_up[l], zw], axis=1),
             jnp.concatenate([zw, rwkv_a_up[l]], axis=1)], axis=0)
        yc = _rwkv(pc, rwkv_mu[l][None], vec_c, lora_w, rwkv_g_up[l], tri, g_c, 256)
        x = _out_ffn(x, ya, yb, yc, mod[l], norm2_g[l][None], w_out[l].astype(BF16),
                     ffn_w_gate[l].astype(BF16), ffn_w_up[l].astype(BF16),
                     ffn_w_down[l].astype(BF16), final_g[None], l == L - 1, tm)
    return x
```

```python
import functools

import numpy as np
import jax
import jax.numpy as jnp
from jax import lax
from jax.experimental import pallas as pl
from jax.experimental.pallas import tpu as pltpu

F32 = jnp.float32
BF16 = jnp.bfloat16

HEAD = 64
HEAD_BITS = 6
LANES = 128
CONV_W = 4
LRU_C = 8.0
NORM_EPS = 1e-6
GN_EPS = 64e-5
CHUNK = 64
N_LEVELS = 6
RWKV_GROUP = 2
SCAN_EVERY = 4
MIXER_ROWS = 512
HGRN_GROUP = 2
LORA_W = 64
LORA_A = 64
LORA_G = 128
VMEM_LIMIT = 56 * 1024 * 1024

NN = (((1,), (0,)), ((), ()))
NT = (((1,), (1,)), ((), ()))

P_ATT = (1, 1)
P_INV = (1, 1)
P_APP = (1, 1)
P_STATE = (1, 1)
P_HATT = (1, 1)
P_HAPP = (1, 1)


def _split(x, n):
    if x.dtype == BF16:
        return [x]
    parts = []
    r = x
    for i in range(n):
        p = r.astype(BF16)
        parts.append(p)
        if i + 1 < n:
            r = r - p.astype(F32)
    return parts


def _mm(a, b, dims=NN, na=2, nb=2):
    pa = _split(a, na)
    pb = _split(b, nb)
    order = max(len(pa), len(pb))
    acc = None
    for i, x in enumerate(pa):
        for j, y in enumerate(pb):
            if i + j < order:
                d = lax.dot_general(x, y, dims, preferred_element_type=F32)
                acc = d if acc is None else acc + d
    return acc


def _segsum(x, g01):
    xb = x.astype(BF16)
    return jnp.concatenate(
        [jnp.dot(xb[:, o:o + LANES], g01, preferred_element_type=F32)
         for o in range(0, x.shape[1], LANES)], axis=1)


def _stack(x, m0):
    return jnp.concatenate([jnp.where(m0, x, 0.0), jnp.where(m0, 0.0, x)], axis=0)


def _unstack(xs):
    c = xs.shape[0] // 2
    return xs[:c] + xs[c:]


def _sigmoid(x):
    return 0.5 * jnp.tanh(0.5 * x) + 0.5


def _log1p(u):
    w = 1.0 + u
    return jnp.where(w == 1.0, u, jnp.log(w) * u / (w - 1.0))


def _expm1(x):
    u = jnp.exp(x)
    um1 = u - 1.0
    return jnp.where(u == 1.0, x, jnp.where(um1 == -1.0, -1.0, um1 * x / jnp.log(u)))


def _softplus(x):
    return jnp.maximum(x, 0.0) + _log1p(jnp.exp(-jnp.abs(x)))


def _mod_kernel(c_ref, w_ref, b_ref, o_ref):
    c = c_ref[...]
    cs = c * _sigmoid(c)
    o_ref[0] = _mm(cs, w_ref[0]) + b_ref[0]


def _modulation(c, ada_w, ada_b):
    L, D, N = ada_w.shape
    B = c.shape[0]
    tn = N // 4
    return pl.pallas_call(
        _mod_kernel,
        grid=(L, N // tn),
        in_specs=[
            pl.BlockSpec((B, D), lambda l, j: (0, 0)),
            pl.BlockSpec((1, D, tn), lambda l, j: (l, 0, j)),
            pl.BlockSpec((1, 1, tn), lambda l, j: (l, 0, j)),
        ],
        out_specs=pl.BlockSpec((1, B, tn), lambda l, j: (l, 0, j)),
        out_shape=jax.ShapeDtypeStruct((L, B, N), F32),
        compiler_params=pltpu.CompilerParams(
            dimension_semantics=("parallel", "parallel"), vmem_limit_bytes=VMEM_LIMIT),
        name="modulation",
    )(c, ada_w, ada_b.reshape(L, 1, N))


def _inproj_kernel(widths, x_ref, mod_ref, g_ref, w_ref, pa_ref, pb_ref, pc_ref):
    x = x_ref[0]
    ms = jnp.mean(x * x, axis=-1, keepdims=True)
    h = x * lax.rsqrt(ms + NORM_EPS) * g_ref[...]
    h = h * (1.0 + mod_ref[0, 1:2, :]) + mod_ref[0, 0:1, :]
    hb = h.astype(BF16)
    off = 0
    for ref, wd in zip((pa_ref, pb_ref, pc_ref), widths):
        ref[0] = jnp.dot(hb, w_ref[:, off:off + wd], preferred_element_type=F32)
        off += wd


def _in_projection(x, mod_l, gain, w_bf, widths, tm):
    B, T, D = x.shape
    P = w_bf.shape[1]
    return pl.pallas_call(
        functools.partial(_inproj_kernel, widths),
        grid=(B, T // tm),
        in_specs=[
            pl.BlockSpec((1, tm, D), lambda b, t: (b, t, 0)),
            pl.BlockSpec((1, 6, D), lambda b, t: (b, 0, 0)),
            pl.BlockSpec((1, D), lambda b, t: (0, 0)),
            pl.BlockSpec((D, P), lambda b, t: (0, 0), pipeline_mode=pl.Buffered(1)),
        ],
        out_specs=[pl.BlockSpec((1, tm, wd), lambda b, t: (b, t, 0)) for wd in widths],
        out_shape=[jax.ShapeDtypeStruct((B, T, wd), F32) for wd in widths],
        compiler_params=pltpu.CompilerParams(
            dimension_semantics=("parallel", "parallel"), vmem_limit_bytes=VMEM_LIMIT),
        name="in_projection",
    )(x, mod_l, gain, w_bf)


def _rglru_kernel(tc, pa_ref, cw_ref, vec_ref, wg_ref, g_ref, o_ref):
    T = pa_ref.shape[1]
    W = o_ref.shape[2]
    conv_b = vec_ref[0:1, :]
    rg_b = vec_ref[1:2, :]
    ig_b = vec_ref[2:3, :]
    sp_lam = _softplus(-vec_ref[3:4, :])
    beta = vec_ref[4:5, :]
    row = lax.broadcasted_iota(jnp.int32, (tc, W), 0)
    g01 = g_ref[...]
    wg = wg_ref[...]

    def body(i, hprev):
        t0 = pl.multiple_of(i * tc, tc)
        xa = pa_ref[0, pl.ds(t0, tc), 0:W]
        ya = pa_ref[0, pl.ds(t0, tc), W:2 * W]
        tp = pl.multiple_of(jnp.maximum(t0 - 8, 0), 8)
        prev8 = jnp.where(i > 0, pa_ref[0, pl.ds(tp, 8), 0:W], 0.0)
        win = jnp.concatenate([prev8, xa], axis=0)
        u = conv_b + cw_ref[CONV_W - 1:CONV_W, :] * xa
        for j in range(1, CONV_W):
            u = u + cw_ref[CONV_W - 1 - j:CONV_W - j, :] * pltpu.roll(win, j, 0)[8:]
        z = jnp.dot(u.astype(BF16), wg, preferred_element_type=F32)
        r = _sigmoid(z[:, 0:W] + rg_b)
        ig = _sigmoid(z[:, W:2 * W] + ig_b)
        log_a = (-LRU_C) * r * sp_lam
        a = jnp.exp(log_a)
        mult = jnp.sqrt(-_expm1(2.0 * log_a))
        mult = jnp.where(row + t0 == 0, 1.0, mult)
        bv = mult * (ig * u)
        s = 1
        while s < tc:
            keep = row >= s
            a_s = jnp.where(keep, pltpu.roll(a, s, 0), 1.0)
            b_s = jnp.where(keep, pltpu.roll(bv, s, 0), 0.0)
            bv = a * b_s + bv
            a = a * a_s
            s *= 2
        h = a * hprev + bv
        y = h * jax.nn.gelu(ya)
        ms = _segsum(y * y, g01) * (1.0 / HEAD)
        o_ref[0, pl.ds(t0, tc), :] = (y * lax.rsqrt(ms + NORM_EPS) * beta).astype(o_ref.dtype)
        return h[tc - 1:tc, :]

    lax.fori_loop(0, T // tc, body, jnp.zeros((1, W), F32))


def _rglru(pa, conv_w, vecs, wg_bf, g01, tc=128):
    B, T, W2 = pa.shape
    W = W2 // 2
    return pl.pallas_call(
        functools.partial(_rglru_kernel, tc),
        grid=(B,),
        in_specs=[
            pl.BlockSpec((1, T, W2), lambda b: (b, 0, 0)),
            pl.BlockSpec(conv_w.shape, lambda b: (0, 0)),
            pl.BlockSpec(vecs.shape, lambda b: (0, 0)),
            pl.BlockSpec(wg_bf.shape, lambda b: (0, 0)),
            pl.BlockSpec(g01.shape, lambda b: (0, 0)),
        ],
        out_specs=pl.BlockSpec((1, T, W), lambda b: (b, 0, 0)),
        out_shape=jax.ShapeDtypeStruct((B, T, W), BF16),
        compiler_params=pltpu.CompilerParams(
            dimension_semantics=("parallel",), vmem_limit_bytes=VMEM_LIMIT),
        name="rglru",
    )(pa, conv_w, vecs, wg_bf, g01)


def _hgrn_kernel(layer, pb_ref, lb_ref, beta_ref, wst_ref, g_ref, o_ref,
                 s_ref, qin_ref, oi_ref, kv_ref, dec_ref):
    C = CHUNK
    R = wst_ref.shape[1]
    G = R // C
    TB = pb_ref.shape[1]
    W = o_ref.shape[2]
    n_pairs = W // LANES

    @pl.when(pl.program_id(1) == 0)
    def _():
        s_ref[...] = jnp.zeros_like(s_ref)

    lbr = lb_ref[...]
    n_layers = lbr.shape[0]
    mx = lbr[0:1, :]
    for i in range(1, n_layers):
        mx = jnp.maximum(mx, lbr[i:i + 1, :])
    es = [jnp.exp(lbr[i:i + 1, :] - mx) for i in range(n_layers)]
    tot = es[0]
    for i in range(1, n_layers):
        tot = tot + es[i]
    sm = [e / tot for e in es]
    cum = sm[0]
    for i in range(1, layer + 1):
        cum = cum + sm[i]
    lb = cum - sm[0]
    log_lb = jnp.log(lb)
    log_1mlb = _log1p(-lb)

    beta = beta_ref[...]
    g01 = g_ref[...]
    wst = wst_ref[...]
    lane = lax.broadcasted_iota(jnp.int32, (C, LANES), 1)
    m0 = lane < HEAD
    tw = lax.broadcasted_iota(jnp.int32, (C, 2 * C), 0)
    sw = lax.broadcasted_iota(jnp.int32, (C, 2 * C), 1) & (C - 1)
    eye = tw == sw
    level = []
    for lev in range(N_LEVELS):
        th = tw >> (N_LEVELS - 1 - lev)
        sh = sw >> (N_LEVELS - 1 - lev)
        level.append((th == sh + 1) & ((sh & 1) == 0))
    r2 = lax.broadcasted_iota(jnp.int32, (LANES, LANES), 0)
    c2 = lax.broadcasted_iota(jnp.int32, (LANES, LANES), 1)
    same_head = (r2 >> HEAD_BITS) == (c2 >> HEAD_BITS)

    def prep(i, carry):
        t0 = pl.multiple_of(i * R, R)
        q = pb_ref[0, pl.ds(t0, R), 0:W] * (HEAD ** -0.5)
        fl = pb_ref[0, pl.ds(t0, R), W:2 * W]
        v = pb_ref[0, pl.ds(t0, R), 2 * W:3 * W]
        lc = log_1mlb + (jnp.minimum(fl, 0.0) - _log1p(jnp.exp(-jnp.abs(fl))))
        log_f = jnp.maximum(log_lb, lc) + _log1p(jnp.exp(-jnp.abs(log_lb - lc)))
        kg = -_expm1(log_f)
        ex = jnp.exp(_mm(wst, log_f, NN, na=1, nb=2))
        qin_ref[pl.ds(t0, R), :] = q * ex[0:R]
        k_dec = kg * ex[R:2 * R]
        items = [(g, p) for g in range(G) for p in range(n_pairs)]
        n = range(len(items))

        def piece(x, g, p, base=0):
            return x[base + g * C:base + (g + 1) * C, p * LANES:(p + 1) * LANES]

        q_p = [piece(q, g, p) for g, p in items]
        v_p = [piece(v, g, p) for g, p in items]
        k_s = [_stack(piece(kg, g, p), m0) for g, p in items]
        amat = [jnp.where(eye, _mm(q_p[j], k_s[j], NT, *P_HATT), 0.0) for j in n]
        for lev in range(N_LEVELS):
            es = [piece(ex, g, p, (2 + lev) * R) for g, p in items]
            amat = [jnp.where(level[lev],
                              _mm(q_p[j] * es[j],
                                  k_s[j] * jnp.concatenate([es[j], es[j]], axis=0), NT, *P_HATT),
                              amat[j]) for j in n]
        oi = [_mm(amat[j], _stack(v_p[j], m0), NN, *P_HAPP) for j in n]
        kv = [_mm(v_p[j].T, piece(k_dec, g, p), NN, *P_HAPP) for j, (g, p) in enumerate(items)]
        for g in range(G):
            oi_ref[pl.ds(t0 + g * C, C), :] = jnp.concatenate(
                [oi[g * n_pairs + p] for p in range(n_pairs)], axis=1)
            dec_ref[i * G + g] = jnp.broadcast_to(ex[(g + 1) * C - 1:(g + 1) * C, :], (8, W))
        for j, (g, p) in enumerate(items):
            kv_ref[(i * G + g) * n_pairs + p] = jnp.where(same_head, kv[j], 0.0)
        return carry

    lax.fori_loop(0, TB // R, prep, 0)

    def scan(i, carry):
        t0 = pl.multiple_of(i * C, C)
        dec = dec_ref[i]
        outs = []
        for p in range(n_pairs):
            sl = slice(p * LANES, (p + 1) * LANES)
            st = s_ref[p]
            outs.append(_mm(qin_ref[pl.ds(t0, C), sl], st, NT, *P_HAPP))
            s_ref[p] = st * dec[0:1, sl] + kv_ref[i * n_pairs + p]
        oi_ref[pl.ds(t0, C), :] = oi_ref[pl.ds(t0, C), :] + jnp.concatenate(outs, axis=1)
        return carry

    lax.fori_loop(0, TB // C, scan, 0)

    o = oi_ref[...]
    g = pb_ref[0, :, 3 * W:4 * W]
    ms = _segsum(o * o, g01) * (1.0 / HEAD)
    o = o * lax.rsqrt(ms + NORM_EPS) * (g * _sigmoid(g)) * beta
    o_ref[0] = o.astype(o_ref.dtype)


def _hgrn_weights(group):
    C = CHUNK
    R = group * C
    w = np.zeros(((2 + N_LEVELS) * R, R), np.float32)
    for g in range(group):
        o = g * C
        for r in range(C):
            w[o + r, o:o + r + 1] = 1.0
            w[R + o + r, o + r + 1:o + C] = 1.0
            for lev in range(N_LEVELS):
                n = C >> lev
                m = (r // n) * n + n // 2 - 1
                if r % n >= n // 2:
                    w[(2 + lev) * R + o + r, o + m + 1:o + r + 1] = 1.0
                else:
                    w[(2 + lev) * R + o + r, o + r + 1:o + m + 1] = 1.0
    return w


def _hgrn(pb, hgrn_lb, beta_b, wst, g01, layer, tb):
    B, T, W4 = pb.shape
    W = W4 // 4
    return pl.pallas_call(
        functools.partial(_hgrn_kernel, layer),
        grid=(B, T // tb),
        in_specs=[
            pl.BlockSpec((1, tb, W4), lambda b, t: (b, t, 0)),
            pl.BlockSpec(hgrn_lb.shape, lambda b, t: (0, 0)),
            pl.BlockSpec(beta_b.shape, lambda b, t: (0, 0)),
            pl.BlockSpec(wst.shape, lambda b, t: (0, 0)),
            pl.BlockSpec(g01.shape, lambda b, t: (0, 0)),
        ],
        out_specs=pl.BlockSpec((1, tb, W), lambda b, t: (b, t, 0)),
        out_shape=jax.ShapeDtypeStruct((B, T, W), BF16),
        scratch_shapes=[pltpu.VMEM((W // LANES, LANES, LANES), F32),
                        pltpu.VMEM((tb, W), F32), pltpu.VMEM((tb, W), F32),
                        pltpu.VMEM(((tb // CHUNK) * (W // LANES), LANES, LANES), F32),
                        pltpu.VMEM((tb // CHUNK, 8, W), F32)],
        compiler_params=pltpu.CompilerParams(
            dimension_semantics=("parallel", "arbitrary"), vmem_limit_bytes=VMEM_LIMIT),
        name="hgrn2",
    )(pb, hgrn_lb, beta_b, wst, g01)


def _rwkv_kernel(pc_ref, mu_ref, vec_ref, lora_ref, gup_ref, tri_ref, g_ref, o_ref,
                 s_ref, prev_ref, rp_ref, y0_ref, mx_ref, nx_ref, bonus_ref, gate_ref, y_ref):
    C = CHUNK
    R = tri_ref.shape[0]
    G = R // C
    TB = pc_ref.shape[1]
    W = o_ref.shape[2]
    n_pairs = W // LANES

    @pl.when(pl.program_id(1) == 0)
    def _():
        s_ref[...] = jnp.zeros_like(s_ref)
        prev_ref[...] = jnp.zeros_like(prev_ref)

    mu = mu_ref[...]
    w0 = vec_ref[0:1, :]
    a0 = vec_ref[1:2, :]
    k_k = vec_ref[2:3, :]
    k_a = vec_ref[3:4, :]
    r_k = vec_ref[4:5, :]
    lnx_w = vec_ref[5:6, :]
    lnx_b = vec_ref[6:7, :]
    beta = vec_ref[7:8, :]
    lora = _split(lora_ref[...], 1)
    gup = _split(gup_ref[...], 1)
    tri = tri_ref[...]
    g01 = g_ref[...]
    lane = lax.broadcasted_iota(jnp.int32, (C, LANES), 1)
    lane_r = lax.broadcasted_iota(jnp.int32, (R, LANES), 1)
    m0 = lane < HEAD
    tw = lax.broadcasted_iota(jnp.int32, (C, 2 * C), 0)
    sw = lax.broadcasted_iota(jnp.int32, (C, 2 * C), 1) & (C - 1)
    strict = tw > sw
    incl = tw >= sw
    eye_w = jnp.where(tw == sw, 1.0, 0.0)
    r2 = lax.broadcasted_iota(jnp.int32, (2 * C, 2 * C), 0)
    c2 = lax.broadcasted_iota(jnp.int32, (2 * C, 2 * C), 1)
    same_head = (r2 >> HEAD_BITS) == (c2 >> HEAD_BITS)
    eye_f = jnp.where(r2 == c2, 1.0, 0.0)
    o_w = 3 * W
    o_g = o_w + LORA_W + LORA_A

    def mm_parts(x, parts):
        return jnp.dot(x.astype(BF16), parts[0], preferred_element_type=F32)

    def elem_steps(gi, out):
        t0 = gi * R
        pc = pc_ref[0, t0:t0 + R, :]
        prev8 = prev_ref[...] if gi == 0 else pc_ref[0, t0 - 8:t0, :]
        prev = pltpu.roll(jnp.concatenate([prev8, pc], axis=0), 1, 0)[8:]
        ps = pc + (prev - pc) * mu
        r = ps[:, 0:W]
        k = ps[:, W:2 * W]
        v = ps[:, 2 * W:3 * W]
        z = ps[:, o_w:o_g]
        z = jnp.where(lane_r < LORA_W, jnp.tanh(z), z)
        lo = mm_parts(z, lora)
        yield
        gate = mm_parts(_sigmoid(ps[:, o_g:o_g + LORA_G]), gup)
        gate_ref[t0:t0 + R, :] = gate * beta
        yield
        w_raw = -_softplus(-(w0 + lo[:, 0:W])) - 0.5
        lw = -jnp.exp(w_raw)
        gam = _mm(tri, lw, NN, na=1, nb=2)
        yield
        a_sig = _sigmoid(a0 + lo[:, W:2 * W])
        kk = k * k_k
        kk = kk * lax.rsqrt(_segsum(kk * kk, g01) + 1e-12)
        yield
        k2 = k * (1.0 + (a_sig - 1.0) * k_a)
        bonus_ref[t0:t0 + R, :] = _segsum(r * k2 * r_k, g01) * v
        yield
        e_in = jnp.exp(gam)
        e_ex = jnp.exp(gam - lw)
        e_neg = jnp.exp(-gam)
        out.append((-kk * e_ex, r * e_in, kk * a_sig * e_neg, k2 * e_neg, v, e_in))

    def mm_stages(gi, vals):
        at, rt, bt, kt, v, e_in = vals
        items = [(g, p) for g in range(G) for p in range(n_pairs)]
        n = range(len(items))

        def pieces(x):
            return [x[g * C:(g + 1) * C, p * LANES:(p + 1) * LANES] for g, p in items]

        def st(x):
            return _stack(x, m0)

        a_p, r_p, b_p, k_p, v_p = pieces(at), pieces(rt), pieces(bt), pieces(kt), pieces(v)
        v_s = [st(v_p[j]) for j in n]
        att = [_mm(jnp.concatenate([a_p[j], r_p[j]], axis=0),
                   jnp.concatenate([st(b_p[j]), st(k_p[j])], axis=0), NT, *P_ATT)
               for j in n]
        yield
        lmat = [jnp.where(strict, att[j][0:C, 0:2 * C], 0.0) for j in n]
        ak = [jnp.where(strict, att[j][0:C, 2 * C:4 * C], 0.0) for j in n]
        rb = [jnp.where(incl, att[j][C:2 * C, 0:2 * C], 0.0) for j in n]
        rk = [jnp.where(incl, att[j][C:2 * C, 2 * C:4 * C], 0.0) for j in n]
        akv = [_mm(ak[j], v_s[j], NN, *P_APP) for j in n]
        yield
        pw = [_mm(lmat[j], st(lmat[j]), NN, *P_INV) for j in n]
        xs = [eye_w + lmat[j] for j in n]
        yield
        for _ in range(N_LEVELS - 2):
            z = [_mm(pw[j], jnp.concatenate([st(xs[j]), st(pw[j])], axis=1), NN, *P_INV)
                 for j in n]
            xs = [xs[j] + z[j][:, 0:2 * C] for j in n]
            pw = [z[j][:, 2 * C:4 * C] for j in n]
            yield
        xs = [xs[j] + _mm(pw[j], st(xs[j]), NN, *P_INV) for j in n]
        yield
        au = [_mm(xs[j], jnp.concatenate([st(a_p[j]), st(akv[j])], axis=1), NN, *P_APP)
              for j in n]
        ap = [au[j][:, 0:LANES] for j in n]
        u0 = [au[j][:, LANES:2 * LANES] for j in n]
        yield
        ry = [_mm(rb[j], jnp.concatenate([st(ap[j]), st(u0[j])], axis=1), NN, *P_APP)
              for j in n]
        rkv = [_mm(rk[j], v_s[j], NN, *P_APP) for j in n]
        yield
        mx = [_mm(ap[j].T, b_p[j], NN, *P_APP) for j in n]
        nx = [_mm(jnp.concatenate([u0[j], v_p[j]], axis=0).T,
                  jnp.concatenate([b_p[j], k_p[j]], axis=0), NN, *P_APP) for j in n]
        for j, (g, p) in enumerate(items):
            idx = (gi * G + g) * n_pairs + p
            glast = e_in[(g + 1) * C - 1:(g + 1) * C, p * LANES:(p + 1) * LANES]
            rp_ref[idx] = r_p[j] + ry[j][:, 0:LANES]
            y0_ref[idx] = ry[j][:, LANES:2 * LANES] + rkv[j]
            mx_ref[idx] = (eye_f + jnp.where(same_head, mx[j], 0.0)) * glast
            nx_ref[idx] = jnp.where(same_head, nx[j], 0.0) * glast
        yield

    def scan_chunk(ci):
        pairs = range(n_pairs)
        sts = [s_ref[p] for p in pairs]
        ys = [_mm(rp_ref[ci * n_pairs + p], sts[p], NT, *P_STATE) for p in pairs]
        sn = [_mm(sts[p], mx_ref[ci * n_pairs + p], NN, *P_STATE) for p in pairs]
        for p in pairs:
            s_ref[p] = sn[p] + nx_ref[ci * n_pairs + p]
        y_ref[ci * C:(ci + 1) * C, :] = jnp.concatenate(
            [ys[p] + y0_ref[ci * n_pairs + p] for p in pairs], axis=1)

    n_groups = TB // R
    vals = []
    for _ in elem_steps(0, vals):
        pass
    ready = []
    for gi in range(n_groups):
        nxt = elem_steps(gi + 1, vals) if gi + 1 < n_groups else iter(())
        for si, _ in enumerate(mm_stages(gi, vals[gi])):
            next(nxt, None)
            if ready and si % SCAN_EVERY == SCAN_EVERY - 1:
                scan_chunk(ready.pop(0))
        for _ in nxt:
            pass
        ready += [gi * G + g for g in range(G)]
    for ci in ready:
        scan_chunk(ci)
    prev_ref[...] = pc_ref[0, TB - 8:TB, :]

    y = y_ref[...]
    mean = _segsum(y, g01) * (1.0 / HEAD)
    d = y - mean
    var = _segsum(d * d, g01) * (1.0 / HEAD)
    yn = d * lax.rsqrt(var + GN_EPS) * lnx_w + lnx_b
    o_ref[0] = ((yn + bonus_ref[...]) * gate_ref[...]).astype(o_ref.dtype)


def _rwkv(pc, mu, vecs, lora_w, g_up, tri, g01, tb):
    B, T, PC = pc.shape
    W = vecs.shape[1]
    n_mats = (tb // CHUNK) * (W // LANES)
    mats = pltpu.VMEM((n_mats, LANES, LANES), F32)
    rows = pltpu.VMEM((n_mats, CHUNK, LANES), F32)
    return pl.pallas_call(
        _rwkv_kernel,
        grid=(B, T // tb),
        in_specs=[
            pl.BlockSpec((1, tb, PC), lambda b, t: (b, t, 0)),
            pl.BlockSpec(mu.shape, lambda b, t: (0, 0)),
            pl.BlockSpec(vecs.shape, lambda b, t: (0, 0)),
            pl.BlockSpec(lora_w.shape, lambda b, t: (0, 0)),
            pl.BlockSpec(g_up.shape, lambda b, t: (0, 0)),
            pl.BlockSpec(tri.shape, lambda b, t: (0, 0)),
            pl.BlockSpec(g01.shape, lambda b, t: (0, 0)),
        ],
        out_specs=pl.BlockSpec((1, tb, W), lambda b, t: (b, t, 0)),
        out_shape=jax.ShapeDtypeStruct((B, T, W), BF16),
        scratch_shapes=[pltpu.VMEM((W // LANES, LANES, LANES), F32),
                        pltpu.VMEM((8, PC), F32),
                        rows, rows, mats, mats,
                        pltpu.VMEM((tb, W), F32), pltpu.VMEM((tb, W), F32),
                        pltpu.VMEM((tb, W), F32)],
        compiler_params=pltpu.CompilerParams(
            dimension_semantics=("parallel", "arbitrary"), vmem_limit_bytes=VMEM_LIMIT),
        name="rwkv7",
    )(pc, mu, vecs, lora_w, g_up, tri, g01)


def _ffn_kernel(final, widths, x_ref, ya_ref, yb_ref, yc_ref, mod_ref, g_ref, wo_ref,
                wg_ref, wu_ref, wd_ref, fg_ref, o_ref):
    x = x_ref[0]
    acc = None
    off = 0
    for ref, wd in zip((ya_ref, yb_ref, yc_ref), widths):
        d = jnp.dot(ref[0], wo_ref[off:off + wd, :], preferred_element_type=F32)
        acc = d if acc is None else acc + d
        off += wd
    x1 = x + mod_ref[0, 2:3, :] * acc
    ms = jnp.mean(x1 * x1, axis=-1, keepdims=True)
    h = x1 * lax.rsqrt(ms + NORM_EPS) * g_ref[...]
    h = h * (1.0 + mod_ref[0, 4:5, :]) + mod_ref[0, 3:4, :]
    hb = h.astype(BF16)
    gt = jnp.dot(hb, wg_ref[...], preferred_element_type=F32)
    up = jnp.dot(hb, wu_ref[...], preferred_element_type=F32)
    act = (gt * _sigmoid(gt) * up).astype(BF16)
    dn = jnp.dot(act, wd_ref[...], preferred_element_type=F32)
    x2 = x1 + mod_ref[0, 5:6, :] * dn
    if final:
        ms2 = jnp.mean(x2 * x2, axis=-1, keepdims=True)
        x2 = x2 * lax.rsqrt(ms2 + NORM_EPS) * fg_ref[...]
    o_ref[0] = x2


def _out_ffn(x, ya, yb, yc, mod_l, gain, wo, wg, wu, wd, final_g, final, tm):
    B, T, D = x.shape
    widths = (ya.shape[2], yb.shape[2], yc.shape[2])
    const = lambda shape: pl.BlockSpec(shape, lambda b, t: (0, 0), pipeline_mode=pl.Buffered(1))
    return pl.pallas_call(
        functools.partial(_ffn_kernel, final, widths),
        grid=(B, T // tm),
        in_specs=[
            pl.BlockSpec((1, tm, D), lambda b, t: (b, t, 0)),
            pl.BlockSpec((1, tm, widths[0]), lambda b, t: (b, t, 0)),
            pl.BlockSpec((1, tm, widths[1]), lambda b, t: (b, t, 0)),
            pl.BlockSpec((1, tm, widths[2]), lambda b, t: (b, t, 0)),
            pl.BlockSpec((1, 6, D), lambda b, t: (b, 0, 0)),
            pl.BlockSpec((1, D), lambda b, t: (0, 0)),
            const(wo.shape), const(wg.shape), const(wu.shape), const(wd.shape),
            pl.BlockSpec((1, D), lambda b, t: (0, 0)),
        ],
        out_specs=pl.BlockSpec((1, tm, D), lambda b, t: (b, t, 0)),
        out_shape=jax.ShapeDtypeStruct((B, T, D), F32),
        compiler_params=pltpu.CompilerParams(
            dimension_semantics=("parallel", "parallel"), vmem_limit_bytes=VMEM_LIMIT),
        name="out_ffn",
    )(x, ya, yb, yc, mod_l, gain, wo, wg, wu, wd, final_g)


def _block_diag(w):
    G, n, _ = w.shape
    eye = jnp.eye(G, dtype=w.dtype)
    return (eye[:, None, :, None] * w[:, :, None, :]).reshape(G * n, G * n)


def _seg_ones():
    idx = np.arange(LANES) // HEAD
    return jnp.asarray((idx[:, None] == idx[None, :]).astype(np.float32), dtype=BF16)


def kernel(x, c, norm1_g, norm2_g, ada_w, ada_b, w_in, conv_w, conv_b, rg_w, rg_b, ig_w, ig_b, lru_lam, hgrn_lb, rwkv_mu, rwkv_w0, rwkv_w_up, rwkv_a0, rwkv_a_up, rwkv_g_up, rwkv_k_k, rwkv_k_a, rwkv_r_k, rwkv_lnx_w, rwkv_lnx_b, mix_beta, w_out, ffn_w_gate, ffn_w_up, ffn_w_down, final_g):
    B, T, D = x.shape
    L = w_in.shape[0]
    WA = conv_w.shape[2]
    WB = hgrn_lb.shape[1]
    WC = rwkv_w0.shape[1]
    widths_p = (2 * WA, 4 * WB, rwkv_mu.shape[1])
    tm = 256

    mod = _modulation(c, ada_w, ada_b).reshape(L, B, 6, D)
    wst = jnp.asarray(_hgrn_weights(HGRN_GROUP), dtype=BF16)
    tri = jnp.asarray(np.kron(np.eye(RWKV_GROUP, dtype=np.float32),
                              np.tril(np.ones((CHUNK, CHUNK), np.float32))), dtype=BF16)
    g_a = g_b = g_c = _seg_ones()

    for l in range(L):
        pa, pb, pc = _in_projection(x, mod[l], norm1_g[l][None], w_in[l].astype(BF16),
                                    widths_p, tm)
        beta = mix_beta[l]
        vec_a = jnp.stack([conv_b[l], rg_b[l], ig_b[l], lru_lam[l], beta[:WA],
                           jnp.zeros_like(beta[:WA]), jnp.zeros_like(beta[:WA]),
                           jnp.zeros_like(beta[:WA])])
        wg_a = jnp.concatenate([_block_diag(rg_w[l]), _block_diag(ig_w[l])], axis=1).astype(BF16)
        ya = _rglru(pa, conv_w[l], vec_a, wg_a, g_a)
        yb = _hgrn(pb, hgrn_lb, beta[None, WA:WA + WB], wst, g_b, l, 256)
        vec_c = jnp.stack([rwkv_w0[l], rwkv_a0[l], rwkv_k_k[l], rwkv_k_a[l],
                           rwkv_r_k[l].reshape(-1), rwkv_lnx_w[l], rwkv_lnx_b[l],
                           beta[WA + WB:]])
        zw = jnp.zeros_like(rwkv_w_up[l])
        lora_w = jnp.concatenate(
            [jnp.concatenate([rwkv_w_up[l], zw], axis=1),
             jnp.concatenate([zw, rwkv_a_up[l]], axis=1)], axis=0)
        yc = _rwkv(pc, rwkv_mu[l][None], vec_c, lora_w, rwkv_g_up[l], tri, g_c, MIXER_ROWS)
        x = _out_ffn(x, ya, yb, yc, mod[l], norm2_g[l][None], w_out[l].astype(BF16),
                     ffn_w_gate[l].astype(BF16), ffn_w_up[l].astype(BF16),
                     ffn_w_down[l].astype(BF16), final_g[None], l == L - 1, tm)
    return x
```

```python
import functools

import numpy as np
import jax
import jax.numpy as jnp
from jax import lax
from jax.experimental import pallas as pl
from jax.experimental.pallas import tpu as pltpu

F32 = jnp.float32
BF16 = jnp.bfloat16

HEAD = 64
HEAD_BITS = 6
LANES = 128
CONV_W = 4
LRU_C = 8.0
NORM_EPS = 1e-6
GN_EPS = 64e-5
CHUNK = 64
N_LEVELS = 6
RWKV_GROUP = 2
SCAN_EVERY = 4
MIXER_ROWS = 512
HGRN_GROUP = 2
LORA_W = 64
LORA_A = 64
LORA_G = 128
VMEM_LIMIT = 56 * 1024 * 1024

NN = (((1,), (0,)), ((), ()))
NT = (((1,), (1,)), ((), ()))

P_ATT = (1, 1)
P_INV = (1, 1)
P_APP = (1, 1)
P_STATE = (1, 1)
P_HATT = (1, 1)
P_HAPP = (1, 1)


def _split(x, n):
    if x.dtype == BF16:
        return [x]
    parts = []
    r = x
    for i in range(n):
        p = r.astype(BF16)
        parts.append(p)
        if i + 1 < n:
            r = r - p.astype(F32)
    return parts


def _mm(a, b, dims=NN, na=2, nb=2):
    pa = _split(a, na)
    pb = _split(b, nb)
    order = max(len(pa), len(pb))
    acc = None
    for i, x in enumerate(pa):
        for j, y in enumerate(pb):
            if i + j < order:
                d = lax.dot_general(x, y, dims, preferred_element_type=F32)
                acc = d if acc is None else acc + d
    return acc


def _segsum(x, g01):
    xb = x.astype(BF16)
    return jnp.concatenate(
        [jnp.dot(xb[:, o:o + LANES], g01, preferred_element_type=F32)
         for o in range(0, x.shape[1], LANES)], axis=1)


def _stack(x, m0):
    return jnp.concatenate([jnp.where(m0, x, 0.0), jnp.where(m0, 0.0, x)], axis=0)


def _sigmoid(x):
    return 0.5 * jnp.tanh(0.5 * x) + 0.5


def _log1p_exp_neg_abs(x):
    return jnp.log(1.0 + jnp.exp(-jnp.abs(x)))


def _neg_expm1(x, ex):
    return -jnp.tanh(0.5 * x) * (ex + 1.0)


def _softplus(x):
    return jnp.maximum(x, 0.0) + _log1p_exp_neg_abs(x)


def _mod_kernel(c_ref, w_ref, b_ref, o_ref):
    c = c_ref[...]
    cs = c * _sigmoid(c)
    o_ref[0] = _mm(cs, w_ref[0]) + b_ref[0]


def _modulation(c, ada_w, ada_b):
    L, D, N = ada_w.shape
    B = c.shape[0]
    tn = N // 4
    return pl.pallas_call(
        _mod_kernel,
        grid=(L, N // tn),
        in_specs=[
            pl.BlockSpec((B, D), lambda l, j: (0, 0)),
            pl.BlockSpec((1, D, tn), lambda l, j: (l, 0, j)),
            pl.BlockSpec((1, 1, tn), lambda l, j: (l, 0, j)),
        ],
        out_specs=pl.BlockSpec((1, B, tn), lambda l, j: (l, 0, j)),
        out_shape=jax.ShapeDtypeStruct((L, B, N), F32),
        compiler_params=pltpu.CompilerParams(
            dimension_semantics=("parallel", "parallel"), vmem_limit_bytes=VMEM_LIMIT),
        name="modulation",
    )(c, ada_w, ada_b.reshape(L, 1, N))


def _inproj_kernel(widths, x_ref, mod_ref, g_ref, w_ref, pa_ref, pb_ref, pc_ref):
    x = x_ref[0]
    ms = jnp.mean(x * x, axis=-1, keepdims=True)
    h = x * lax.rsqrt(ms + NORM_EPS) * g_ref[...]
    h = h * (1.0 + mod_ref[0, 1:2, :]) + mod_ref[0, 0:1, :]
    hb = h.astype(BF16)
    off = 0
    for ref, wd in zip((pa_ref, pb_ref, pc_ref), widths):
        ref[0] = jnp.dot(hb, w_ref[:, off:off + wd], preferred_element_type=F32)
        off += wd


def _in_projection(x, mod_l, gain, w_bf, widths, tm):
    B, T, D = x.shape
    P = w_bf.shape[1]
    return pl.pallas_call(
        functools.partial(_inproj_kernel, widths),
        grid=(B, T // tm),
        in_specs=[
            pl.BlockSpec((1, tm, D), lambda b, t: (b, t, 0)),
            pl.BlockSpec((1, 6, D), lambda b, t: (b, 0, 0)),
            pl.BlockSpec((1, D), lambda b, t: (0, 0)),
            pl.BlockSpec((D, P), lambda b, t: (0, 0), pipeline_mode=pl.Buffered(1)),
        ],
        out_specs=[pl.BlockSpec((1, tm, wd), lambda b, t: (b, t, 0)) for wd in widths],
        out_shape=[jax.ShapeDtypeStruct((B, T, wd), F32) for wd in widths],
        compiler_params=pltpu.CompilerParams(
            dimension_semantics=("parallel", "parallel"), vmem_limit_bytes=VMEM_LIMIT),
        name="in_projection",
    )(x, mod_l, gain, w_bf)


def _rglru_kernel(tc, pa_ref, cw_ref, vec_ref, wg_ref, g_ref, o_ref):
    T = pa_ref.shape[1]
    W = o_ref.shape[2]
    conv_b = vec_ref[0:1, :]
    rg_b = vec_ref[1:2, :]
    ig_b = vec_ref[2:3, :]
    sp_lam = _softplus(-vec_ref[3:4, :])
    beta = vec_ref[4:5, :]
    row = lax.broadcasted_iota(jnp.int32, (tc, W), 0)
    g01 = g_ref[...]
    wg = wg_ref[...]

    def body(i, hprev):
        t0 = pl.multiple_of(i * tc, tc)
        xa = pa_ref[0, pl.ds(t0, tc), 0:W]
        ya = pa_ref[0, pl.ds(t0, tc), W:2 * W]
        tp = pl.multiple_of(jnp.maximum(t0 - 8, 0), 8)
        prev8 = jnp.where(i > 0, pa_ref[0, pl.ds(tp, 8), 0:W], 0.0)
        win = jnp.concatenate([prev8, xa], axis=0)
        u = conv_b + cw_ref[CONV_W - 1:CONV_W, :] * xa
        for j in range(1, CONV_W):
            u = u + cw_ref[CONV_W - 1 - j:CONV_W - j, :] * pltpu.roll(win, j, 0)[8:]
        z = jnp.dot(u.astype(BF16), wg, preferred_element_type=F32)
        r = _sigmoid(z[:, 0:W] + rg_b)
        ig = _sigmoid(z[:, W:2 * W] + ig_b)
        log_a = (-LRU_C) * r * sp_lam
        a = jnp.exp(log_a)
        mult = jnp.sqrt(_neg_expm1(2.0 * log_a, a * a))
        mult = jnp.where(row + t0 == 0, 1.0, mult)
        bv = mult * (ig * u)
        s = 1
        while s < tc:
            keep = row >= s
            a_s = jnp.where(keep, pltpu.roll(a, s, 0), 1.0)
            b_s = jnp.where(keep, pltpu.roll(bv, s, 0), 0.0)
            bv = a * b_s + bv
            a = a * a_s
            s *= 2
        h = a * hprev + bv
        y = h * jax.nn.gelu(ya)
        ms = _segsum(y * y, g01) * (1.0 / HEAD)
        o_ref[0, pl.ds(t0, tc), :] = (y * lax.rsqrt(ms + NORM_EPS) * beta).astype(o_ref.dtype)
        return h[tc - 1:tc, :]

    lax.fori_loop(0, T // tc, body, jnp.zeros((1, W), F32))


def _rglru(pa, conv_w, vecs, wg_bf, g01, tc=128):
    B, T, W2 = pa.shape
    W = W2 // 2
    return pl.pallas_call(
        functools.partial(_rglru_kernel, tc),
        grid=(B,),
        in_specs=[
            pl.BlockSpec((1, T, W2), lambda b: (b, 0, 0)),
            pl.BlockSpec(conv_w.shape, lambda b: (0, 0)),
            pl.BlockSpec(vecs.shape, lambda b: (0, 0)),
            pl.BlockSpec(wg_bf.shape, lambda b: (0, 0)),
            pl.BlockSpec(g01.shape, lambda b: (0, 0)),
        ],
        out_specs=pl.BlockSpec((1, T, W), lambda b: (b, 0, 0)),
        out_shape=jax.ShapeDtypeStruct((B, T, W), BF16),
        compiler_params=pltpu.CompilerParams(
            dimension_semantics=("parallel",), vmem_limit_bytes=VMEM_LIMIT),
        name="rglru",
    )(pa, conv_w, vecs, wg_bf, g01)


def _hgrn_kernel(layer, pb_ref, lb_ref, beta_ref, wst_ref, g_ref, o_ref,
                 s_ref, qin_ref, oi_ref, kv_ref, dec_ref):
    C = CHUNK
    R = wst_ref.shape[1]
    G = R // C
    TB = pb_ref.shape[1]
    W = o_ref.shape[2]
    n_pairs = W // LANES

    @pl.when(pl.program_id(1) == 0)
    def _():
        s_ref[...] = jnp.zeros_like(s_ref)

    lbr = lb_ref[...]
    n_layers = lbr.shape[0]
    mx = lbr[0:1, :]
    for i in range(1, n_layers):
        mx = jnp.maximum(mx, lbr[i:i + 1, :])
    es = [jnp.exp(lbr[i:i + 1, :] - mx) for i in range(n_layers)]
    tot = es[0]
    for i in range(1, n_layers):
        tot = tot + es[i]
    sm = [e / tot for e in es]
    cum = sm[0]
    for i in range(1, layer + 1):
        cum = cum + sm[i]
    lb = cum - sm[0]
    log_lb = jnp.log(lb)
    log_1mlb = jnp.log(1.0 - lb)

    beta = beta_ref[...]
    g01 = g_ref[...]
    wst = wst_ref[...]
    lane = lax.broadcasted_iota(jnp.int32, (C, LANES), 1)
    m0 = lane < HEAD
    tw = lax.broadcasted_iota(jnp.int32, (C, 2 * C), 0)
    sw = lax.broadcasted_iota(jnp.int32, (C, 2 * C), 1) & (C - 1)
    eye = tw == sw
    level = []
    for lev in range(N_LEVELS):
        th = tw >> (N_LEVELS - 1 - lev)
        sh = sw >> (N_LEVELS - 1 - lev)
        level.append((th == sh + 1) & ((sh & 1) == 0))
    r2 = lax.broadcasted_iota(jnp.int32, (LANES, LANES), 0)
    c2 = lax.broadcasted_iota(jnp.int32, (LANES, LANES), 1)
    same_head = (r2 >> HEAD_BITS) == (c2 >> HEAD_BITS)

    odd_row = (lax.broadcasted_iota(jnp.int32, (C, LANES), 0) & 1) == 1
    n_kinds = wst_ref.shape[0] // R

    def elem_steps(gi, out):
        t0 = gi * R
        q = pb_ref[0, t0:t0 + R, 0:W] * (HEAD ** -0.5)
        fl = pb_ref[0, t0:t0 + R, W:2 * W]
        v = pb_ref[0, t0:t0 + R, 2 * W:3 * W]
        lc = log_1mlb + (jnp.minimum(fl, 0.0) - _log1p_exp_neg_abs(fl))
        log_f = jnp.maximum(log_lb, lc) + _log1p_exp_neg_abs(log_lb - lc)
        expo = _mm(wst, log_f, NN, na=1, nb=2)
        yield
        f = jnp.exp(log_f)
        kg = _neg_expm1(log_f, f)
        ex = jnp.exp(expo)
        qin_ref[t0:t0 + R, :] = q * ex[0:R]
        for g in range(G):
            dec_ref[gi * G + g] = jnp.broadcast_to(ex[(g + 1) * C - 1:(g + 1) * C, :], (8, W))
        out.append((q, kg, v, f, ex))

    def mm_stages(gi, vals):
        q, kg, v, f, ex = vals
        items = [(g, p) for g in range(G) for p in range(n_pairs)]
        n = range(len(items))

        def piece(x, g, p, base=0):
            return x[base + g * C:base + (g + 1) * C, p * LANES:(p + 1) * LANES]

        q_p = [piece(q, g, p) for g, p in items]
        v_p = [piece(v, g, p) for g, p in items]
        k_s = [_stack(piece(kg, g, p), m0) for g, p in items]
        amat = [jnp.where(eye, _mm(q_p[j], k_s[j], NT, *P_HATT), 0.0) for j in n]
        yield
        for lev in range(N_LEVELS):
            if 2 + lev < n_kinds:
                es = [piece(ex, g, p, (2 + lev) * R) for g, p in items]
            else:
                es = [jnp.where(odd_row, piece(f, g, p), 1.0) for g, p in items]
            amat = [jnp.where(level[lev],
                              _mm(q_p[j] * es[j],
                                  k_s[j] * jnp.concatenate([es[j], es[j]], axis=0), NT, *P_HATT),
                              amat[j]) for j in n]
            yield
        oi = [_mm(amat[j], _stack(v_p[j], m0), NN, *P_HAPP) for j in n]
        yield
        k_dec = kg * ex[R:2 * R]
        kv = [_mm(v_p[j].T, piece(k_dec, g, p), NN, *P_HAPP) for j, (g, p) in enumerate(items)]
        for g in range(G):
            t0 = (gi * G + g) * C
            oi_ref[t0:t0 + C, :] = jnp.concatenate(
                [oi[g * n_pairs + p] for p in range(n_pairs)], axis=1)
        for j, (g, p) in enumerate(items):
            kv_ref[(gi * G + g) * n_pairs + p] = jnp.where(same_head, kv[j], 0.0)
        yield

    def scan_chunk(ci):
        t0 = ci * C
        dec = dec_ref[ci]
        outs = []
        for p in range(n_pairs):
            sl = slice(p * LANES, (p + 1) * LANES)
            st = s_ref[p]
            outs.append(_mm(qin_ref[t0:t0 + C, sl], st, NT, *P_HAPP))
            s_ref[p] = st * dec[0:1, sl] + kv_ref[ci * n_pairs + p]
        oi_ref[t0:t0 + C, :] = oi_ref[t0:t0 + C, :] + jnp.concatenate(outs, axis=1)

    n_groups = TB // R
    vals = []
    for _ in elem_steps(0, vals):
        pass
    ready = []
    for gi in range(n_groups):
        nxt = elem_steps(gi + 1, vals) if gi + 1 < n_groups else iter(())
        for si, _ in enumerate(mm_stages(gi, vals[gi])):
            next(nxt, None)
            if ready and si % SCAN_EVERY == SCAN_EVERY - 1:
                scan_chunk(ready.pop(0))
        for _ in nxt:
            pass
        ready += [gi * G + g for g in range(G)]
    for ci in ready:
        scan_chunk(ci)

    o = oi_ref[...]
    g = pb_ref[0, :, 3 * W:4 * W]
    ms = _segsum(o * o, g01) * (1.0 / HEAD)
    o = o * lax.rsqrt(ms + NORM_EPS) * (g * _sigmoid(g)) * beta
    o_ref[0] = o.astype(o_ref.dtype)


def _hgrn_weights(group):
    C = CHUNK
    R = group * C
    w = np.zeros(((2 + N_LEVELS - 1) * R, R), np.float32)
    for g in range(group):
        o = g * C
        for r in range(C):
            w[o + r, o:o + r + 1] = 1.0
            w[R + o + r, o + r + 1:o + C] = 1.0
            for lev in range(N_LEVELS - 1):
                n = C >> lev
                m = (r // n) * n + n // 2 - 1
                if r % n >= n // 2:
                    w[(2 + lev) * R + o + r, o + m + 1:o + r + 1] = 1.0
                else:
                    w[(2 + lev) * R + o + r, o + r + 1:o + m + 1] = 1.0
    return w


def _hgrn(pb, hgrn_lb, beta_b, wst, g01, layer, tb):
    B, T, W4 = pb.shape
    W = W4 // 4
    return pl.pallas_call(
        functools.partial(_hgrn_kernel, layer),
        grid=(B, T // tb),
        in_specs=[
            pl.BlockSpec((1, tb, W4), lambda b, t: (b, t, 0)),
            pl.BlockSpec(hgrn_lb.shape, lambda b, t: (0, 0)),
            pl.BlockSpec(beta_b.shape, lambda b, t: (0, 0)),
            pl.BlockSpec(wst.shape, lambda b, t: (0, 0)),
            pl.BlockSpec(g01.shape, lambda b, t: (0, 0)),
        ],
        out_specs=pl.BlockSpec((1, tb, W), lambda b, t: (b, t, 0)),
        out_shape=jax.ShapeDtypeStruct((B, T, W), BF16),
        scratch_shapes=[pltpu.VMEM((W // LANES, LANES, LANES), F32),
                        pltpu.VMEM((tb, W), F32), pltpu.VMEM((tb, W), F32),
                        pltpu.VMEM(((tb // CHUNK) * (W // LANES), LANES, LANES), F32),
                        pltpu.VMEM((tb // CHUNK, 8, W), F32)],
        compiler_params=pltpu.CompilerParams(
            dimension_semantics=("parallel", "arbitrary"), vmem_limit_bytes=VMEM_LIMIT),
        name="hgrn2",
    )(pb, hgrn_lb, beta_b, wst, g01)


def _rwkv_kernel(pc_ref, mu_ref, vec_ref, lora_ref, gup_ref, tri_ref, g_ref, o_ref,
                 s_ref, prev_ref, rp_ref, y0_ref, mx_ref, nx_ref, bonus_ref, gate_ref, y_ref):
    C = CHUNK
    R = tri_ref.shape[0]
    G = R // C
    TB = pc_ref.shape[1]
    W = o_ref.shape[2]
    n_pairs = W // LANES

    @pl.when(pl.program_id(1) == 0)
    def _():
        s_ref[...] = jnp.zeros_like(s_ref)
        prev_ref[...] = jnp.zeros_like(prev_ref)

    mu = mu_ref[...]
    w0 = vec_ref[0:1, :]
    a0 = vec_ref[1:2, :]
    k_k = vec_ref[2:3, :]
    k_a = vec_ref[3:4, :]
    r_k = vec_ref[4:5, :]
    lnx_w = vec_ref[5:6, :]
    lnx_b = vec_ref[6:7, :]
    beta = vec_ref[7:8, :]
    lora = _split(lora_ref[...], 1)
    gup = _split(gup_ref[...], 1)
    tri = tri_ref[...]
    g01 = g_ref[...]
    lane = lax.broadcasted_iota(jnp.int32, (C, LANES), 1)
    lane_r = lax.broadcasted_iota(jnp.int32, (R, LANES), 1)
    m0 = lane < HEAD
    tw = lax.broadcasted_iota(jnp.int32, (C, 2 * C), 0)
    sw = lax.broadcasted_iota(jnp.int32, (C, 2 * C), 1) & (C - 1)
    strict = tw > sw
    incl = tw >= sw
    eye_w = jnp.where(tw == sw, 1.0, 0.0)
    r2 = lax.broadcasted_iota(jnp.int32, (2 * C, 2 * C), 0)
    c2 = lax.broadcasted_iota(jnp.int32, (2 * C, 2 * C), 1)
    same_head = (r2 >> HEAD_BITS) == (c2 >> HEAD_BITS)
    eye_f = jnp.where(r2 == c2, 1.0, 0.0)
    o_w = 3 * W
    o_g = o_w + LORA_W + LORA_A

    def mm_parts(x, parts):
        return jnp.dot(x.astype(BF16), parts[0], preferred_element_type=F32)

    def elem_steps(gi, out):
        t0 = gi * R
        pc = pc_ref[0, t0:t0 + R, :]
        prev8 = prev_ref[...] if gi == 0 else pc_ref[0, t0 - 8:t0, :]
        prev = pltpu.roll(jnp.concatenate([prev8, pc], axis=0), 1, 0)[8:]
        ps = pc + (prev - pc) * mu
        r = ps[:, 0:W]
        k = ps[:, W:2 * W]
        v = ps[:, 2 * W:3 * W]
        z = ps[:, o_w:o_g]
        z = jnp.where(lane_r < LORA_W, jnp.tanh(z), z)
        lo = mm_parts(z, lora)
        yield
        gate = mm_parts(_sigmoid(ps[:, o_g:o_g + LORA_G]), gup)
        gate_ref[t0:t0 + R, :] = gate * beta
        yield
        w_raw = -_softplus(-(w0 + lo[:, 0:W])) - 0.5
        lw = -jnp.exp(w_raw)
        gam = _mm(tri, lw, NN, na=1, nb=2)
        yield
        a_sig = _sigmoid(a0 + lo[:, W:2 * W])
        kk = k * k_k
        kk = kk * lax.rsqrt(_segsum(kk * kk, g01) + 1e-12)
        yield
        k2 = k * (1.0 + (a_sig - 1.0) * k_a)
        bonus_ref[t0:t0 + R, :] = _segsum(r * k2 * r_k, g01) * v
        yield
        e_in = jnp.exp(gam)
        e_ex = jnp.exp(gam - lw)
        e_neg = jnp.exp(-gam)
        out.append((-kk * e_ex, r * e_in, kk * a_sig * e_neg, k2 * e_neg, v, e_in))

    def mm_stages(gi, vals):
        at, rt, bt, kt, v, e_in = vals
        items = [(g, p) for g in range(G) for p in range(n_pairs)]
        n = range(len(items))

        def pieces(x):
            return [x[g * C:(g + 1) * C, p * LANES:(p + 1) * LANES] for g, p in items]

        def st(x):
            return _stack(x, m0)

        a_p, r_p, b_p, k_p, v_p = pieces(at), pieces(rt), pieces(bt), pieces(kt), pieces(v)
        v_s = [st(v_p[j]) for j in n]
        att = [_mm(jnp.concatenate([a_p[j], r_p[j]], axis=0),
                   jnp.concatenate([st(b_p[j]), st(k_p[j])], axis=0), NT, *P_ATT)
               for j in n]
        yield
        lmat = [jnp.where(strict, att[j][0:C, 0:2 * C], 0.0) for j in n]
        ak = [jnp.where(strict, att[j][0:C, 2 * C:4 * C], 0.0) for j in n]
        rb = [jnp.where(incl, att[j][C:2 * C, 0:2 * C], 0.0) for j in n]
        rk = [jnp.where(incl, att[j][C:2 * C, 2 * C:4 * C], 0.0) for j in n]
        akv = [_mm(ak[j], v_s[j], NN, *P_APP) for j in n]
        yield
        pw = [_mm(lmat[j], st(lmat[j]), NN, *P_INV) for j in n]
        xs = [eye_w + lmat[j] for j in n]
        yield
        for _ in range(N_LEVELS - 2):
            z = [_mm(pw[j], jnp.concatenate([st(xs[j]), st(pw[j])], axis=1), NN, *P_INV)
                 for j in n]
            xs = [xs[j] + z[j][:, 0:2 * C] for j in n]
            pw = [z[j][:, 2 * C:4 * C] for j in n]
            yield
        xs = [xs[j] + _mm(pw[j], st(xs[j]), NN, *P_INV) for j in n]
        yield
        au = [_mm(xs[j], jnp.concatenate([st(a_p[j]), st(akv[j])], axis=1), NN, *P_APP)
              for j in n]
        ap = [au[j][:, 0:LANES] for j in n]
        u0 = [au[j][:, LANES:2 * LANES] for j in n]
        yield
        ry = [_mm(rb[j], jnp.concatenate([st(ap[j]), st(u0[j])], axis=1), NN, *P_APP)
              for j in n]
        rkv = [_mm(rk[j], v_s[j], NN, *P_APP) for j in n]
        yield
        mx = [_mm(ap[j].T, b_p[j], NN, *P_APP) for j in n]
        nx = [_mm(jnp.concatenate([u0[j], v_p[j]], axis=0).T,
                  jnp.concatenate([b_p[j], k_p[j]], axis=0), NN, *P_APP) for j in n]
        for j, (g, p) in enumerate(items):
            idx = (gi * G + g) * n_pairs + p
            glast = e_in[(g + 1) * C - 1:(g + 1) * C, p * LANES:(p + 1) * LANES]
            rp_ref[idx] = r_p[j] + ry[j][:, 0:LANES]
            y0_ref[idx] = ry[j][:, LANES:2 * LANES] + rkv[j]
            mx_ref[idx] = (eye_f + jnp.where(same_head, mx[j], 0.0)) * glast
            nx_ref[idx] = jnp.where(same_head, nx[j], 0.0) * glast
        yield

    def scan_chunk(ci):
        pairs = range(n_pairs)
        sts = [s_ref[p] for p in pairs]
        ys = [_mm(rp_ref[ci * n_pairs + p], sts[p], NT, *P_STATE) for p in pairs]
        sn = [_mm(sts[p], mx_ref[ci * n_pairs + p], NN, *P_STATE) for p in pairs]
        for p in pairs:
            s_ref[p] = sn[p] + nx_ref[ci * n_pairs + p]
        y_ref[ci * C:(ci + 1) * C, :] = jnp.concatenate(
            [ys[p] + y0_ref[ci * n_pairs + p] for p in pairs], axis=1)

    n_groups = TB // R
    vals = []
    for _ in elem_steps(0, vals):
        pass
    ready = []
    for gi in range(n_groups):
        nxt = elem_steps(gi + 1, vals) if gi + 1 < n_groups else iter(())
        for si, _ in enumerate(mm_stages(gi, vals[gi])):
            next(nxt, None)
            if ready and si % SCAN_EVERY == SCAN_EVERY - 1:
                scan_chunk(ready.pop(0))
        for _ in nxt:
            pass
        ready += [gi * G + g for g in range(G)]
    for ci in ready:
        scan_chunk(ci)
    prev_ref[...] = pc_ref[0, TB - 8:TB, :]

    y = y_ref[...]
    mean = _segsum(y, g01) * (1.0 / HEAD)
    d = y - mean
    var = _segsum(d * d, g01) * (1.0 / HEAD)
    yn = d * lax.rsqrt(var + GN_EPS) * lnx_w + lnx_b
    o_ref[0] = ((yn + bonus_ref[...]) * gate_ref[...]).astype(o_ref.dtype)


def _rwkv(pc, mu, vecs, lora_w, g_up, tri, g01, tb):
    B, T, PC = pc.shape
    W = vecs.shape[1]
    n_mats = (tb // CHUNK) * (W // LANES)
    mats = pltpu.VMEM((n_mats, LANES, LANES), F32)
    rows = pltpu.VMEM((n_mats, CHUNK, LANES), F32)
    return pl.pallas_call(
        _rwkv_kernel,
        grid=(B, T // tb),
        in_specs=[
            pl.BlockSpec((1, tb, PC), lambda b, t: (b, t, 0)),
            pl.BlockSpec(mu.shape, lambda b, t: (0, 0)),
            pl.BlockSpec(vecs.shape, lambda b, t: (0, 0)),
            pl.BlockSpec(lora_w.shape, lambda b, t: (0, 0)),
            pl.BlockSpec(g_up.shape, lambda b, t: (0, 0)),
            pl.BlockSpec(tri.shape, lambda b, t: (0, 0)),
            pl.BlockSpec(g01.shape, lambda b, t: (0, 0)),
        ],
        out_specs=pl.BlockSpec((1, tb, W), lambda b, t: (b, t, 0)),
        out_shape=jax.ShapeDtypeStruct((B, T, W), BF16),
        scratch_shapes=[pltpu.VMEM((W // LANES, LANES, LANES), F32),
                        pltpu.VMEM((8, PC), F32),
                        rows, rows, mats, mats,
                        pltpu.VMEM((tb, W), F32), pltpu.VMEM((tb, W), F32),
                        pltpu.VMEM((tb, W), F32)],
        compiler_params=pltpu.CompilerParams(
            dimension_semantics=("parallel", "arbitrary"), vmem_limit_bytes=VMEM_LIMIT),
        name="rwkv7",
    )(pc, mu, vecs, lora_w, g_up, tri, g01)


def _ffn_kernel(final, widths, x_ref, ya_ref, yb_ref, yc_ref, mod_ref, g_ref, wo_ref,
                wg_ref, wu_ref, wd_ref, fg_ref, o_ref):
    x = x_ref[0]
    acc = None
    off = 0
    for ref, wd in zip((ya_ref, yb_ref, yc_ref), widths):
        d = jnp.dot(ref[0], wo_ref[off:off + wd, :], preferred_element_type=F32)
        acc = d if acc is None else acc + d
        off += wd
    x1 = x + mod_ref[0, 2:3, :] * acc
    ms = jnp.mean(x1 * x1, axis=-1, keepdims=True)
    h = x1 * lax.rsqrt(ms + NORM_EPS) * g_ref[...]
    h = h * (1.0 + mod_ref[0, 4:5, :]) + mod_ref[0, 3:4, :]
    hb = h.astype(BF16)
    gt = jnp.dot(hb, wg_ref[...], preferred_element_type=F32)
    up = jnp.dot(hb, wu_ref[...], preferred_element_type=F32)
    act = (gt * _sigmoid(gt) * up).astype(BF16)
    dn = jnp.dot(act, wd_ref[...], preferred_element_type=F32)
    x2 = x1 + mod_ref[0, 5:6, :] * dn
    if final:
        ms2 = jnp.mean(x2 * x2, axis=-1, keepdims=True)
        x2 = x2 * lax.rsqrt(ms2 + NORM_EPS) * fg_ref[...]
    o_ref[0] = x2


def _out_ffn(x, ya, yb, yc, mod_l, gain, wo, wg, wu, wd, final_g, final, tm):
    B, T, D = x.shape
    widths = (ya.shape[2], yb.shape[2], yc.shape[2])
    const = lambda shape: pl.BlockSpec(shape, lambda b, t: (0, 0), pipeline_mode=pl.Buffered(1))
    return pl.pallas_call(
        functools.partial(_ffn_kernel, final, widths),
        grid=(B, T // tm),
        in_specs=[
            pl.BlockSpec((1, tm, D), lambda b, t: (b, t, 0)),
            pl.BlockSpec((1, tm, widths[0]), lambda b, t: (b, t, 0)),
            pl.BlockSpec((1, tm, widths[1]), lambda b, t: (b, t, 0)),
            pl.BlockSpec((1, tm, widths[2]), lambda b, t: (b, t, 0)),
            pl.BlockSpec((1, 6, D), lambda b, t: (b, 0, 0)),
            pl.BlockSpec((1, D), lambda b, t: (0, 0)),
            const(wo.shape), const(wg.shape), const(wu.shape), const(wd.shape),
            pl.BlockSpec((1, D), lambda b, t: (0, 0)),
        ],
        out_specs=pl.BlockSpec((1, tm, D), lambda b, t: (b, t, 0)),
        out_shape=jax.ShapeDtypeStruct((B, T, D), F32),
        compiler_params=pltpu.CompilerParams(
            dimension_semantics=("parallel", "parallel"), vmem_limit_bytes=VMEM_LIMIT),
        name="out_ffn",
    )(x, ya, yb, yc, mod_l, gain, wo, wg, wu, wd, final_g)


def _block_diag(w):
    G, n, _ = w.shape
    eye = jnp.eye(G, dtype=w.dtype)
    return (eye[:, None, :, None] * w[:, :, None, :]).reshape(G * n, G * n)


def _seg_ones():
    idx = np.arange(LANES) // HEAD
    return jnp.asarray((idx[:, None] == idx[None, :]).astype(np.float32), dtype=BF16)


def kernel(x, c, norm1_g, norm2_g, ada_w, ada_b, w_in, conv_w, conv_b, rg_w, rg_b, ig_w, ig_b, lru_lam, hgrn_lb, rwkv_mu, rwkv_w0, rwkv_w_up, rwkv_a0, rwkv_a_up, rwkv_g_up, rwkv_k_k, rwkv_k_a, rwkv_r_k, rwkv_lnx_w, rwkv_lnx_b, mix_beta, w_out, ffn_w_gate, ffn_w_up, ffn_w_down, final_g):
    B, T, D = x.shape
    L = w_in.shape[0]
    WA = conv_w.shape[2]
    WB = hgrn_lb.shape[1]
    WC = rwkv_w0.shape[1]
    widths_p = (2 * WA, 4 * WB, rwkv_mu.shape[1])
    tm = 256

    mod = _modulation(c, ada_w, ada_b).reshape(L, B, 6, D)
    wst = jnp.asarray(_hgrn_weights(HGRN_GROUP), dtype=BF16)
    tri = jnp.asarray(np.kron(np.eye(RWKV_GROUP, dtype=np.float32),
                              np.tril(np.ones((CHUNK, CHUNK), np.float32))), dtype=BF16)
    g_a = g_b = g_c = _seg_ones()

    for l in range(L):
        pa, pb, pc = _in_projection(x, mod[l], norm1_g[l][None], w_in[l].astype(BF16),
                                    widths_p, tm)
        beta = mix_beta[l]
        vec_a = jnp.stack([conv_b[l], rg_b[l], ig_b[l], lru_lam[l], beta[:WA],
                           jnp.zeros_like(beta[:WA]), jnp.zeros_like(beta[:WA]),
                           jnp.zeros_like(beta[:WA])])
        wg_a = jnp.concatenate([_block_diag(rg_w[l]), _block_diag(ig_w[l])], axis=1).astype(BF16)
        ya = _rglru(pa, conv_w[l], vec_a, wg_a, g_a)
        yb = _hgrn(pb, hgrn_lb, beta[None, WA:WA + WB], wst, g_b, l, MIXER_ROWS)
        vec_c = jnp.stack([rwkv_w0[l], rwkv_a0[l], rwkv_k_k[l], rwkv_k_a[l],
                           rwkv_r_k[l].reshape(-1), rwkv_lnx_w[l], rwkv_lnx_b[l],
                           beta[WA + WB:]])
        zw = jnp.zeros_like(rwkv_w_up[l])
        lora_w = jnp.concatenate(
            [jnp.concatenate([rwkv_w_up[l], zw], axis=1),
             jnp.concatenate([zw, rwkv_a_up[l]], axis=1)], axis=0)
        yc = _rwkv(pc, rwkv_mu[l][None], vec_c, lora_w, rwkv_g_up[l], tri, g_c, MIXER_ROWS)
        x = _out_ffn(x, ya, yb, yc, mod[l], norm2_g[l][None], w_out[l].astype(BF16),
                     ffn_w_gate[l].astype(BF16), ffn_w_up[l].astype(BF16),
                     ffn_w_down[l].astype(BF16), final_g[None], l == L - 1, tm)
    return x
```

```python
import functools

import numpy as np
import jax
import jax.numpy as jnp
from jax import lax
from jax.experimental import pallas as pl
from jax.experimental.pallas import tpu as pltpu

F32 = jnp.float32
BF16 = jnp.bfloat16

HEAD = 64
HEAD_BITS = 6
LANES = 128
CONV_W = 4
LRU_C = 8.0
NORM_EPS = 1e-6
GN_EPS = 64e-5
CHUNK = 64
N_LEVELS = 6
RWKV_GROUP = 4
SCAN_EVERY = 2
MIXER_ROWS = 512
PROJ_ROWS = 512
HGRN_GROUP = 2
LORA_W = 64
LORA_A = 64
LORA_G = 128
VMEM_LIMIT = 56 * 1024 * 1024

NN = (((1,), (0,)), ((), ()))
NT = (((1,), (1,)), ((), ()))

P_ATT = (1, 1)
P_INV = (1, 1)
P_APP = (1, 1)
P_STATE = (1, 1)
P_HATT = (1, 1)
P_HAPP = (1, 1)


def _split(x, n):
    if x.dtype == BF16:
        return [x]
    parts = []
    r = x
    for i in range(n):
        p = r.astype(BF16)
        parts.append(p)
        if i + 1 < n:
            r = r - p.astype(F32)
    return parts


def _mm(a, b, dims=NN, na=2, nb=2):
    pa = _split(a, na)
    pb = _split(b, nb)
    order = max(len(pa), len(pb))
    acc = None
    for i, x in enumerate(pa):
        for j, y in enumerate(pb):
            if i + j < order:
                d = lax.dot_general(x, y, dims, preferred_element_type=F32)
                acc = d if acc is None else acc + d
    return acc


def _segsum(x, g01):
    xb = x.astype(BF16)
    return jnp.concatenate(
        [jnp.dot(xb[:, o:o + LANES], g01, preferred_element_type=F32)
         for o in range(0, x.shape[1], LANES)], axis=1)


def _stack(x, m0):
    return jnp.concatenate([jnp.where(m0, x, 0.0), jnp.where(m0, 0.0, x)], axis=0)


def _sigmoid(x):
    return 0.5 * jnp.tanh(0.5 * x) + 0.5


def _log1p_exp_neg_abs(x):
    return jnp.log(1.0 + jnp.exp(-jnp.abs(x)))


def _neg_expm1(x, ex):
    return -jnp.tanh(0.5 * x) * (ex + 1.0)


def _softplus(x):
    return jnp.maximum(x, 0.0) + _log1p_exp_neg_abs(x)


def _mod_kernel(c_ref, w_ref, b_ref, o_ref):
    c = c_ref[...]
    cs = c * _sigmoid(c)
    o_ref[0] = _mm(cs, w_ref[0]) + b_ref[0]


def _modulation(c, ada_w, ada_b):
    L, D, N = ada_w.shape
    B = c.shape[0]
    tn = N // 4
    return pl.pallas_call(
        _mod_kernel,
        grid=(L, N // tn),
        in_specs=[
            pl.BlockSpec((B, D), lambda l, j: (0, 0)),
            pl.BlockSpec((1, D, tn), lambda l, j: (l, 0, j)),
            pl.BlockSpec((1, 1, tn), lambda l, j: (l, 0, j)),
        ],
        out_specs=pl.BlockSpec((1, B, tn), lambda l, j: (l, 0, j)),
        out_shape=jax.ShapeDtypeStruct((L, B, N), F32),
        compiler_params=pltpu.CompilerParams(
            dimension_semantics=("parallel", "parallel"), vmem_limit_bytes=VMEM_LIMIT),
        name="modulation",
    )(c, ada_w, ada_b.reshape(L, 1, N))


def _inproj_kernel(widths, x_ref, mod_ref, g_ref, w_ref, pa_ref, pb_ref, pc_ref):
    x = x_ref[0]
    ms = jnp.mean(x * x, axis=-1, keepdims=True)
    h = x * lax.rsqrt(ms + NORM_EPS) * g_ref[...]
    h = h * (1.0 + mod_ref[0, 1:2, :]) + mod_ref[0, 0:1, :]
    hb = h.astype(BF16)
    off = 0
    for ref, wd in zip((pa_ref, pb_ref, pc_ref), widths):
        ref[0] = jnp.dot(hb, w_ref[:, off:off + wd], preferred_element_type=F32)
        off += wd


def _in_projection(x, mod_l, gain, w_bf, widths, tm):
    B, T, D = x.shape
    P = w_bf.shape[1]
    return pl.pallas_call(
        functools.partial(_inproj_kernel, widths),
        grid=(B, T // tm),
        in_specs=[
            pl.BlockSpec((1, tm, D), lambda b, t: (b, t, 0)),
            pl.BlockSpec((1, 6, D), lambda b, t: (b, 0, 0)),
            pl.BlockSpec((1, D), lambda b, t: (0, 0)),
            pl.BlockSpec((D, P), lambda b, t: (0, 0), pipeline_mode=pl.Buffered(1)),
        ],
        out_specs=[pl.BlockSpec((1, tm, wd), lambda b, t: (b, t, 0)) for wd in widths],
        out_shape=[jax.ShapeDtypeStruct((B, T, wd), F32) for wd in widths],
        compiler_params=pltpu.CompilerParams(
            dimension_semantics=("parallel", "parallel"), vmem_limit_bytes=VMEM_LIMIT),
        name="in_projection",
    )(x, mod_l, gain, w_bf)


def _rglru_kernel(tc, pa_ref, cw_ref, vec_ref, wg_ref, g_ref, o_ref):
    T = pa_ref.shape[1]
    W = o_ref.shape[2]
    conv_b = vec_ref[0:1, :]
    rg_b = vec_ref[1:2, :]
    ig_b = vec_ref[2:3, :]
    sp_lam = _softplus(-vec_ref[3:4, :])
    beta = vec_ref[4:5, :]
    row = lax.broadcasted_iota(jnp.int32, (tc, W), 0)
    g01 = g_ref[...]
    wg = wg_ref[...]

    def body(i, hprev):
        t0 = pl.multiple_of(i * tc, tc)
        xa = pa_ref[0, pl.ds(t0, tc), 0:W]
        ya = pa_ref[0, pl.ds(t0, tc), W:2 * W]
        tp = pl.multiple_of(jnp.maximum(t0 - 8, 0), 8)
        prev8 = jnp.where(i > 0, pa_ref[0, pl.ds(tp, 8), 0:W], 0.0)
        win = jnp.concatenate([prev8, xa], axis=0)
        u = conv_b + cw_ref[CONV_W - 1:CONV_W, :] * xa
        for j in range(1, CONV_W):
            u = u + cw_ref[CONV_W - 1 - j:CONV_W - j, :] * pltpu.roll(win, j, 0)[8:]
        z = jnp.dot(u.astype(BF16), wg, preferred_element_type=F32)
        r = _sigmoid(z[:, 0:W] + rg_b)
        ig = _sigmoid(z[:, W:2 * W] + ig_b)
        log_a = (-LRU_C) * r * sp_lam
        a = jnp.exp(log_a)
        mult = jnp.sqrt(_neg_expm1(2.0 * log_a, a * a))
        mult = jnp.where(row + t0 == 0, 1.0, mult)
        bv = mult * (ig * u)
        s = 1
        while s < tc:
            keep = row >= s
            a_s = jnp.where(keep, pltpu.roll(a, s, 0), 1.0)
            b_s = jnp.where(keep, pltpu.roll(bv, s, 0), 0.0)
            bv = a * b_s + bv
            a = a * a_s
            s *= 2
        h = a * hprev + bv
        y = h * jax.nn.gelu(ya)
        ms = _segsum(y * y, g01) * (1.0 / HEAD)
        o_ref[0, pl.ds(t0, tc), :] = (y * lax.rsqrt(ms + NORM_EPS) * beta).astype(o_ref.dtype)
        return h[tc - 1:tc, :]

    lax.fori_loop(0, T // tc, body, jnp.zeros((1, W), F32))


def _rglru(pa, conv_w, vecs, wg_bf, g01, tc=128):
    B, T, W2 = pa.shape
    W = W2 // 2
    return pl.pallas_call(
        functools.partial(_rglru_kernel, tc),
        grid=(B,),
        in_specs=[
            pl.BlockSpec((1, T, W2), lambda b: (b, 0, 0)),
            pl.BlockSpec(conv_w.shape, lambda b: (0, 0)),
            pl.BlockSpec(vecs.shape, lambda b: (0, 0)),
            pl.BlockSpec(wg_bf.shape, lambda b: (0, 0)),
            pl.BlockSpec(g01.shape, lambda b: (0, 0)),
        ],
        out_specs=pl.BlockSpec((1, T, W), lambda b: (b, 0, 0)),
        out_shape=jax.ShapeDtypeStruct((B, T, W), BF16),
        compiler_params=pltpu.CompilerParams(
            dimension_semantics=("parallel",), vmem_limit_bytes=VMEM_LIMIT),
        name="rglru",
    )(pa, conv_w, vecs, wg_bf, g01)


def _hgrn_kernel(layer, pb_ref, lb_ref, beta_ref, wst_ref, g_ref, o_ref,
                 s_ref, qin_ref, oi_ref, kv_ref, dec_ref):
    C = CHUNK
    R = wst_ref.shape[1]
    G = R // C
    TB = pb_ref.shape[1]
    W = o_ref.shape[2]
    n_pairs = W // LANES

    @pl.when(pl.program_id(1) == 0)
    def _():
        s_ref[...] = jnp.zeros_like(s_ref)

    lbr = lb_ref[...]
    n_layers = lbr.shape[0]
    mx = lbr[0:1, :]
    for i in range(1, n_layers):
        mx = jnp.maximum(mx, lbr[i:i + 1, :])
    es = [jnp.exp(lbr[i:i + 1, :] - mx) for i in range(n_layers)]
    tot = es[0]
    for i in range(1, n_layers):
        tot = tot + es[i]
    sm = [e / tot for e in es]
    cum = sm[0]
    for i in range(1, layer + 1):
        cum = cum + sm[i]
    lb = cum - sm[0]
    log_lb = jnp.log(lb)
    log_1mlb = jnp.log(1.0 - lb)

    beta = beta_ref[...]
    g01 = g_ref[...]
    wst = wst_ref[...]
    lane = lax.broadcasted_iota(jnp.int32, (C, LANES), 1)
    m0 = lane < HEAD
    tw = lax.broadcasted_iota(jnp.int32, (C, 2 * C), 0)
    sw = lax.broadcasted_iota(jnp.int32, (C, 2 * C), 1) & (C - 1)
    eye = tw == sw
    level = []
    for lev in range(N_LEVELS):
        th = tw >> (N_LEVELS - 1 - lev)
        sh = sw >> (N_LEVELS - 1 - lev)
        level.append((th == sh + 1) & ((sh & 1) == 0))
    r2 = lax.broadcasted_iota(jnp.int32, (LANES, LANES), 0)
    c2 = lax.broadcasted_iota(jnp.int32, (LANES, LANES), 1)
    same_head = (r2 >> HEAD_BITS) == (c2 >> HEAD_BITS)

    odd_row = (lax.broadcasted_iota(jnp.int32, (C, LANES), 0) & 1) == 1
    n_kinds = wst_ref.shape[0] // R

    def elem_steps(gi, out):
        t0 = gi * R
        q = pb_ref[0, t0:t0 + R, 0:W] * (HEAD ** -0.5)
        fl = pb_ref[0, t0:t0 + R, W:2 * W]
        v = pb_ref[0, t0:t0 + R, 2 * W:3 * W]
        lc = log_1mlb + (jnp.minimum(fl, 0.0) - _log1p_exp_neg_abs(fl))
        log_f = jnp.maximum(log_lb, lc) + _log1p_exp_neg_abs(log_lb - lc)
        expo = _mm(wst, log_f, NN, na=1, nb=2)
        yield
        f = jnp.exp(log_f)
        kg = _neg_expm1(log_f, f)
        ex = jnp.exp(expo)
        qin_ref[t0:t0 + R, :] = q * ex[0:R]
        for g in range(G):
            dec_ref[gi * G + g] = jnp.broadcast_to(ex[(g + 1) * C - 1:(g + 1) * C, :], (8, W))
        out.append((q, kg, v, f, ex))

    def mm_stages(gi, vals):
        q, kg, v, f, ex = vals
        items = [(g, p) for g in range(G) for p in range(n_pairs)]
        n = range(len(items))

        def piece(x, g, p, base=0):
            return x[base + g * C:base + (g + 1) * C, p * LANES:(p + 1) * LANES]

        q_p = [piece(q, g, p) for g, p in items]
        v_p = [piece(v, g, p) for g, p in items]
        k_s = [_stack(piece(kg, g, p), m0) for g, p in items]
        amat = [jnp.where(eye, _mm(q_p[j], k_s[j], NT, *P_HATT), 0.0) for j in n]
        yield
        for lev in range(N_LEVELS):
            if 2 + lev < n_kinds:
                es = [piece(ex, g, p, (2 + lev) * R) for g, p in items]
            else:
                es = [jnp.where(odd_row, piece(f, g, p), 1.0) for g, p in items]
            amat = [jnp.where(level[lev],
                              _mm(q_p[j] * es[j],
                                  k_s[j] * jnp.concatenate([es[j], es[j]], axis=0), NT, *P_HATT),
                              amat[j]) for j in n]
            yield
        oi = [_mm(amat[j], _stack(v_p[j], m0), NN, *P_HAPP) for j in n]
        yield
        k_dec = kg * ex[R:2 * R]
        kv = [_mm(v_p[j].T, piece(k_dec, g, p), NN, *P_HAPP) for j, (g, p) in enumerate(items)]
        for g in range(G):
            t0 = (gi * G + g) * C
            oi_ref[t0:t0 + C, :] = jnp.concatenate(
                [oi[g * n_pairs + p] for p in range(n_pairs)], axis=1)
        for j, (g, p) in enumerate(items):
            kv_ref[(gi * G + g) * n_pairs + p] = jnp.where(same_head, kv[j], 0.0)
        yield

    def scan_chunk(ci):
        t0 = ci * C
        dec = dec_ref[ci]
        outs = []
        for p in range(n_pairs):
            sl = slice(p * LANES, (p + 1) * LANES)
            st = s_ref[p]
            outs.append(_mm(qin_ref[t0:t0 + C, sl], st, NT, *P_HAPP))
            s_ref[p] = st * dec[0:1, sl] + kv_ref[ci * n_pairs + p]
        oi_ref[t0:t0 + C, :] = oi_ref[t0:t0 + C, :] + jnp.concatenate(outs, axis=1)

    n_groups = TB // R
    vals = []
    for _ in elem_steps(0, vals):
        pass
    ready = []
    for gi in range(n_groups):
        nxt = elem_steps(gi + 1, vals) if gi + 1 < n_groups else iter(())
        for si, _ in enumerate(mm_stages(gi, vals[gi])):
            next(nxt, None)
            if ready and si % SCAN_EVERY == SCAN_EVERY - 1:
                scan_chunk(ready.pop(0))
        for _ in nxt:
            pass
        ready += [gi * G + g for g in range(G)]
    for ci in ready:
        scan_chunk(ci)

    o = oi_ref[...]
    g = pb_ref[0, :, 3 * W:4 * W]
    ms = _segsum(o * o, g01) * (1.0 / HEAD)
    o = o * lax.rsqrt(ms + NORM_EPS) * (g * _sigmoid(g)) * beta
    o_ref[0] = o.astype(o_ref.dtype)


def _hgrn_weights(group):
    C = CHUNK
    R = group * C
    w = np.zeros(((2 + N_LEVELS - 1) * R, R), np.float32)
    for g in range(group):
        o = g * C
        for r in range(C):
            w[o + r, o:o + r + 1] = 1.0
            w[R + o + r, o + r + 1:o + C] = 1.0
            for lev in range(N_LEVELS - 1):
                n = C >> lev
                m = (r // n) * n + n // 2 - 1
                if r % n >= n // 2:
                    w[(2 + lev) * R + o + r, o + m + 1:o + r + 1] = 1.0
                else:
                    w[(2 + lev) * R + o + r, o + r + 1:o + m + 1] = 1.0
    return w


def _hgrn(pb, hgrn_lb, beta_b, wst, g01, layer, tb):
    B, T, W4 = pb.shape
    W = W4 // 4
    return pl.pallas_call(
        functools.partial(_hgrn_kernel, layer),
        grid=(B, T // tb),
        in_specs=[
            pl.BlockSpec((1, tb, W4), lambda b, t: (b, t, 0)),
            pl.BlockSpec(hgrn_lb.shape, lambda b, t: (0, 0)),
            pl.BlockSpec(beta_b.shape, lambda b, t: (0, 0)),
            pl.BlockSpec(wst.shape, lambda b, t: (0, 0)),
            pl.BlockSpec(g01.shape, lambda b, t: (0, 0)),
        ],
        out_specs=pl.BlockSpec((1, tb, W), lambda b, t: (b, t, 0)),
        out_shape=jax.ShapeDtypeStruct((B, T, W), BF16),
        scratch_shapes=[pltpu.VMEM((W // LANES, LANES, LANES), F32),
                        pltpu.VMEM((tb, W), F32), pltpu.VMEM((tb, W), F32),
                        pltpu.VMEM(((tb // CHUNK) * (W // LANES), LANES, LANES), F32),
                        pltpu.VMEM((tb // CHUNK, 8, W), F32)],
        compiler_params=pltpu.CompilerParams(
            dimension_semantics=("parallel", "arbitrary"), vmem_limit_bytes=VMEM_LIMIT),
        name="hgrn2",
    )(pb, hgrn_lb, beta_b, wst, g01)


def _rwkv_kernel(pc_ref, mu_ref, vec_ref, lora_ref, gup_ref, tri_ref, g_ref, o_ref,
                 s_ref, prev_ref, rp_ref, y0_ref, mx_ref, nx_ref, bonus_ref, gate_ref, y_ref):
    C = CHUNK
    R = tri_ref.shape[0]
    G = R // C
    TB = pc_ref.shape[1]
    W = o_ref.shape[2]
    n_pairs = W // LANES

    @pl.when(pl.program_id(1) == 0)
    def _():
        s_ref[...] = jnp.zeros_like(s_ref)
        prev_ref[...] = jnp.zeros_like(prev_ref)

    mu = mu_ref[...]
    w0 = vec_ref[0:1, :]
    a0 = vec_ref[1:2, :]
    k_k = vec_ref[2:3, :]
    k_a = vec_ref[3:4, :]
    r_k = vec_ref[4:5, :]
    lnx_w = vec_ref[5:6, :]
    lnx_b = vec_ref[6:7, :]
    beta = vec_ref[7:8, :]
    lora = _split(lora_ref[...], 1)
    gup = _split(gup_ref[...], 1)
    tri = tri_ref[...]
    g01 = g_ref[...]
    lane = lax.broadcasted_iota(jnp.int32, (C, LANES), 1)
    lane_r = lax.broadcasted_iota(jnp.int32, (R, LANES), 1)
    m0 = lane < HEAD
    tw = lax.broadcasted_iota(jnp.int32, (C, 2 * C), 0)
    sw = lax.broadcasted_iota(jnp.int32, (C, 2 * C), 1) & (C - 1)
    strict = tw > sw
    incl = tw >= sw
    eye_w = jnp.where(tw == sw, 1.0, 0.0)
    r2 = lax.broadcasted_iota(jnp.int32, (2 * C, 2 * C), 0)
    c2 = lax.broadcasted_iota(jnp.int32, (2 * C, 2 * C), 1)
    same_head = (r2 >> HEAD_BITS) == (c2 >> HEAD_BITS)
    eye_f = jnp.where(r2 == c2, 1.0, 0.0)
    o_w = 3 * W
    o_g = o_w + LORA_W + LORA_A

    def mm_parts(x, parts):
        return jnp.dot(x.astype(BF16), parts[0], preferred_element_type=F32)

    def elem_steps(gi, out):
        t0 = gi * R
        pc = pc_ref[0, t0:t0 + R, :]
        prev8 = prev_ref[...] if gi == 0 else pc_ref[0, t0 - 8:t0, :]
        prev = pltpu.roll(jnp.concatenate([prev8, pc], axis=0), 1, 0)[8:]
        ps = pc + (prev - pc) * mu
        r = ps[:, 0:W]
        k = ps[:, W:2 * W]
        v = ps[:, 2 * W:3 * W]
        z = ps[:, o_w:o_g]
        z = jnp.where(lane_r < LORA_W, jnp.tanh(z), z)
        lo = mm_parts(z, lora)
        yield
        gate = mm_parts(_sigmoid(ps[:, o_g:o_g + LORA_G]), gup)
        gate_ref[t0:t0 + R, :] = gate * beta
        yield
        w_raw = -_softplus(-(w0 + lo[:, 0:W])) - 0.5
        lw = -jnp.exp(w_raw)
        gam = _mm(tri, lw, NN, na=1, nb=2)
        yield
        a_sig = _sigmoid(a0 + lo[:, W:2 * W])
        kk = k * k_k
        kk = kk * lax.rsqrt(_segsum(kk * kk, g01) + 1e-12)
        yield
        k2 = k * (1.0 + (a_sig - 1.0) * k_a)
        bonus_ref[t0:t0 + R, :] = _segsum(r * k2 * r_k, g01) * v
        yield
        e_in = jnp.exp(gam)
        e_ex = jnp.exp(gam - lw)
        e_neg = jnp.exp(-gam)
        out.append((-kk * e_ex, r * e_in, kk * a_sig * e_neg, k2 * e_neg, v, e_in))

    def mm_stages(gi, vals):
        at, rt, bt, kt, v, e_in = vals
        items = [(g, p) for g in range(G) for p in range(n_pairs)]
        n = range(len(items))

        def pieces(x):
            return [x[g * C:(g + 1) * C, p * LANES:(p + 1) * LANES] for g, p in items]

        def st(x):
            return _stack(x, m0)

        a_p, r_p, b_p, k_p, v_p = pieces(at), pieces(rt), pieces(bt), pieces(kt), pieces(v)
        v_s = [st(v_p[j]) for j in n]
        att = [_mm(jnp.concatenate([a_p[j], r_p[j]], axis=0),
                   jnp.concatenate([st(b_p[j]), st(k_p[j])], axis=0), NT, *P_ATT)
               for j in n]
        yield
        lmat = [jnp.where(strict, att[j][0:C, 0:2 * C], 0.0) for j in n]
        ak = [jnp.where(strict, att[j][0:C, 2 * C:4 * C], 0.0) for j in n]
        rb = [jnp.where(incl, att[j][C:2 * C, 0:2 * C], 0.0) for j in n]
        rk = [jnp.where(incl, att[j][C:2 * C, 2 * C:4 * C], 0.0) for j in n]
        akv = [_mm(ak[j], v_s[j], NN, *P_APP) for j in n]
        yield
        pw = [_mm(lmat[j], st(lmat[j]), NN, *P_INV) for j in n]
        xs = [eye_w + lmat[j] for j in n]
        yield
        for _ in range(N_LEVELS - 2):
            z = [_mm(pw[j], jnp.concatenate([st(xs[j]), st(pw[j])], axis=1), NN, *P_INV)
                 for j in n]
            xs = [xs[j] + z[j][:, 0:2 * C] for j in n]
            pw = [z[j][:, 2 * C:4 * C] for j in n]
            yield
        xs = [xs[j] + _mm(pw[j], st(xs[j]), NN, *P_INV) for j in n]
        yield
        au = [_mm(xs[j], jnp.concatenate([st(a_p[j]), st(akv[j])], axis=1), NN, *P_APP)
              for j in n]
        ap = [au[j][:, 0:LANES] for j in n]
        u0 = [au[j][:, LANES:2 * LANES] for j in n]
        yield
        ry = [_mm(rb[j], jnp.concatenate([st(ap[j]), st(u0[j])], axis=1), NN, *P_APP)
              for j in n]
        rkv = [_mm(rk[j], v_s[j], NN, *P_APP) for j in n]
        yield
        mx = [_mm(ap[j].T, b_p[j], NN, *P_APP) for j in n]
        nx = [_mm(jnp.concatenate([u0[j], v_p[j]], axis=0).T,
                  jnp.concatenate([b_p[j], k_p[j]], axis=0), NN, *P_APP) for j in n]
        for j, (g, p) in enumerate(items):
            idx = (gi * G + g) * n_pairs + p
            glast = e_in[(g + 1) * C - 1:(g + 1) * C, p * LANES:(p + 1) * LANES]
            rp_ref[idx] = r_p[j] + ry[j][:, 0:LANES]
            y0_ref[idx] = ry[j][:, LANES:2 * LANES] + rkv[j]
            mx_ref[idx] = (eye_f + jnp.where(same_head, mx[j], 0.0)) * glast
            nx_ref[idx] = jnp.where(same_head, nx[j], 0.0) * glast
        yield

    def scan_chunk(ci):
        pairs = range(n_pairs)
        sts = [s_ref[p] for p in pairs]
        ys = [_mm(rp_ref[ci * n_pairs + p], sts[p], NT, *P_STATE) for p in pairs]
        sn = [_mm(sts[p], mx_ref[ci * n_pairs + p], NN, *P_STATE) for p in pairs]
        for p in pairs:
            s_ref[p] = sn[p] + nx_ref[ci * n_pairs + p]
        y_ref[ci * C:(ci + 1) * C, :] = jnp.concatenate(
            [ys[p] + y0_ref[ci * n_pairs + p] for p in pairs], axis=1)

    n_groups = TB // R
    vals = []
    for _ in elem_steps(0, vals):
        pass
    ready = []
    for gi in range(n_groups):
        nxt = elem_steps(gi + 1, vals) if gi + 1 < n_groups else iter(())
        for si, _ in enumerate(mm_stages(gi, vals[gi])):
            next(nxt, None)
            if ready and si % SCAN_EVERY == SCAN_EVERY - 1:
                scan_chunk(ready.pop(0))
        for _ in nxt:
            pass
        ready += [gi * G + g for g in range(G)]
    for ci in ready:
        scan_chunk(ci)
    prev_ref[...] = pc_ref[0, TB - 8:TB, :]

    y = y_ref[...]
    mean = _segsum(y, g01) * (1.0 / HEAD)
    d = y - mean
    var = _segsum(d * d, g01) * (1.0 / HEAD)
    yn = d * lax.rsqrt(var + GN_EPS) * lnx_w + lnx_b
    o_ref[0] = ((yn + bonus_ref[...]) * gate_ref[...]).astype(o_ref.dtype)


def _rwkv(pc, mu, vecs, lora_w, g_up, tri, g01, tb):
    B, T, PC = pc.shape
    W = vecs.shape[1]
    n_mats = (tb // CHUNK) * (W // LANES)
    mats = pltpu.VMEM((n_mats, LANES, LANES), F32)
    rows = pltpu.VMEM((n_mats, CHUNK, LANES), F32)
    return pl.pallas_call(
        _rwkv_kernel,
        grid=(B, T // tb),
        in_specs=[
            pl.BlockSpec((1, tb, PC), lambda b, t: (b, t, 0)),
            pl.BlockSpec(mu.shape, lambda b, t: (0, 0)),
            pl.BlockSpec(vecs.shape, lambda b, t: (0, 0)),
            pl.BlockSpec(lora_w.shape, lambda b, t: (0, 0)),
            pl.BlockSpec(g_up.shape, lambda b, t: (0, 0)),
            pl.BlockSpec(tri.shape, lambda b, t: (0, 0)),
            pl.BlockSpec(g01.shape, lambda b, t: (0, 0)),
        ],
        out_specs=pl.BlockSpec((1, tb, W), lambda b, t: (b, t, 0)),
        out_shape=jax.ShapeDtypeStruct((B, T, W), BF16),
        scratch_shapes=[pltpu.VMEM((W // LANES, LANES, LANES), F32),
                        pltpu.VMEM((8, PC), F32),
                        rows, rows, mats, mats,
                        pltpu.VMEM((tb, W), F32), pltpu.VMEM((tb, W), F32),
                        pltpu.VMEM((tb, W), F32)],
        compiler_params=pltpu.CompilerParams(
            dimension_semantics=("parallel", "arbitrary"), vmem_limit_bytes=VMEM_LIMIT),
        name="rwkv7",
    )(pc, mu, vecs, lora_w, g_up, tri, g01)


def _ffn_kernel(final, widths, x_ref, ya_ref, yb_ref, yc_ref, mod_ref, g_ref, wo_ref,
                wg_ref, wu_ref, wd_ref, fg_ref, o_ref):
    x = x_ref[0]
    acc = None
    off = 0
    for ref, wd in zip((ya_ref, yb_ref, yc_ref), widths):
        d = jnp.dot(ref[0], wo_ref[off:off + wd, :], preferred_element_type=F32)
        acc = d if acc is None else acc + d
        off += wd
    x1 = x + mod_ref[0, 2:3, :] * acc
    ms = jnp.mean(x1 * x1, axis=-1, keepdims=True)
    h = x1 * lax.rsqrt(ms + NORM_EPS) * g_ref[...]
    h = h * (1.0 + mod_ref[0, 4:5, :]) + mod_ref[0, 3:4, :]
    hb = h.astype(BF16)
    gt = jnp.dot(hb, wg_ref[...], preferred_element_type=F32)
    up = jnp.dot(hb, wu_ref[...], preferred_element_type=F32)
    act = (gt * _sigmoid(gt) * up).astype(BF16)
    dn = jnp.dot(act, wd_ref[...], preferred_element_type=F32)
    x2 = x1 + mod_ref[0, 5:6, :] * dn
    if final:
        ms2 = jnp.mean(x2 * x2, axis=-1, keepdims=True)
        x2 = x2 * lax.rsqrt(ms2 + NORM_EPS) * fg_ref[...]
    o_ref[0] = x2


def _out_ffn(x, ya, yb, yc, mod_l, gain, wo, wg, wu, wd, final_g, final, tm):
    B, T, D = x.shape
    widths = (ya.shape[2], yb.shape[2], yc.shape[2])
    const = lambda shape: pl.BlockSpec(shape, lambda b, t: (0, 0), pipeline_mode=pl.Buffered(1))
    return pl.pallas_call(
        functools.partial(_ffn_kernel, final, widths),
        grid=(B, T // tm),
        in_specs=[
            pl.BlockSpec((1, tm, D), lambda b, t: (b, t, 0)),
            pl.BlockSpec((1, tm, widths[0]), lambda b, t: (b, t, 0)),
            pl.BlockSpec((1, tm, widths[1]), lambda b, t: (b, t, 0)),
            pl.BlockSpec((1, tm, widths[2]), lambda b, t: (b, t, 0)),
            pl.BlockSpec((1, 6, D), lambda b, t: (b, 0, 0)),
            pl.BlockSpec((1, D), lambda b, t: (0, 0)),
            const(wo.shape), const(wg.shape), const(wu.shape), const(wd.shape),
            pl.BlockSpec((1, D), lambda b, t: (0, 0)),
        ],
        out_specs=pl.BlockSpec((1, tm, D), lambda b, t: (b, t, 0)),
        out_shape=jax.ShapeDtypeStruct((B, T, D), F32),
        compiler_params=pltpu.CompilerParams(
            dimension_semantics=("parallel", "parallel"), vmem_limit_bytes=VMEM_LIMIT),
        name="out_ffn",
    )(x, ya, yb, yc, mod_l, gain, wo, wg, wu, wd, final_g)


def _block_diag(w):
    G, n, _ = w.shape
    eye = jnp.eye(G, dtype=w.dtype)
    return (eye[:, None, :, None] * w[:, :, None, :]).reshape(G * n, G * n)


def _seg_ones():
    idx = np.arange(LANES) // HEAD
    return jnp.asarray((idx[:, None] == idx[None, :]).astype(np.float32), dtype=BF16)


def kernel(x, c, norm1_g, norm2_g, ada_w, ada_b, w_in, conv_w, conv_b, rg_w, rg_b, ig_w, ig_b, lru_lam, hgrn_lb, rwkv_mu, rwkv_w0, rwkv_w_up, rwkv_a0, rwkv_a_up, rwkv_g_up, rwkv_k_k, rwkv_k_a, rwkv_r_k, rwkv_lnx_w, rwkv_lnx_b, mix_beta, w_out, ffn_w_gate, ffn_w_up, ffn_w_down, final_g):
    B, T, D = x.shape
    L = w_in.shape[0]
    WA = conv_w.shape[2]
    WB = hgrn_lb.shape[1]
    WC = rwkv_w0.shape[1]
    widths_p = (2 * WA, 4 * WB, rwkv_mu.shape[1])
    tm = PROJ_ROWS

    mod = _modulation(c, ada_w, ada_b).reshape(L, B, 6, D)
    wst = jnp.asarray(_hgrn_weights(HGRN_GROUP), dtype=BF16)
    tri = jnp.asarray(np.kron(np.eye(RWKV_GROUP, dtype=np.float32),
                              np.tril(np.ones((CHUNK, CHUNK), np.float32))), dtype=BF16)
    g_a = g_b = g_c = _seg_ones()

    for l in range(L):
        pa, pb, pc = _in_projection(x, mod[l], norm1_g[l][None], w_in[l].astype(BF16),
                                    widths_p, tm)
        beta = mix_beta[l]
        vec_a = jnp.stack([conv_b[l], rg_b[l], ig_b[l], lru_lam[l], beta[:WA],
                           jnp.zeros_like(beta[:WA]), jnp.zeros_like(beta[:WA]),
                           jnp.zeros_like(beta[:WA])])
        wg_a = jnp.concatenate([_block_diag(rg_w[l]), _block_diag(ig_w[l])], axis=1).astype(BF16)
        ya = _rglru(pa, conv_w[l], vec_a, wg_a, g_a)
        yb = _hgrn(pb, hgrn_lb, beta[None, WA:WA + WB], wst, g_b, l, MIXER_ROWS)
        vec_c = jnp.stack([rwkv_w0[l], rwkv_a0[l], rwkv_k_k[l], rwkv_k_a[l],
                           rwkv_r_k[l].reshape(-1), rwkv_lnx_w[l], rwkv_lnx_b[l],
                           beta[WA + WB:]])
        zw = jnp.zeros_like(rwkv_w_up[l])
        lora_w = jnp.concatenate(
            [jnp.concatenate([rwkv_w_up[l], zw], axis=1),
             jnp.concatenate([zw, rwkv_a_up[l]], axis=1)], axis=0)
        yc = _rwkv(pc, rwkv_mu[l][None], vec_c, lora_w, rwkv_g_up[l], tri, g_c, 2 * MIXER_ROWS)
        x = _out_ffn(x, ya, yb, yc, mod[l], norm2_g[l][None], w_out[l].astype(BF16),
                     ffn_w_gate[l].astype(BF16), ffn_w_up[l].astype(BF16),
                     ffn_w_down[l].astype(BF16), final_g[None], l == L - 1, tm)
    return x
```

```python
import functools

import numpy as np
import jax
import jax.numpy as jnp
from jax import lax
from jax.experimental import pallas as pl
from jax.experimental.pallas import tpu as pltpu

F32 = jnp.float32
BF16 = jnp.bfloat16

HEAD = 64
HEAD_BITS = 6
LANES = 128
SUBLANES = 8
CONV_W = 4
LRU_C = 8.0
NORM_EPS = 1e-6
GN_EPS = 64e-5
CHUNK = 64
N_LEVELS = 6
RWKV_GROUP = 4
SCAN_EVERY = 2
IN_ROWS = 512
FFN_ROWS = 512
HGRN_ROWS = 1024
RWKV_ROWS = 1024
HGRN_GROUP = 2
LORA_W = 64
LORA_A = 64
LORA_G = 128
VMEM_LIMIT = 56 * 1024 * 1024

NN = (((1,), (0,)), ((), ()))
NT = (((1,), (1,)), ((), ()))

P_ATT = (1, 1)
P_INV = (1, 1)
P_APP = (1, 1)
P_STATE = (1, 1)
P_HATT = (1, 1)
P_HAPP = (1, 1)


def _split(x, n):
    if x.dtype == BF16:
        return [x]
    parts = []
    r = x
    for i in range(n):
        p = r.astype(BF16)
        parts.append(p)
        if i + 1 < n:
            r = r - p.astype(F32)
    return parts


def _mm(a, b, dims=NN, na=2, nb=2):
    pa = _split(a, na)
    pb = _split(b, nb)
    order = max(len(pa), len(pb))
    acc = None
    for i, x in enumerate(pa):
        for j, y in enumerate(pb):
            if i + j < order:
                d = lax.dot_general(x, y, dims, preferred_element_type=F32)
                acc = d if acc is None else acc + d
    return acc


def _segsum(x, g01):
    xb = x.astype(BF16)
    return jnp.concatenate(
        [jnp.dot(xb[:, o:o + LANES], g01, preferred_element_type=F32)
         for o in range(0, x.shape[1], LANES)], axis=1)


def _stack(x, m0):
    return jnp.concatenate([jnp.where(m0, x, 0.0), jnp.where(m0, 0.0, x)], axis=0)


def _sigmoid(x):
    return 0.5 * jnp.tanh(0.5 * x) + 0.5


def _log1p_exp_neg_abs(x):
    return jnp.log(1.0 + jnp.exp(-jnp.abs(x)))


def _neg_expm1(x, ex):
    return -jnp.tanh(0.5 * x) * (ex + 1.0)


def _softplus(x):
    return jnp.maximum(x, 0.0) + _log1p_exp_neg_abs(x)


def _mod_kernel(c_ref, w_ref, b_ref, o_ref):
    c = c_ref[...]
    cs = c * _sigmoid(c)
    o_ref[0] = _mm(cs, w_ref[0]) + b_ref[0]


def _modulation(c, ada_w, ada_b):
    L, D, N = ada_w.shape
    B = c.shape[0]
    tn = N // 4
    return pl.pallas_call(
        _mod_kernel,
        grid=(L, N // tn),
        in_specs=[
            pl.BlockSpec((B, D), lambda l, j: (0, 0)),
            pl.BlockSpec((1, D, tn), lambda l, j: (l, 0, j)),
            pl.BlockSpec((1, 1, tn), lambda l, j: (l, 0, j)),
        ],
        out_specs=pl.BlockSpec((1, B, tn), lambda l, j: (l, 0, j)),
        out_shape=jax.ShapeDtypeStruct((L, B, N), F32),
        compiler_params=pltpu.CompilerParams(
            dimension_semantics=("parallel", "parallel"), vmem_limit_bytes=VMEM_LIMIT),
        name="modulation",
    )(c, ada_w, ada_b.reshape(L, 1, N))


def _inproj_kernel(widths, x_ref, mod_ref, g_ref, w_ref, cw_ref, vec_ref, wg_ref, seg_ref,
                   ya_ref, pb_ref, pc_ref, xc_ref, hc_ref):
    wa, wb, wc = widths
    W = wa // 2
    tm = x_ref.shape[1]
    first = pl.program_id(1) == 0

    @pl.when(first)
    def _():
        xc_ref[...] = jnp.zeros_like(xc_ref)
        hc_ref[...] = jnp.zeros_like(hc_ref)

    x = x_ref[0]
    ms = jnp.mean(x * x, axis=-1, keepdims=True)
    h = x * lax.rsqrt(ms + NORM_EPS) * g_ref[...]
    h = h * (1.0 + mod_ref[0, 1:2, :]) + mod_ref[0, 0:1, :]
    hb = h.astype(BF16)

    conv_b = vec_ref[0:1, :]
    rg_b = vec_ref[1:2, :]
    ig_b = vec_ref[2:3, :]
    sp_lam = _softplus(-vec_ref[3:4, :])
    beta = vec_ref[4:5, :]
    row = lax.broadcasted_iota(jnp.int32, (tm, W), 0)

    pa = jnp.dot(hb, w_ref[:, 0:wa], preferred_element_type=F32)
    xa = pa[:, 0:W]
    ya = pa[:, W:wa]
    win = jnp.concatenate([xc_ref[...], xa], axis=0)
    xc_ref[...] = xa[tm - SUBLANES:tm, :]
    u = conv_b + cw_ref[CONV_W - 1:CONV_W, :] * xa
    for j in range(1, CONV_W):
        u = u + cw_ref[CONV_W - 1 - j:CONV_W - j, :] * pltpu.roll(win, j, 0)[SUBLANES:]
    pb_ref[0] = jnp.dot(hb, w_ref[:, wa:wa + wb], preferred_element_type=F32)
    z = jnp.dot(u.astype(BF16), wg_ref[...], preferred_element_type=F32)
    pc_ref[0] = jnp.dot(hb, w_ref[:, wa + wb:wa + wb + wc], preferred_element_type=F32)

    r = _sigmoid(z[:, 0:W] + rg_b)
    ig = _sigmoid(z[:, W:2 * W] + ig_b)
    log_a = (-LRU_C) * r * sp_lam
    a = jnp.exp(log_a)
    mult = jnp.sqrt(_neg_expm1(2.0 * log_a, a * a))
    mult = jnp.where(row + pl.program_id(1) * tm == 0, 1.0, mult)
    bv = mult * (ig * u)
    s = 1
    while s < SUBLANES:
        keep = (row & (SUBLANES - 1)) >= s
        a_s = jnp.where(keep, pltpu.roll(a, s, 0), 1.0)
        b_s = jnp.where(keep, pltpu.roll(bv, s, 0), 0.0)
        bv = a * b_s + bv
        a = a * a_s
        s *= 2
    hprev = hc_ref[0:1, :]
    hs = []
    for j in range(tm // SUBLANES):
        blk = a[j * SUBLANES:(j + 1) * SUBLANES] * hprev + bv[j * SUBLANES:(j + 1) * SUBLANES]
        hs.append(blk)
        hprev = blk[SUBLANES - 1:SUBLANES, :]
    hc_ref[...] = jnp.broadcast_to(hprev, hc_ref.shape)
    y = jnp.concatenate(hs, axis=0) * jax.nn.gelu(ya)
    ms_a = _segsum(y * y, seg_ref[...]) * (1.0 / HEAD)
    ya_ref[0] = (y * lax.rsqrt(ms_a + NORM_EPS) * beta).astype(ya_ref.dtype)


def _in_projection(x, mod_l, gain, w_bf, conv_w, vecs, wg_bf, g01, widths, tm):
    B, T, D = x.shape
    P = w_bf.shape[1]
    W = widths[0] // 2
    small = lambda a: pl.BlockSpec(a.shape, lambda b, t: (0, 0))
    return pl.pallas_call(
        functools.partial(_inproj_kernel, widths),
        grid=(B, T // tm),
        in_specs=[
            pl.BlockSpec((1, tm, D), lambda b, t: (b, t, 0)),
            pl.BlockSpec((1, 6, D), lambda b, t: (b, 0, 0)),
            pl.BlockSpec((1, D), lambda b, t: (0, 0)),
            pl.BlockSpec((D, P), lambda b, t: (0, 0), pipeline_mode=pl.Buffered(1)),
            small(conv_w), small(vecs), small(wg_bf), small(g01),
        ],
        out_specs=[pl.BlockSpec((1, tm, W), lambda b, t: (b, t, 0)),
                   pl.BlockSpec((1, tm, widths[1]), lambda b, t: (b, t, 0)),
                   pl.BlockSpec((1, tm, widths[2]), lambda b, t: (b, t, 0))],
        out_shape=[jax.ShapeDtypeStruct((B, T, W), BF16),
                   jax.ShapeDtypeStruct((B, T, widths[1]), F32),
                   jax.ShapeDtypeStruct((B, T, widths[2]), F32)],
        scratch_shapes=[pltpu.VMEM((SUBLANES, W), F32), pltpu.VMEM((SUBLANES, W), F32)],
        compiler_params=pltpu.CompilerParams(
            dimension_semantics=("parallel", "arbitrary"), vmem_limit_bytes=VMEM_LIMIT),
        name="in_projection",
    )(x, mod_l, gain, w_bf, conv_w, vecs, wg_bf, g01)


def _hgrn_kernel(layer, pb_ref, lb_ref, beta_ref, wst_ref, g_ref, o_ref,
                 s_ref, qin_ref, oi_ref, kv_ref, dec_ref):
    C = CHUNK
    R = wst_ref.shape[1]
    G = R // C
    TB = pb_ref.shape[1]
    W = o_ref.shape[2]
    n_pairs = W // LANES

    @pl.when(pl.program_id(1) == 0)
    def _():
        s_ref[...] = jnp.zeros_like(s_ref)

    lbr = lb_ref[...]
    n_layers = lbr.shape[0]
    mx = lbr[0:1, :]
    for i in range(1, n_layers):
        mx = jnp.maximum(mx, lbr[i:i + 1, :])
    es = [jnp.exp(lbr[i:i + 1, :] - mx) for i in range(n_layers)]
    tot = es[0]
    for i in range(1, n_layers):
        tot = tot + es[i]
    sm = [e / tot for e in es]
    cum = sm[0]
    for i in range(1, layer + 1):
        cum = cum + sm[i]
    lb = cum - sm[0]
    log_lb = jnp.log(lb)
    log_1mlb = jnp.log(1.0 - lb)

    beta = beta_ref[...]
    g01 = g_ref[...]
    wst = wst_ref[...]
    lane = lax.broadcasted_iota(jnp.int32, (C, LANES), 1)
    m0 = lane < HEAD
    tw = lax.broadcasted_iota(jnp.int32, (C, 2 * C), 0)
    sw = lax.broadcasted_iota(jnp.int32, (C, 2 * C), 1) & (C - 1)
    eye = tw == sw
    level = []
    for lev in range(N_LEVELS):
        th = tw >> (N_LEVELS - 1 - lev)
        sh = sw >> (N_LEVELS - 1 - lev)
        level.append((th == sh + 1) & ((sh & 1) == 0))
    r2 = lax.broadcasted_iota(jnp.int32, (LANES, LANES), 0)
    c2 = lax.broadcasted_iota(jnp.int32, (LANES, LANES), 1)
    same_head = (r2 >> HEAD_BITS) == (c2 >> HEAD_BITS)

    odd_row = (lax.broadcasted_iota(jnp.int32, (C, LANES), 0) & 1) == 1
    n_kinds = wst_ref.shape[0] // R

    def elem_steps(gi, out):
        t0 = gi * R
        q = pb_ref[0, t0:t0 + R, 0:W] * (HEAD ** -0.5)
        fl = pb_ref[0, t0:t0 + R, W:2 * W]
        v = pb_ref[0, t0:t0 + R, 2 * W:3 * W]
        lc = log_1mlb + (jnp.minimum(fl, 0.0) - _log1p_exp_neg_abs(fl))
        log_f = jnp.maximum(log_lb, lc) + _log1p_exp_neg_abs(log_lb - lc)
        expo = _mm(wst, log_f, NN, na=1, nb=2)
        yield
        f = jnp.exp(log_f)
        kg = _neg_expm1(log_f, f)
        ex = jnp.exp(expo)
        qin_ref[t0:t0 + R, :] = q * ex[0:R]
        for g in range(G):
            dec_ref[gi * G + g] = jnp.broadcast_to(ex[(g + 1) * C - 1:(g + 1) * C, :], (8, W))
        out.append((q, kg, v, f, ex))

    def mm_stages(gi, vals):
        q, kg, v, f, ex = vals
        items = [(g, p) for g in range(G) for p in range(n_pairs)]
        n = range(len(items))

        def piece(x, g, p, base=0):
            return x[base + g * C:base + (g + 1) * C, p * LANES:(p + 1) * LANES]

        q_p = [piece(q, g, p) for g, p in items]
        v_p = [piece(v, g, p) for g, p in items]
        k_s = [_stack(piece(kg, g, p), m0) for g, p in items]
        q_b = [x.astype(BF16) for x in q_p]
        k_b = [x.astype(BF16) for x in k_s]
        amat = [jnp.where(eye, _mm(q_b[j], k_b[j], NT), 0.0) for j in n]
        yield
        for lev in range(N_LEVELS):
            if 2 + lev < n_kinds:
                es = [piece(ex, g, p, (2 + lev) * R) for g, p in items]
            else:
                es = [jnp.where(odd_row, piece(f, g, p), 1.0) for g, p in items]
            es = [e.astype(BF16) for e in es]
            amat = [jnp.where(level[lev],
                              _mm(q_b[j] * es[j],
                                  k_b[j] * jnp.concatenate([es[j], es[j]], axis=0), NT),
                              amat[j]) for j in n]
            yield
        oi = [_mm(amat[j], _stack(v_p[j], m0), NN, *P_HAPP) for j in n]
        yield
        k_dec = kg * ex[R:2 * R]
        kv = [_mm(v_p[j].T, piece(k_dec, g, p), NN, *P_HAPP) for j, (g, p) in enumerate(items)]
        for g in range(G):
            t0 = (gi * G + g) * C
            oi_ref[t0:t0 + C, :] = jnp.concatenate(
                [oi[g * n_pairs + p] for p in range(n_pairs)], axis=1)
        for j, (g, p) in enumerate(items):
            kv_ref[(gi * G + g) * n_pairs + p] = jnp.where(same_head, kv[j], 0.0)
        yield

    def scan_chunk(ci):
        t0 = ci * C
        dec = dec_ref[ci]
        outs = []
        for p in range(n_pairs):
            sl = slice(p * LANES, (p + 1) * LANES)
            st = s_ref[p]
            outs.append(_mm(qin_ref[t0:t0 + C, sl], st, NT, *P_HAPP))
            s_ref[p] = st * dec[0:1, sl] + kv_ref[ci * n_pairs + p]
        oi_ref[t0:t0 + C, :] = oi_ref[t0:t0 + C, :] + jnp.concatenate(outs, axis=1)

    n_groups = TB // R
    vals = []
    for _ in elem_steps(0, vals):
        pass
    ready = []
    for gi in range(n_groups):
        nxt = elem_steps(gi + 1, vals) if gi + 1 < n_groups else iter(())
        for si, _ in enumerate(mm_stages(gi, vals[gi])):
            next(nxt, None)
            if ready and si % SCAN_EVERY == SCAN_EVERY - 1:
                scan_chunk(ready.pop(0))
        for _ in nxt:
            pass
        ready += [gi * G + g for g in range(G)]
    for ci in ready:
        scan_chunk(ci)

    o = oi_ref[...]
    g = pb_ref[0, :, 3 * W:4 * W]
    ms = _segsum(o * o, g01) * (1.0 / HEAD)
    o = o * lax.rsqrt(ms + NORM_EPS) * (g * _sigmoid(g)) * beta
    o_ref[0] = o.astype(o_ref.dtype)


def _hgrn_weights(group):
    C = CHUNK
    R = group * C
    w = np.zeros(((2 + N_LEVELS - 1) * R, R), np.float32)
    for g in range(group):
        o = g * C
        for r in range(C):
            w[o + r, o:o + r + 1] = 1.0
            w[R + o + r, o + r + 1:o + C] = 1.0
            for lev in range(N_LEVELS - 1):
                n = C >> lev
                m = (r // n) * n + n // 2 - 1
                if r % n >= n // 2:
                    w[(2 + lev) * R + o + r, o + m + 1:o + r + 1] = 1.0
                else:
                    w[(2 + lev) * R + o + r, o + r + 1:o + m + 1] = 1.0
    return w


def _hgrn(pb, hgrn_lb, beta_b, wst, g01, layer, tb):
    B, T, W4 = pb.shape
    W = W4 // 4
    return pl.pallas_call(
        functools.partial(_hgrn_kernel, layer),
        grid=(B, T // tb),
        in_specs=[
            pl.BlockSpec((1, tb, W4), lambda b, t: (b, t, 0)),
            pl.BlockSpec(hgrn_lb.shape, lambda b, t: (0, 0)),
            pl.BlockSpec(beta_b.shape, lambda b, t: (0, 0)),
            pl.BlockSpec(wst.shape, lambda b, t: (0, 0)),
            pl.BlockSpec(g01.shape, lambda b, t: (0, 0)),
        ],
        out_specs=pl.BlockSpec((1, tb, W), lambda b, t: (b, t, 0)),
        out_shape=jax.ShapeDtypeStruct((B, T, W), BF16),
        scratch_shapes=[pltpu.VMEM((W // LANES, LANES, LANES), F32),
                        pltpu.VMEM((tb, W), F32), pltpu.VMEM((tb, W), F32),
                        pltpu.VMEM(((tb // CHUNK) * (W // LANES), LANES, LANES), F32),
                        pltpu.VMEM((tb // CHUNK, 8, W), F32)],
        compiler_params=pltpu.CompilerParams(
            dimension_semantics=("parallel", "arbitrary"), vmem_limit_bytes=VMEM_LIMIT),
        name="hgrn2",
    )(pb, hgrn_lb, beta_b, wst, g01)


def _rwkv_kernel(pc_ref, mu_ref, vec_ref, lora_ref, gup_ref, tri_ref, g_ref, o_ref,
                 s_ref, prev_ref, rp_ref, y0_ref, mx_ref, nx_ref, bonus_ref, gate_ref, y_ref):
    C = CHUNK
    R = tri_ref.shape[0]
    G = R // C
    TB = pc_ref.shape[1]
    W = o_ref.shape[2]
    n_pairs = W // LANES

    @pl.when(pl.program_id(1) == 0)
    def _():
        s_ref[...] = jnp.zeros_like(s_ref)
        prev_ref[...] = jnp.zeros_like(prev_ref)

    mu = mu_ref[...]
    w0 = vec_ref[0:1, :]
    a0 = vec_ref[1:2, :]
    k_k = vec_ref[2:3, :]
    k_a = vec_ref[3:4, :]
    r_k = vec_ref[4:5, :]
    lnx_w = vec_ref[5:6, :]
    lnx_b = vec_ref[6:7, :]
    beta = vec_ref[7:8, :]
    lora = _split(lora_ref[...], 1)
    gup = _split(gup_ref[...], 1)
    tri = tri_ref[...]
    g01 = g_ref[...]
    lane = lax.broadcasted_iota(jnp.int32, (C, LANES), 1)
    lane_r = lax.broadcasted_iota(jnp.int32, (R, LANES), 1)
    m0 = lane < HEAD
    tw = lax.broadcasted_iota(jnp.int32, (C, 2 * C), 0)
    sw = lax.broadcasted_iota(jnp.int32, (C, 2 * C), 1) & (C - 1)
    strict = tw > sw
    incl = tw >= sw
    eye_w = jnp.where(tw == sw, 1.0, 0.0)
    r2 = lax.broadcasted_iota(jnp.int32, (2 * C, 2 * C), 0)
    c2 = lax.broadcasted_iota(jnp.int32, (2 * C, 2 * C), 1)
    same_head = (r2 >> HEAD_BITS) == (c2 >> HEAD_BITS)
    eye_f = jnp.where(r2 == c2, 1.0, 0.0)
    o_w = 3 * W
    o_g = o_w + LORA_W + LORA_A

    def mm_parts(x, parts):
        return jnp.dot(x.astype(BF16), parts[0], preferred_element_type=F32)

    def elem_steps(gi, out):
        t0 = gi * R
        pc = pc_ref[0, t0:t0 + R, :]
        prev8 = prev_ref[...] if gi == 0 else pc_ref[0, t0 - 8:t0, :]
        prev = pltpu.roll(jnp.concatenate([prev8, pc], axis=0), 1, 0)[8:]
        ps = pc + (prev - pc) * mu
        r = ps[:, 0:W]
        k = ps[:, W:2 * W]
        v = ps[:, 2 * W:3 * W]
        z = ps[:, o_w:o_g]
        z = jnp.where(lane_r < LORA_W, jnp.tanh(z), z)
        lo = mm_parts(z, lora)
        yield
        gate = mm_parts(_sigmoid(ps[:, o_g:o_g + LORA_G]), gup)
        gate_ref[t0:t0 + R, :] = gate * beta
        yield
        w_raw = -_softplus(-(w0 + lo[:, 0:W])) - 0.5
        lw = -jnp.exp(w_raw)
        gam = _mm(tri, lw, NN, na=1, nb=2)
        yield
        a_sig = _sigmoid(a0 + lo[:, W:2 * W])
        kk = k * k_k
        kk = kk * lax.rsqrt(_segsum(kk * kk, g01) + 1e-12)
        yield
        k2 = k * (1.0 + (a_sig - 1.0) * k_a)
        bonus_ref[t0:t0 + R, :] = _segsum(r * k2 * r_k, g01) * v
        yield
        e_in = jnp.exp(gam)
        e_ex = jnp.exp(gam - lw)
        e_neg = jnp.exp(-gam)
        out.append((-kk * e_ex, r * e_in, kk * a_sig * e_neg, k2 * e_neg, v, e_in))

    def mm_stages(gi, vals):
        at, rt, bt, kt, v, e_in = vals
        items = [(g, p) for g in range(G) for p in range(n_pairs)]
        n = range(len(items))

        def pieces(x):
            return [x[g * C:(g + 1) * C, p * LANES:(p + 1) * LANES] for g, p in items]

        def st(x):
            return _stack(x, m0)

        a_p, r_p, b_p, k_p, v_p = pieces(at), pieces(rt), pieces(bt), pieces(kt), pieces(v)
        v_s = [st(v_p[j]) for j in n]
        att = [_mm(jnp.concatenate([a_p[j], r_p[j]], axis=0),
                   jnp.concatenate([st(b_p[j]), st(k_p[j])], axis=0), NT, *P_ATT)
               for j in n]
        yield
        lmat = [jnp.where(strict, att[j][0:C, 0:2 * C], 0.0) for j in n]
        ak = [jnp.where(strict, att[j][0:C, 2 * C:4 * C], 0.0) for j in n]
        rb = [jnp.where(incl, att[j][C:2 * C, 0:2 * C], 0.0) for j in n]
        rk = [jnp.where(incl, att[j][C:2 * C, 2 * C:4 * C], 0.0) for j in n]
        akv = [_mm(ak[j], v_s[j], NN, *P_APP) for j in n]
        yield
        pw = [_mm(lmat[j], st(lmat[j]), NN, *P_INV) for j in n]
        xs = [eye_w + lmat[j] for j in n]
        yield
        for _ in range(N_LEVELS - 2):
            z = [_mm(pw[j], jnp.concatenate([st(xs[j]), st(pw[j])], axis=1), NN, *P_INV)
                 for j in n]
            xs = [xs[j] + z[j][:, 0:2 * C] for j in n]
            pw = [z[j][:, 2 * C:4 * C] for j in n]
            yield
        xs = [xs[j] + _mm(pw[j], st(xs[j]), NN, *P_INV) for j in n]
        yield
        au = [_mm(xs[j], jnp.concatenate([st(a_p[j]), st(akv[j])], axis=1), NN, *P_APP)
              for j in n]
        ap = [au[j][:, 0:LANES] for j in n]
        u0 = [au[j][:, LANES:2 * LANES] for j in n]
        yield
        ry = [_mm(rb[j], jnp.concatenate([st(ap[j]), st(u0[j])], axis=1), NN, *P_APP)
              for j in n]
        rkv = [_mm(rk[j], v_s[j], NN, *P_APP) for j in n]
        yield
        mx = [_mm(ap[j].T, b_p[j], NN, *P_APP) for j in n]
        nx = [_mm(jnp.concatenate([u0[j], v_p[j]], axis=0).T,
                  jnp.concatenate([b_p[j], k_p[j]], axis=0), NN, *P_APP) for j in n]
        for j, (g, p) in enumerate(items):
            idx = (gi * G + g) * n_pairs + p
            glast = e_in[(g + 1) * C - 1:(g + 1) * C, p * LANES:(p + 1) * LANES]
            rp_ref[idx] = r_p[j] + ry[j][:, 0:LANES]
            y0_ref[idx] = ry[j][:, LANES:2 * LANES] + rkv[j]
            mx_ref[idx] = (eye_f + jnp.where(same_head, mx[j], 0.0)) * glast
            nx_ref[idx] = jnp.where(same_head, nx[j], 0.0) * glast
        yield

    def scan_chunk(ci):
        pairs = range(n_pairs)
        sts = [s_ref[p] for p in pairs]
        ys = [_mm(rp_ref[ci * n_pairs + p], sts[p], NT, *P_STATE) for p in pairs]
        sn = [_mm(sts[p], mx_ref[ci * n_pairs + p], NN, *P_STATE) for p in pairs]
        for p in pairs:
            s_ref[p] = sn[p] + nx_ref[ci * n_pairs + p]
        y_ref[ci * C:(ci + 1) * C, :] = jnp.concatenate(
            [ys[p] + y0_ref[ci * n_pairs + p] for p in pairs], axis=1)

    n_groups = TB // R
    vals = []
    for _ in elem_steps(0, vals):
        pass
    ready = []
    for gi in range(n_groups):
        nxt = elem_steps(gi + 1, vals) if gi + 1 < n_groups else iter(())
        for si, _ in enumerate(mm_stages(gi, vals[gi])):
            next(nxt, None)
            if ready and si % SCAN_EVERY == SCAN_EVERY - 1:
                scan_chunk(ready.pop(0))
        for _ in nxt:
            pass
        ready += [gi * G + g for g in range(G)]
    for ci in ready:
        scan_chunk(ci)
    prev_ref[...] = pc_ref[0, TB - 8:TB, :]

    y = y_ref[...]
    mean = _segsum(y, g01) * (1.0 / HEAD)
    d = y - mean
    var = _segsum(d * d, g01) * (1.0 / HEAD)
    yn = d * lax.rsqrt(var + GN_EPS) * lnx_w + lnx_b
    o_ref[0] = ((yn + bonus_ref[...]) * gate_ref[...]).astype(o_ref.dtype)


def _rwkv(pc, mu, vecs, lora_w, g_up, tri, g01, tb):
    B, T, PC = pc.shape
    W = vecs.shape[1]
    n_mats = (tb // CHUNK) * (W // LANES)
    mats = pltpu.VMEM((n_mats, LANES, LANES), F32)
    rows = pltpu.VMEM((n_mats, CHUNK, LANES), F32)
    return pl.pallas_call(
        _rwkv_kernel,
        grid=(B, T // tb),
        in_specs=[
            pl.BlockSpec((1, tb, PC), lambda b, t: (b, t, 0)),
            pl.BlockSpec(mu.shape, lambda b, t: (0, 0)),
            pl.BlockSpec(vecs.shape, lambda b, t: (0, 0)),
            pl.BlockSpec(lora_w.shape, lambda b, t: (0, 0)),
            pl.BlockSpec(g_up.shape, lambda b, t: (0, 0)),
            pl.BlockSpec(tri.shape, lambda b, t: (0, 0)),
            pl.BlockSpec(g01.shape, lambda b, t: (0, 0)),
        ],
        out_specs=pl.BlockSpec((1, tb, W), lambda b, t: (b, t, 0)),
        out_shape=jax.ShapeDtypeStruct((B, T, W), BF16),
        scratch_shapes=[pltpu.VMEM((W // LANES, LANES, LANES), F32),
                        pltpu.VMEM((8, PC), F32),
                        rows, rows, mats, mats,
                        pltpu.VMEM((tb, W), F32), pltpu.VMEM((tb, W), F32),
                        pltpu.VMEM((tb, W), F32)],
        compiler_params=pltpu.CompilerParams(
            dimension_semantics=("parallel", "arbitrary"), vmem_limit_bytes=VMEM_LIMIT),
        name="rwkv7",
    )(pc, mu, vecs, lora_w, g_up, tri, g01)


def _ffn_kernel(final, widths, x_ref, ya_ref, yb_ref, yc_ref, mod_ref, g_ref, wo_ref,
                wg_ref, wu_ref, wd_ref, fg_ref, o_ref):
    x = x_ref[0]
    acc = None
    off = 0
    for ref, wd in zip((ya_ref, yb_ref, yc_ref), widths):
        d = jnp.dot(ref[0], wo_ref[off:off + wd, :], preferred_element_type=F32)
        acc = d if acc is None else acc + d
        off += wd
    x1 = x + mod_ref[0, 2:3, :] * acc
    ms = jnp.mean(x1 * x1, axis=-1, keepdims=True)
    h = x1 * lax.rsqrt(ms + NORM_EPS) * g_ref[...]
    h = h * (1.0 + mod_ref[0, 4:5, :]) + mod_ref[0, 3:4, :]
    hb = h.astype(BF16)
    gt = jnp.dot(hb, wg_ref[...], preferred_element_type=F32)
    up = jnp.dot(hb, wu_ref[...], preferred_element_type=F32)
    act = (gt * _sigmoid(gt) * up).astype(BF16)
    dn = jnp.dot(act, wd_ref[...], preferred_element_type=F32)
    x2 = x1 + mod_ref[0, 5:6, :] * dn
    if final:
        ms2 = jnp.mean(x2 * x2, axis=-1, keepdims=True)
        x2 = x2 * lax.rsqrt(ms2 + NORM_EPS) * fg_ref[...]
    o_ref[0] = x2


def _out_ffn(x, ya, yb, yc, mod_l, gain, wo, wg, wu, wd, final_g, final, tm):
    B, T, D = x.shape
    widths = (ya.shape[2], yb.shape[2], yc.shape[2])
    const = lambda shape: pl.BlockSpec(shape, lambda b, t: (0, 0), pipeline_mode=pl.Buffered(1))
    return pl.pallas_call(
        functools.partial(_ffn_kernel, final, widths),
        grid=(B, T // tm),
        in_specs=[
            pl.BlockSpec((1, tm, D), lambda b, t: (b, t, 0)),
            pl.BlockSpec((1, tm, widths[0]), lambda b, t: (b, t, 0)),
            pl.BlockSpec((1, tm, widths[1]), lambda b, t: (b, t, 0)),
            pl.BlockSpec((1, tm, widths[2]), lambda b, t: (b, t, 0)),
            pl.BlockSpec((1, 6, D), lambda b, t: (b, 0, 0)),
            pl.BlockSpec((1, D), lambda b, t: (0, 0)),
            const(wo.shape), const(wg.shape), const(wu.shape), const(wd.shape),
            pl.BlockSpec((1, D), lambda b, t: (0, 0)),
        ],
        out_specs=pl.BlockSpec((1, tm, D), lambda b, t: (b, t, 0)),
        out_shape=jax.ShapeDtypeStruct((B, T, D), F32),
        compiler_params=pltpu.CompilerParams(
            dimension_semantics=("parallel", "parallel"), vmem_limit_bytes=VMEM_LIMIT),
        name="out_ffn",
    )(x, ya, yb, yc, mod_l, gain, wo, wg, wu, wd, final_g)


def _block_diag(w):
    G, n, _ = w.shape
    eye = jnp.eye(G, dtype=w.dtype)
    return (eye[:, None, :, None] * w[:, :, None, :]).reshape(G * n, G * n)


def _seg_ones():
    idx = np.arange(LANES) // HEAD
    return jnp.asarray((idx[:, None] == idx[None, :]).astype(np.float32), dtype=BF16)


def kernel(x, c, norm1_g, norm2_g, ada_w, ada_b, w_in, conv_w, conv_b, rg_w, rg_b, ig_w, ig_b, lru_lam, hgrn_lb, rwkv_mu, rwkv_w0, rwkv_w_up, rwkv_a0, rwkv_a_up, rwkv_g_up, rwkv_k_k, rwkv_k_a, rwkv_r_k, rwkv_lnx_w, rwkv_lnx_b, mix_beta, w_out, ffn_w_gate, ffn_w_up, ffn_w_down, final_g):
    B, T, D = x.shape
    L = w_in.shape[0]
    WA = conv_w.shape[2]
    WB = hgrn_lb.shape[1]
    WC = rwkv_w0.shape[1]
    widths_p = (2 * WA, 4 * WB, rwkv_mu.shape[1])

    mod = _modulation(c, ada_w, ada_b).reshape(L, B, 6, D)
    wst = jnp.asarray(_hgrn_weights(HGRN_GROUP), dtype=BF16)
    tri = jnp.asarray(np.kron(np.eye(RWKV_GROUP, dtype=np.float32),
                              np.tril(np.ones((CHUNK, CHUNK), np.float32))), dtype=BF16)
    g_a = g_b = g_c = _seg_ones()

    for l in range(L):
        beta = mix_beta[l]
        vec_a = jnp.stack([conv_b[l], rg_b[l], ig_b[l], lru_lam[l], beta[:WA],
                           jnp.zeros_like(beta[:WA]), jnp.zeros_like(beta[:WA]),
                           jnp.zeros_like(beta[:WA])])
        wg_a = jnp.concatenate([_block_diag(rg_w[l]), _block_diag(ig_w[l])], axis=1).astype(BF16)
        ya, pb, pc = _in_projection(x, mod[l], norm1_g[l][None], w_in[l].astype(BF16),
                                    conv_w[l], vec_a, wg_a, g_a, widths_p, IN_ROWS)
        yb = _hgrn(pb, hgrn_lb, beta[None, WA:WA + WB], wst, g_b, l, HGRN_ROWS)
        vec_c = jnp.stack([rwkv_w0[l], rwkv_a0[l], rwkv_k_k[l], rwkv_k_a[l],
                           rwkv_r_k[l].reshape(-1), rwkv_lnx_w[l], rwkv_lnx_b[l],
                           beta[WA + WB:]])
        zw = jnp.zeros_like(rwkv_w_up[l])
        lora_w = jnp.concatenate(
            [jnp.concatenate([rwkv_w_up[l], zw], axis=1),
             jnp.concatenate([zw, rwkv_a_up[l]], axis=1)], axis=0)
        yc = _rwkv(pc, rwkv_mu[l][None], vec_c, lora_w, rwkv_g_up[l], tri, g_c, RWKV_ROWS)
        x = _out_ffn(x, ya, yb, yc, mod[l], norm2_g[l][None], w_out[l].astype(BF16),
                     ffn_w_gate[l].astype(BF16), ffn_w_up[l].astype(BF16),
                     ffn_w_down[l].astype(BF16), final_g[None], l == L - 1, FFN_ROWS)
    return x
```

```python
import functools

import numpy as np
import jax
import jax.numpy as jnp
from jax import lax
from jax.experimental import pallas as pl
from jax.experimental.pallas import tpu as pltpu

F32 = jnp.float32
BF16 = jnp.bfloat16

HEAD = 64
HEAD_BITS = 6
LANES = 128
SUBLANES = 8
W_CHUNK = 128
CONV_W = 4
LRU_C = 8.0
NORM_EPS = 1e-6
GN_EPS = 64e-5
CHUNK = 64
N_LEVELS = 6
RWKV_GROUP = 4
SCAN_EVERY = 2
IN_ROWS = 512
FFN_ROWS = 512
HGRN_ROWS = 1024
RWKV_ROWS = 1024
HGRN_GROUP = 2
LORA_W = 64
LORA_A = 64
LORA_G = 128
VMEM_LIMIT = 56 * 1024 * 1024

NN = (((1,), (0,)), ((), ()))
NT = (((1,), (1,)), ((), ()))

P_ATT = (1, 1)
P_INV = (1, 1)
P_APP = (1, 1)
P_STATE = (1, 1)
P_HATT = (1, 1)
P_HAPP = (1, 1)


def _split(x, n):
    if x.dtype == BF16:
        return [x]
    parts = []
    r = x
    for i in range(n):
        p = r.astype(BF16)
        parts.append(p)
        if i + 1 < n:
            r = r - p.astype(F32)
    return parts


def _mm(a, b, dims=NN, na=2, nb=2):
    pa = _split(a, na)
    pb = _split(b, nb)
    order = max(len(pa), len(pb))
    acc = None
    for i, x in enumerate(pa):
        for j, y in enumerate(pb):
            if i + j < order:
                d = lax.dot_general(x, y, dims, preferred_element_type=F32)
                acc = d if acc is None else acc + d
    return acc


def _segsum(x, g01):
    xb = x.astype(BF16)
    return jnp.concatenate(
        [jnp.dot(xb[:, o:o + LANES], g01, preferred_element_type=F32)
         for o in range(0, x.shape[1], LANES)], axis=1)


def _stack(x, m0):
    return jnp.concatenate([jnp.where(m0, x, 0.0), jnp.where(m0, 0.0, x)], axis=0)


def _sigmoid(x):
    return 0.5 * jnp.tanh(0.5 * x) + 0.5


def _log1p_exp_neg_abs(x):
    return jnp.log(1.0 + jnp.exp(-jnp.abs(x)))


def _neg_expm1(x, ex):
    return -jnp.tanh(0.5 * x) * (ex + 1.0)


def _softplus(x):
    return jnp.maximum(x, 0.0) + _log1p_exp_neg_abs(x)


def _load_weight_bf16(w_hbm, layer, dst_ref, stage_ref, sem_ref):
    K, N = dst_ref.shape

    def copy(c):
        return pltpu.make_async_copy(
            w_hbm.at[layer, pl.ds(c * W_CHUNK, W_CHUNK), :],
            stage_ref.at[c % 2, :, pl.ds(0, N)],
            sem_ref.at[c % 2])

    n = K // W_CHUNK
    copy(0).start()
    for c in range(n):
        if c + 1 < n:
            copy(c + 1).start()
        copy(c).wait()
        dst_ref[c * W_CHUNK:(c + 1) * W_CHUNK, :] = stage_ref[c % 2, :, 0:N].astype(BF16)


def _is_first_step():
    return (pl.program_id(0) == 0) & (pl.program_id(1) == 0)


def _mod_kernel(c_ref, w_ref, b_ref, o_ref):
    c = c_ref[...]
    cs = c * _sigmoid(c)
    o_ref[0] = _mm(cs, w_ref[0]) + b_ref[0]


def _modulation(c, ada_w, ada_b):
    L, D, N = ada_w.shape
    B = c.shape[0]
    tn = N // 4
    return pl.pallas_call(
        _mod_kernel,
        grid=(L, N // tn),
        in_specs=[
            pl.BlockSpec((B, D), lambda l, j: (0, 0)),
            pl.BlockSpec((1, D, tn), lambda l, j: (l, 0, j)),
            pl.BlockSpec((1, 1, tn), lambda l, j: (l, 0, j)),
        ],
        out_specs=pl.BlockSpec((1, B, tn), lambda l, j: (l, 0, j)),
        out_shape=jax.ShapeDtypeStruct((L, B, N), F32),
        compiler_params=pltpu.CompilerParams(
            dimension_semantics=("parallel", "parallel"), vmem_limit_bytes=VMEM_LIMIT),
        name="modulation",
    )(c, ada_w, ada_b.reshape(L, 1, N))


def _inproj_kernel(widths, layer, x_ref, mod_ref, g_ref, w_hbm, cw_ref, vec_ref, wg_ref,
                   seg_ref, ya_ref, pb_ref, pc_ref, xc_ref, hc_ref, w_ref, stage_ref, sem_ref):
    wa, wb, wc = widths
    W = wa // 2
    tm = x_ref.shape[1]
    @pl.when(_is_first_step())
    def _():
        _load_weight_bf16(w_hbm, layer, w_ref, stage_ref, sem_ref)

    @pl.when(pl.program_id(1) == 0)
    def _():
        xc_ref[...] = jnp.zeros_like(xc_ref)
        hc_ref[...] = jnp.zeros_like(hc_ref)

    x = x_ref[0]
    ms = jnp.mean(x * x, axis=-1, keepdims=True)
    h = x * lax.rsqrt(ms + NORM_EPS) * g_ref[...]
    h = h * (1.0 + mod_ref[0, 1:2, :]) + mod_ref[0, 0:1, :]
    hb = h.astype(BF16)

    conv_b = vec_ref[0:1, :]
    rg_b = vec_ref[1:2, :]
    ig_b = vec_ref[2:3, :]
    sp_lam = _softplus(-vec_ref[3:4, :])
    beta = vec_ref[4:5, :]
    row = lax.broadcasted_iota(jnp.int32, (tm, W), 0)

    pa = jnp.dot(hb, w_ref[:, 0:wa], preferred_element_type=F32)
    xa = pa[:, 0:W]
    ya = pa[:, W:wa]
    win = jnp.concatenate([xc_ref[...], xa], axis=0)
    xc_ref[...] = xa[tm - SUBLANES:tm, :]
    u = conv_b + cw_ref[CONV_W - 1:CONV_W, :] * xa
    for j in range(1, CONV_W):
        u = u + cw_ref[CONV_W - 1 - j:CONV_W - j, :] * pltpu.roll(win, j, 0)[SUBLANES:]
    pb_ref[0] = jnp.dot(hb, w_ref[:, wa:wa + wb], preferred_element_type=F32)
    z = jnp.dot(u.astype(BF16), wg_ref[...], preferred_element_type=F32)
    pc_ref[0] = jnp.dot(hb, w_ref[:, wa + wb:wa + wb + wc], preferred_element_type=F32)

    r = _sigmoid(z[:, 0:W] + rg_b)
    ig = _sigmoid(z[:, W:2 * W] + ig_b)
    log_a = (-LRU_C) * r * sp_lam
    a = jnp.exp(log_a)
    mult = jnp.sqrt(_neg_expm1(2.0 * log_a, a * a))
    mult = jnp.where(row + pl.program_id(1) * tm == 0, 1.0, mult)
    bv = mult * (ig * u)
    s = 1
    while s < SUBLANES:
        keep = (row & (SUBLANES - 1)) >= s
        a_s = jnp.where(keep, pltpu.roll(a, s, 0), 1.0)
        b_s = jnp.where(keep, pltpu.roll(bv, s, 0), 0.0)
        bv = a * b_s + bv
        a = a * a_s
        s *= 2
    hprev = hc_ref[0:1, :]
    hs = []
    for j in range(tm // SUBLANES):
        blk = a[j * SUBLANES:(j + 1) * SUBLANES] * hprev + bv[j * SUBLANES:(j + 1) * SUBLANES]
        hs.append(blk)
        hprev = blk[SUBLANES - 1:SUBLANES, :]
    hc_ref[...] = jnp.broadcast_to(hprev, hc_ref.shape)
    y = jnp.concatenate(hs, axis=0) * jax.nn.gelu(ya)
    ms_a = _segsum(y * y, seg_ref[...]) * (1.0 / HEAD)
    ya_ref[0] = (y * lax.rsqrt(ms_a + NORM_EPS) * beta).astype(ya_ref.dtype)


def _in_projection(x, mod_l, gain, w_in, conv_w, vecs, wg_bf, g01, widths, layer, tm):
    B, T, D = x.shape
    P = w_in.shape[2]
    W = widths[0] // 2
    small = lambda a: pl.BlockSpec(a.shape, lambda b, t: (0, 0))
    return pl.pallas_call(
        functools.partial(_inproj_kernel, widths, layer),
        grid=(B, T // tm),
        in_specs=[
            pl.BlockSpec((1, tm, D), lambda b, t: (b, t, 0)),
            pl.BlockSpec((1, 6, D), lambda b, t: (b, 0, 0)),
            pl.BlockSpec((1, D), lambda b, t: (0, 0)),
            pl.BlockSpec(memory_space=pl.ANY),
            small(conv_w), small(vecs), small(wg_bf), small(g01),
        ],
        out_specs=[pl.BlockSpec((1, tm, W), lambda b, t: (b, t, 0)),
                   pl.BlockSpec((1, tm, widths[1]), lambda b, t: (b, t, 0)),
                   pl.BlockSpec((1, tm, widths[2]), lambda b, t: (b, t, 0))],
        out_shape=[jax.ShapeDtypeStruct((B, T, W), BF16),
                   jax.ShapeDtypeStruct((B, T, widths[1]), F32),
                   jax.ShapeDtypeStruct((B, T, widths[2]), F32)],
        scratch_shapes=[pltpu.VMEM((SUBLANES, W), F32), pltpu.VMEM((SUBLANES, W), F32),
                        pltpu.VMEM((D, P), BF16), pltpu.VMEM((2, W_CHUNK, P), F32),
                        pltpu.SemaphoreType.DMA((2,))],
        compiler_params=pltpu.CompilerParams(
            dimension_semantics=("arbitrary", "arbitrary"), vmem_limit_bytes=VMEM_LIMIT),
        name="in_projection",
    )(x, mod_l, gain, w_in, conv_w, vecs, wg_bf, g01)


def _hgrn_kernel(layer, pb_ref, lb_ref, beta_ref, wst_ref, g_ref, o_ref,
                 s_ref, qin_ref, oi_ref, kv_ref, dec_ref):
    C = CHUNK
    R = wst_ref.shape[1]
    G = R // C
    TB = pb_ref.shape[1]
    W = o_ref.shape[2]
    n_pairs = W // LANES

    @pl.when(pl.program_id(1) == 0)
    def _():
        s_ref[...] = jnp.zeros_like(s_ref)

    lbr = lb_ref[...]
    n_layers = lbr.shape[0]
    mx = lbr[0:1, :]
    for i in range(1, n_layers):
        mx = jnp.maximum(mx, lbr[i:i + 1, :])
    es = [jnp.exp(lbr[i:i + 1, :] - mx) for i in range(n_layers)]
    tot = es[0]
    for i in range(1, n_layers):
        tot = tot + es[i]
    sm = [e / tot for e in es]
    cum = sm[0]
    for i in range(1, layer + 1):
        cum = cum + sm[i]
    lb = cum - sm[0]
    log_lb = jnp.log(lb)
    log_1mlb = jnp.log(1.0 - lb)

    beta = beta_ref[...]
    g01 = g_ref[...]
    wst = wst_ref[...]
    lane = lax.broadcasted_iota(jnp.int32, (C, LANES), 1)
    m0 = lane < HEAD
    tw = lax.broadcasted_iota(jnp.int32, (C, 2 * C), 0)
    sw = lax.broadcasted_iota(jnp.int32, (C, 2 * C), 1) & (C - 1)
    eye = tw == sw
    level = []
    for lev in range(N_LEVELS):
        th = tw >> (N_LEVELS - 1 - lev)
        sh = sw >> (N_LEVELS - 1 - lev)
        level.append((th == sh + 1) & ((sh & 1) == 0))
    r2 = lax.broadcasted_iota(jnp.int32, (LANES, LANES), 0)
    c2 = lax.broadcasted_iota(jnp.int32, (LANES, LANES), 1)
    same_head = (r2 >> HEAD_BITS) == (c2 >> HEAD_BITS)

    odd_row = (lax.broadcasted_iota(jnp.int32, (C, LANES), 0) & 1) == 1
    n_kinds = wst_ref.shape[0] // R

    def elem_steps(gi, out):
        t0 = gi * R
        q = pb_ref[0, t0:t0 + R, 0:W] * (HEAD ** -0.5)
        fl = pb_ref[0, t0:t0 + R, W:2 * W]
        v = pb_ref[0, t0:t0 + R, 2 * W:3 * W]
        lc = log_1mlb + (jnp.minimum(fl, 0.0) - _log1p_exp_neg_abs(fl))
        log_f = jnp.maximum(log_lb, lc) + _log1p_exp_neg_abs(log_lb - lc)
        expo = _mm(wst, log_f, NN, na=1, nb=2)
        yield
        f = jnp.exp(log_f)
        kg = _neg_expm1(log_f, f)
        ex = jnp.exp(expo)
        qin_ref[t0:t0 + R, :] = q * ex[0:R]
        for g in range(G):
            dec_ref[gi * G + g] = jnp.broadcast_to(ex[(g + 1) * C - 1:(g + 1) * C, :], (8, W))
        out.append((q, kg, v, f, ex))

    def mm_stages(gi, vals):
        q, kg, v, f, ex = vals
        items = [(g, p) for g in range(G) for p in range(n_pairs)]
        n = range(len(items))

        def piece(x, g, p, base=0):
            return x[base + g * C:base + (g + 1) * C, p * LANES:(p + 1) * LANES]

        q_p = [piece(q, g, p) for g, p in items]
        v_p = [piece(v, g, p) for g, p in items]
        k_s = [_stack(piece(kg, g, p), m0) for g, p in items]
        q_b = [x.astype(BF16) for x in q_p]
        k_b = [x.astype(BF16) for x in k_s]
        amat = [jnp.where(eye, _mm(q_b[j], k_b[j], NT), 0.0) for j in n]
        yield
        for lev in range(N_LEVELS):
            if 2 + lev < n_kinds:
                es = [piece(ex, g, p, (2 + lev) * R) for g, p in items]
            else:
                es = [jnp.where(odd_row, piece(f, g, p), 1.0) for g, p in items]
            es = [e.astype(BF16) for e in es]
            amat = [jnp.where(level[lev],
                              _mm(q_b[j] * es[j],
                                  k_b[j] * jnp.concatenate([es[j], es[j]], axis=0), NT),
                              amat[j]) for j in n]
            yield
        oi = [_mm(amat[j], _stack(v_p[j], m0), NN, *P_HAPP) for j in n]
        yield
        k_dec = kg * ex[R:2 * R]
        kv = [_mm(v_p[j].T, piece(k_dec, g, p), NN, *P_HAPP) for j, (g, p) in enumerate(items)]
        for g in range(G):
            t0 = (gi * G + g) * C
            oi_ref[t0:t0 + C, :] = jnp.concatenate(
                [oi[g * n_pairs + p] for p in range(n_pairs)], axis=1)
        for j, (g, p) in enumerate(items):
            kv_ref[(gi * G + g) * n_pairs + p] = jnp.where(same_head, kv[j], 0.0)
        yield

    def scan_chunk(ci):
        t0 = ci * C
        dec = dec_ref[ci]
        outs = []
        for p in range(n_pairs):
            sl = slice(p * LANES, (p + 1) * LANES)
            st = s_ref[p]
            outs.append(_mm(qin_ref[t0:t0 + C, sl], st, NT, *P_HAPP))
            s_ref[p] = st * dec[0:1, sl] + kv_ref[ci * n_pairs + p]
        oi_ref[t0:t0 + C, :] = oi_ref[t0:t0 + C, :] + jnp.concatenate(outs, axis=1)

    n_groups = TB // R
    vals = []
    for _ in elem_steps(0, vals):
        pass
    ready = []
    for gi in range(n_groups):
        nxt = elem_steps(gi + 1, vals) if gi + 1 < n_groups else iter(())
        for si, _ in enumerate(mm_stages(gi, vals[gi])):
            next(nxt, None)
            if ready and si % SCAN_EVERY == SCAN_EVERY - 1:
                scan_chunk(ready.pop(0))
        for _ in nxt:
            pass
        ready += [gi * G + g for g in range(G)]
    for ci in ready:
        scan_chunk(ci)

    o = oi_ref[...]
    g = pb_ref[0, :, 3 * W:4 * W]
    ms = _segsum(o * o, g01) * (1.0 / HEAD)
    o = o * lax.rsqrt(ms + NORM_EPS) * (g * _sigmoid(g)) * beta
    o_ref[0] = o.astype(o_ref.dtype)


def _hgrn_weights(group):
    C = CHUNK
    R = group * C
    w = np.zeros(((2 + N_LEVELS - 1) * R, R), np.float32)
    for g in range(group):
        o = g * C
        for r in range(C):
            w[o + r, o:o + r + 1] = 1.0
            w[R + o + r, o + r + 1:o + C] = 1.0
            for lev in range(N_LEVELS - 1):
                n = C >> lev
                m = (r // n) * n + n // 2 - 1
                if r % n >= n // 2:
                    w[(2 + lev) * R + o + r, o + m + 1:o + r + 1] = 1.0
                else:
                    w[(2 + lev) * R + o + r, o + r + 1:o + m + 1] = 1.0
    return w


def _hgrn(pb, hgrn_lb, beta_b, wst, g01, layer, tb):
    B, T, W4 = pb.shape
    W = W4 // 4
    return pl.pallas_call(
        functools.partial(_hgrn_kernel, layer),
        grid=(B, T // tb),
        in_specs=[
            pl.BlockSpec((1, tb, W4), lambda b, t: (b, t, 0)),
            pl.BlockSpec(hgrn_lb.shape, lambda b, t: (0, 0)),
            pl.BlockSpec(beta_b.shape, lambda b, t: (0, 0)),
            pl.BlockSpec(wst.shape, lambda b, t: (0, 0)),
            pl.BlockSpec(g01.shape, lambda b, t: (0, 0)),
        ],
        out_specs=pl.BlockSpec((1, tb, W), lambda b, t: (b, t, 0)),
        out_shape=jax.ShapeDtypeStruct((B, T, W), BF16),
        scratch_shapes=[pltpu.VMEM((W // LANES, LANES, LANES), F32),
                        pltpu.VMEM((tb, W), F32), pltpu.VMEM((tb, W), F32),
                        pltpu.VMEM(((tb // CHUNK) * (W // LANES), LANES, LANES), F32),
                        pltpu.VMEM((tb // CHUNK, 8, W), F32)],
        compiler_params=pltpu.CompilerParams(
            dimension_semantics=("parallel", "arbitrary"), vmem_limit_bytes=VMEM_LIMIT),
        name="hgrn2",
    )(pb, hgrn_lb, beta_b, wst, g01)


def _rwkv_kernel(pc_ref, mu_ref, vec_ref, lora_ref, gup_ref, tri_ref, g_ref, o_ref,
                 s_ref, prev_ref, rp_ref, y0_ref, mx_ref, nx_ref, bonus_ref, gate_ref, y_ref):
    C = CHUNK
    R = tri_ref.shape[0]
    G = R // C
    TB = pc_ref.shape[1]
    W = o_ref.shape[2]
    n_pairs = W // LANES

    @pl.when(pl.program_id(1) == 0)
    def _():
        s_ref[...] = jnp.zeros_like(s_ref)
        prev_ref[...] = jnp.zeros_like(prev_ref)

    mu = mu_ref[...]
    w0 = vec_ref[0:1, :]
    a0 = vec_ref[1:2, :]
    k_k = vec_ref[2:3, :]
    k_a = vec_ref[3:4, :]
    r_k = vec_ref[4:5, :]
    lnx_w = vec_ref[5:6, :]
    lnx_b = vec_ref[6:7, :]
    beta = vec_ref[7:8, :]
    lora = _split(lora_ref[...], 1)
    gup = _split(gup_ref[...], 1)
    tri = tri_ref[...]
    g01 = g_ref[...]
    lane = lax.broadcasted_iota(jnp.int32, (C, LANES), 1)
    lane_r = lax.broadcasted_iota(jnp.int32, (R, LANES), 1)
    m0 = lane < HEAD
    tw = lax.broadcasted_iota(jnp.int32, (C, 2 * C), 0)
    sw = lax.broadcasted_iota(jnp.int32, (C, 2 * C), 1) & (C - 1)
    strict = tw > sw
    incl = tw >= sw
    eye_w = jnp.where(tw == sw, 1.0, 0.0)
    r2 = lax.broadcasted_iota(jnp.int32, (2 * C, 2 * C), 0)
    c2 = lax.broadcasted_iota(jnp.int32, (2 * C, 2 * C), 1)
    same_head = (r2 >> HEAD_BITS) == (c2 >> HEAD_BITS)
    eye_f = jnp.where(r2 == c2, 1.0, 0.0)
    o_w = 3 * W
    o_g = o_w + LORA_W + LORA_A

    def mm_parts(x, parts):
        return jnp.dot(x.astype(BF16), parts[0], preferred_element_type=F32)

    def elem_steps(gi, out):
        t0 = gi * R
        pc = pc_ref[0, t0:t0 + R, :]
        prev8 = prev_ref[...] if gi == 0 else pc_ref[0, t0 - 8:t0, :]
        prev = pltpu.roll(jnp.concatenate([prev8, pc], axis=0), 1, 0)[8:]
        ps = pc + (prev - pc) * mu
        r = ps[:, 0:W]
        k = ps[:, W:2 * W]
        v = ps[:, 2 * W:3 * W]
        z = ps[:, o_w:o_g]
        z = jnp.where(lane_r < LORA_W, jnp.tanh(z), z)
        lo = mm_parts(z, lora)
        yield
        gate = mm_parts(_sigmoid(ps[:, o_g:o_g + LORA_G]), gup)
        gate_ref[t0:t0 + R, :] = gate * beta
        yield
        w_raw = -_softplus(-(w0 + lo[:, 0:W])) - 0.5
        lw = -jnp.exp(w_raw)
        gam = _mm(tri, lw, NN, na=1, nb=2)
        yield
        a_sig = _sigmoid(a0 + lo[:, W:2 * W])
        kk = k * k_k
        kk = kk * lax.rsqrt(_segsum(kk * kk, g01) + 1e-12)
        yield
        k2 = k * (1.0 + (a_sig - 1.0) * k_a)
        bonus_ref[t0:t0 + R, :] = _segsum(r * k2 * r_k, g01) * v
        yield
        e_in = jnp.exp(gam)
        e_ex = jnp.exp(gam - lw)
        e_neg = jnp.exp(-gam)
        out.append((-kk * e_ex, r * e_in, kk * a_sig * e_neg, k2 * e_neg, v, e_in))

    def mm_stages(gi, vals):
        at, rt, bt, kt, v, e_in = vals
        items = [(g, p) for g in range(G) for p in range(n_pairs)]
        n = range(len(items))

        def pieces(x):
            return [x[g * C:(g + 1) * C, p * LANES:(p + 1) * LANES] for g, p in items]

        def st(x):
            return _stack(x, m0)

        a_p, r_p, b_p, k_p, v_p = pieces(at), pieces(rt), pieces(bt), pieces(kt), pieces(v)
        v_s = [st(v_p[j]) for j in n]
        att = [_mm(jnp.concatenate([a_p[j], r_p[j]], axis=0),
                   jnp.concatenate([st(b_p[j]), st(k_p[j])], axis=0), NT, *P_ATT)
               for j in n]
        yield
        lmat = [jnp.where(strict, att[j][0:C, 0:2 * C], 0.0) for j in n]
        ak = [jnp.where(strict, att[j][0:C, 2 * C:4 * C], 0.0) for j in n]
        rb = [jnp.where(incl, att[j][C:2 * C, 0:2 * C], 0.0) for j in n]
        rk = [jnp.where(incl, att[j][C:2 * C, 2 * C:4 * C], 0.0) for j in n]
        akv = [_mm(ak[j], v_s[j], NN, *P_APP) for j in n]
        yield
        pw = [_mm(lmat[j], st(lmat[j]), NN, *P_INV) for j in n]
        xs = [eye_w + lmat[j] for j in n]
        yield
        for _ in range(N_LEVELS - 2):
            z = [_mm(pw[j], jnp.concatenate([st(xs[j]), st(pw[j])], axis=1), NN, *P_INV)
                 for j in n]
            xs = [xs[j] + z[j][:, 0:2 * C] for j in n]
            pw = [z[j][:, 2 * C:4 * C] for j in n]
            yield
        xs = [xs[j] + _mm(pw[j], st(xs[j]), NN, *P_INV) for j in n]
        yield
        au = [_mm(xs[j], jnp.concatenate([st(a_p[j]), st(akv[j])], axis=1), NN, *P_APP)
              for j in n]
        ap = [au[j][:, 0:LANES] for j in n]
        u0 = [au[j][:, LANES:2 * LANES] for j in n]
        yield
        ry = [_mm(rb[j], jnp.concatenate([st(ap[j]), st(u0[j])], axis=1), NN, *P_APP)
              for j in n]
        rkv = [_mm(rk[j], v_s[j], NN, *P_APP) for j in n]
        yield
        mx = [_mm(ap[j].T, b_p[j], NN, *P_APP) for j in n]
        nx = [_mm(jnp.concatenate([u0[j], v_p[j]], axis=0).T,
                  jnp.concatenate([b_p[j], k_p[j]], axis=0), NN, *P_APP) for j in n]
        for j, (g, p) in enumerate(items):
            idx = (gi * G + g) * n_pairs + p
            glast = e_in[(g + 1) * C - 1:(g + 1) * C, p * LANES:(p + 1) * LANES]
            rp_ref[idx] = r_p[j] + ry[j][:, 0:LANES]
            y0_ref[idx] = ry[j][:, LANES:2 * LANES] + rkv[j]
            mx_ref[idx] = (eye_f + jnp.where(same_head, mx[j], 0.0)) * glast
            nx_ref[idx] = jnp.where(same_head, nx[j], 0.0) * glast
        yield

    def scan_chunk(ci):
        pairs = range(n_pairs)
        sts = [s_ref[p] for p in pairs]
        ys = [_mm(rp_ref[ci * n_pairs + p], sts[p], NT, *P_STATE) for p in pairs]
        sn = [_mm(sts[p], mx_ref[ci * n_pairs + p], NN, *P_STATE) for p in pairs]
        for p in pairs:
            s_ref[p] = sn[p] + nx_ref[ci * n_pairs + p]
        y_ref[ci * C:(ci + 1) * C, :] = jnp.concatenate(
            [ys[p] + y0_ref[ci * n_pairs + p] for p in pairs], axis=1)

    n_groups = TB // R
    vals = []
    for _ in elem_steps(0, vals):
        pass
    ready = []
    for gi in range(n_groups):
        nxt = elem_steps(gi + 1, vals) if gi + 1 < n_groups else iter(())
        for si, _ in enumerate(mm_stages(gi, vals[gi])):
            next(nxt, None)
            if ready and si % SCAN_EVERY == SCAN_EVERY - 1:
                scan_chunk(ready.pop(0))
        for _ in nxt:
            pass
        ready += [gi * G + g for g in range(G)]
    for ci in ready:
        scan_chunk(ci)
    prev_ref[...] = pc_ref[0, TB - 8:TB, :]

    y = y_ref[...]
    mean = _segsum(y, g01) * (1.0 / HEAD)
    d = y - mean
    var = _segsum(d * d, g01) * (1.0 / HEAD)
    yn = d * lax.rsqrt(var + GN_EPS) * lnx_w + lnx_b
    o_ref[0] = ((yn + bonus_ref[...]) * gate_ref[...]).astype(o_ref.dtype)


def _rwkv(pc, mu, vecs, lora_w, g_up, tri, g01, tb):
    B, T, PC = pc.shape
    W = vecs.shape[1]
    n_mats = (tb // CHUNK) * (W // LANES)
    mats = pltpu.VMEM((n_mats, LANES, LANES), F32)
    rows = pltpu.VMEM((n_mats, CHUNK, LANES), F32)
    return pl.pallas_call(
        _rwkv_kernel,
        grid=(B, T // tb),
        in_specs=[
            pl.BlockSpec((1, tb, PC), lambda b, t: (b, t, 0)),
            pl.BlockSpec(mu.shape, lambda b, t: (0, 0)),
            pl.BlockSpec(vecs.shape, lambda b, t: (0, 0)),
            pl.BlockSpec(lora_w.shape, lambda b, t: (0, 0)),
            pl.BlockSpec(g_up.shape, lambda b, t: (0, 0)),
            pl.BlockSpec(tri.shape, lambda b, t: (0, 0)),
            pl.BlockSpec(g01.shape, lambda b, t: (0, 0)),
        ],
        out_specs=pl.BlockSpec((1, tb, W), lambda b, t: (b, t, 0)),
        out_shape=jax.ShapeDtypeStruct((B, T, W), BF16),
        scratch_shapes=[pltpu.VMEM((W // LANES, LANES, LANES), F32),
                        pltpu.VMEM((8, PC), F32),
                        rows, rows, mats, mats,
                        pltpu.VMEM((tb, W), F32), pltpu.VMEM((tb, W), F32),
                        pltpu.VMEM((tb, W), F32)],
        compiler_params=pltpu.CompilerParams(
            dimension_semantics=("parallel", "arbitrary"), vmem_limit_bytes=VMEM_LIMIT),
        name="rwkv7",
    )(pc, mu, vecs, lora_w, g_up, tri, g01)


def _ffn_kernel(final, layer, widths, x_ref, ya_ref, yb_ref, yc_ref, mod_ref, g_ref, wo_hbm,
                wg_hbm, wu_hbm, wd_hbm, fg_ref, o_ref,
                wo_ref, wg_ref, wu_ref, wd_ref, stage_ref, sem_ref):
    @pl.when(_is_first_step())
    def _():
        for src, dst in ((wo_hbm, wo_ref), (wg_hbm, wg_ref), (wu_hbm, wu_ref),
                         (wd_hbm, wd_ref)):
            _load_weight_bf16(src, layer, dst, stage_ref, sem_ref)

    x = x_ref[0]
    acc = None
    off = 0
    for ref, wd in zip((ya_ref, yb_ref, yc_ref), widths):
        d = jnp.dot(ref[0], wo_ref[off:off + wd, :], preferred_element_type=F32)
        acc = d if acc is None else acc + d
        off += wd
    x1 = x + mod_ref[0, 2:3, :] * acc
    ms = jnp.mean(x1 * x1, axis=-1, keepdims=True)
    h = x1 * lax.rsqrt(ms + NORM_EPS) * g_ref[...]
    h = h * (1.0 + mod_ref[0, 4:5, :]) + mod_ref[0, 3:4, :]
    hb = h.astype(BF16)
    gt = jnp.dot(hb, wg_ref[...], preferred_element_type=F32)
    up = jnp.dot(hb, wu_ref[...], preferred_element_type=F32)
    act = (gt * _sigmoid(gt) * up).astype(BF16)
    dn = jnp.dot(act, wd_ref[...], preferred_element_type=F32)
    x2 = x1 + mod_ref[0, 5:6, :] * dn
    if final:
        ms2 = jnp.mean(x2 * x2, axis=-1, keepdims=True)
        x2 = x2 * lax.rsqrt(ms2 + NORM_EPS) * fg_ref[...]
    o_ref[0] = x2


def _out_ffn(x, ya, yb, yc, mod_l, gain, wo, wg, wu, wd, final_g, layer, tm):
    B, T, D = x.shape
    L, _, FF = wg.shape
    widths = (ya.shape[2], yb.shape[2], yc.shape[2])
    hbm = pl.BlockSpec(memory_space=pl.ANY)
    return pl.pallas_call(
        functools.partial(_ffn_kernel, layer == L - 1, layer, widths),
        grid=(B, T // tm),
        in_specs=[
            pl.BlockSpec((1, tm, D), lambda b, t: (b, t, 0)),
            pl.BlockSpec((1, tm, widths[0]), lambda b, t: (b, t, 0)),
            pl.BlockSpec((1, tm, widths[1]), lambda b, t: (b, t, 0)),
            pl.BlockSpec((1, tm, widths[2]), lambda b, t: (b, t, 0)),
            pl.BlockSpec((1, 6, D), lambda b, t: (b, 0, 0)),
            pl.BlockSpec((1, D), lambda b, t: (0, 0)),
            hbm, hbm, hbm, hbm,
            pl.BlockSpec((1, D), lambda b, t: (0, 0)),
        ],
        out_specs=pl.BlockSpec((1, tm, D), lambda b, t: (b, t, 0)),
        out_shape=jax.ShapeDtypeStruct((B, T, D), F32),
        scratch_shapes=[pltpu.VMEM((D, D), BF16), pltpu.VMEM((D, FF), BF16),
                        pltpu.VMEM((D, FF), BF16), pltpu.VMEM((FF, D), BF16),
                        pltpu.VMEM((2, W_CHUNK, max(D, FF)), F32),
                        pltpu.SemaphoreType.DMA((2,))],
        compiler_params=pltpu.CompilerParams(
            dimension_semantics=("arbitrary", "arbitrary"), vmem_limit_bytes=VMEM_LIMIT),
        name="out_ffn",
    )(x, ya, yb, yc, mod_l, gain, wo, wg, wu, wd, final_g)


def _block_diag(w):
    G, n, _ = w.shape
    eye = jnp.eye(G, dtype=w.dtype)
    return (eye[:, None, :, None] * w[:, :, None, :]).reshape(G * n, G * n)


def _seg_ones():
    idx = np.arange(LANES) // HEAD
    return jnp.asarray((idx[:, None] == idx[None, :]).astype(np.float32), dtype=BF16)


def kernel(x, c, norm1_g, norm2_g, ada_w, ada_b, w_in, conv_w, conv_b, rg_w, rg_b, ig_w, ig_b, lru_lam, hgrn_lb, rwkv_mu, rwkv_w0, rwkv_w_up, rwkv_a0, rwkv_a_up, rwkv_g_up, rwkv_k_k, rwkv_k_a, rwkv_r_k, rwkv_lnx_w, rwkv_lnx_b, mix_beta, w_out, ffn_w_gate, ffn_w_up, ffn_w_down, final_g):
    B, T, D = x.shape
    L = w_in.shape[0]
    WA = conv_w.shape[2]
    WB = hgrn_lb.shape[1]
    WC = rwkv_w0.shape[1]
    widths_p = (2 * WA, 4 * WB, rwkv_mu.shape[1])

    mod = _modulation(c, ada_w, ada_b).reshape(L, B, 6, D)
    wst = jnp.asarray(_hgrn_weights(HGRN_GROUP), dtype=BF16)
    tri = jnp.asarray(np.kron(np.eye(RWKV_GROUP, dtype=np.float32),
                              np.tril(np.ones((CHUNK, CHUNK), np.float32))), dtype=BF16)
    g_a = g_b = g_c = _seg_ones()

    for l in range(L):
        beta = mix_beta[l]
        vec_a = jnp.stack([conv_b[l], rg_b[l], ig_b[l], lru_lam[l], beta[:WA],
                           jnp.zeros_like(beta[:WA]), jnp.zeros_like(beta[:WA]),
                           jnp.zeros_like(beta[:WA])])
        wg_a = jnp.concatenate([_block_diag(rg_w[l]), _block_diag(ig_w[l])], axis=1).astype(BF16)
        ya, pb, pc = _in_projection(x, mod[l], norm1_g[l][None], w_in, conv_w[l], vec_a, wg_a,
                                    g_a, widths_p, l, IN_ROWS)
        yb = _hgrn(pb, hgrn_lb, beta[None, WA:WA + WB], wst, g_b, l, HGRN_ROWS)
        vec_c = jnp.stack([rwkv_w0[l], rwkv_a0[l], rwkv_k_k[l], rwkv_k_a[l],
                           rwkv_r_k[l].reshape(-1), rwkv_lnx_w[l], rwkv_lnx_b[l],
                           beta[WA + WB:]])
        zw = jnp.zeros_like(rwkv_w_up[l])
        lora_w = jnp.concatenate(
            [jnp.concatenate([rwkv_w_up[l], zw], axis=1),
             jnp.concatenate([zw, rwkv_a_up[l]], axis=1)], axis=0)
        yc = _rwkv(pc, rwkv_mu[l][None], vec_c, lora_w, rwkv_g_up[l], tri, g_c, RWKV_ROWS)
        x = _out_ffn(x, ya, yb, yc, mod[l], norm2_g[l][None], w_out, ffn_w_gate, ffn_w_up,
                     ffn_w_down, final_g[None], l, FFN_ROWS)
    return x
```

```python
import functools

import numpy as np
import jax
import jax.numpy as jnp
from jax import lax
from jax.experimental import pallas as pl
from jax.experimental.pallas import tpu as pltpu

F32 = jnp.float32
BF16 = jnp.bfloat16

HEAD = 64
HEAD_BITS = 6
LANES = 128
SUBLANES = 8
W_CHUNK = 128
W_SLOTS = 4
CONV_W = 4
LRU_C = 8.0
NORM_EPS = 1e-6
GN_EPS = 64e-5
CHUNK = 64
N_LEVELS = 6
RWKV_GROUP = 4
SCAN_EVERY = 2
IN_ROWS = 512
FFN_ROWS = 512
MIXER_ROWS = 1024
HGRN_GROUP = 2
LORA_W = 64
LORA_A = 64
LORA_G = 128
VMEM_LIMIT = 56 * 1024 * 1024

NN = (((1,), (0,)), ((), ()))
NT = (((1,), (1,)), ((), ()))

P_ATT = (1, 1)
P_INV = (1, 1)
P_APP = (1, 1)
P_STATE = (1, 1)
P_HAPP = (1, 1)


def _split(x, n):
    if x.dtype == BF16:
        return [x]
    parts = []
    r = x
    for i in range(n):
        p = r.astype(BF16)
        parts.append(p)
        if i + 1 < n:
            r = r - p.astype(F32)
    return parts


def _mm(a, b, dims=NN, na=2, nb=2):
    pa = _split(a, na)
    pb = _split(b, nb)
    order = max(len(pa), len(pb))
    acc = None
    for i, x in enumerate(pa):
        for j, y in enumerate(pb):
            if i + j < order:
                d = lax.dot_general(x, y, dims, preferred_element_type=F32)
                acc = d if acc is None else acc + d
    return acc


def _segsum(x, g01):
    xb = x.astype(BF16)
    return jnp.concatenate(
        [jnp.dot(xb[:, o:o + LANES], g01, preferred_element_type=F32)
         for o in range(0, x.shape[1], LANES)], axis=1)


def _stack(x, m0):
    return jnp.concatenate([jnp.where(m0, x, 0.0), jnp.where(m0, 0.0, x)], axis=0)


def _sigmoid(x):
    return 0.5 * jnp.tanh(0.5 * x) + 0.5


def _log1p_exp_neg_abs(x):
    return jnp.log(1.0 + jnp.exp(-jnp.abs(x)))


def _neg_expm1(x, ex):
    return -jnp.tanh(0.5 * x) * (ex + 1.0)


def _softplus(x):
    return jnp.maximum(x, 0.0) + _log1p_exp_neg_abs(x)


def _load_weight_bf16(w_hbm, layer, dst_ref, stage_ref, sem_ref):
    K, N = dst_ref.shape
    depth = stage_ref.shape[0]

    def copy(c):
        return pltpu.make_async_copy(
            w_hbm.at[layer, pl.ds(c * W_CHUNK, W_CHUNK), :],
            stage_ref.at[c % depth, :, pl.ds(0, N)],
            sem_ref.at[c % depth])

    n = K // W_CHUNK
    for c in range(min(depth, n)):
        copy(c).start()
    for c in range(n):
        copy(c).wait()
        dst_ref[c * W_CHUNK:(c + 1) * W_CHUNK, :] = stage_ref[c % depth, :, 0:N].astype(BF16)
        if c + depth < n:
            copy(c + depth).start()


def _is_first_step():
    return (pl.program_id(0) == 0) & (pl.program_id(1) == 0)


def _mod_kernel(c_ref, w_ref, b_ref, o_ref):
    c = c_ref[...]
    cs = c * _sigmoid(c)
    o_ref[0] = _mm(cs, w_ref[0], NN, na=2, nb=1) + b_ref[0]


def _modulation(c, ada_w, ada_b):
    L, D, N = ada_w.shape
    B = c.shape[0]
    tn = N // 4
    return pl.pallas_call(
        _mod_kernel,
        grid=(L, N // tn),
        in_specs=[
            pl.BlockSpec((B, D), lambda l, j: (0, 0)),
            pl.BlockSpec((1, D, tn), lambda l, j: (l, 0, j)),
            pl.BlockSpec((1, 1, tn), lambda l, j: (l, 0, j)),
        ],
        out_specs=pl.BlockSpec((1, B, tn), lambda l, j: (l, 0, j)),
        out_shape=jax.ShapeDtypeStruct((L, B, N), F32),
        compiler_params=pltpu.CompilerParams(
            dimension_semantics=("parallel", "parallel"), vmem_limit_bytes=VMEM_LIMIT),
        name="modulation",
    )(c, ada_w, ada_b.reshape(L, 1, N))


def _inproj_kernel(widths, layer, x_ref, mod_ref, g_ref, w_hbm, cw_ref, vec_ref, wg_ref,
                   seg_ref, ya_ref, pb_ref, pc_ref, xc_ref, hc_ref, w_ref, stage_ref, sem_ref):
    wa, wb, wc = widths
    W = wa // 2
    tm = x_ref.shape[1]
    @pl.when(_is_first_step())
    def _():
        _load_weight_bf16(w_hbm, layer, w_ref, stage_ref, sem_ref)

    @pl.when(pl.program_id(1) == 0)
    def _():
        xc_ref[...] = jnp.zeros_like(xc_ref)
        hc_ref[...] = jnp.zeros_like(hc_ref)

    x = x_ref[0]
    ms = jnp.mean(x * x, axis=-1, keepdims=True)
    h = x * lax.rsqrt(ms + NORM_EPS) * g_ref[...]
    h = h * (1.0 + mod_ref[0, 1:2, :]) + mod_ref[0, 0:1, :]
    hb = h.astype(BF16)

    conv_b = vec_ref[0:1, :]
    rg_b = vec_ref[1:2, :]
    ig_b = vec_ref[2:3, :]
    sp_lam = _softplus(-vec_ref[3:4, :])
    beta = vec_ref[4:5, :]
    row = lax.broadcasted_iota(jnp.int32, (tm, W), 0)

    pa = jnp.dot(hb, w_ref[:, 0:wa], preferred_element_type=F32)
    xa = pa[:, 0:W]
    ya = pa[:, W:wa]
    win = jnp.concatenate([xc_ref[...], xa], axis=0)
    xc_ref[...] = xa[tm - SUBLANES:tm, :]
    u = conv_b + cw_ref[CONV_W - 1:CONV_W, :] * xa
    for j in range(1, CONV_W):
        u = u + cw_ref[CONV_W - 1 - j:CONV_W - j, :] * pltpu.roll(win, j, 0)[SUBLANES:]
    pb_ref[0] = jnp.dot(hb, w_ref[:, wa:wa + wb], preferred_element_type=F32)
    z = jnp.dot(u.astype(BF16), wg_ref[...], preferred_element_type=F32)
    pc_ref[0] = jnp.dot(hb, w_ref[:, wa + wb:wa + wb + wc], preferred_element_type=F32)

    r = _sigmoid(z[:, 0:W] + rg_b)
    ig = _sigmoid(z[:, W:2 * W] + ig_b)
    log_a = (-LRU_C) * r * sp_lam
    a = jnp.exp(log_a)
    mult = jnp.sqrt(_neg_expm1(2.0 * log_a, a * a))
    mult = jnp.where(row + pl.program_id(1) * tm == 0, 1.0, mult)
    bv = mult * (ig * u)
    s = 1
    while s < SUBLANES:
        keep = (row & (SUBLANES - 1)) >= s
        a_s = jnp.where(keep, pltpu.roll(a, s, 0), 1.0)
        b_s = jnp.where(keep, pltpu.roll(bv, s, 0), 0.0)
        bv = a * b_s + bv
        a = a * a_s
        s *= 2
    hprev = hc_ref[0:1, :]
    hs = []
    for j in range(tm // SUBLANES):
        blk = a[j * SUBLANES:(j + 1) * SUBLANES] * hprev + bv[j * SUBLANES:(j + 1) * SUBLANES]
        hs.append(blk)
        hprev = blk[SUBLANES - 1:SUBLANES, :]
    hc_ref[...] = jnp.broadcast_to(hprev, hc_ref.shape)
    y = jnp.concatenate(hs, axis=0) * jax.nn.gelu(ya)
    ms_a = _segsum(y * y, seg_ref[...]) * (1.0 / HEAD)
    ya_ref[0] = (y * lax.rsqrt(ms_a + NORM_EPS) * beta).astype(ya_ref.dtype)


def _in_projection(x, mod_l, gain, w_in, conv_w, vecs, wg_bf, g01, widths, layer, tm):
    B, T, D = x.shape
    P = w_in.shape[2]
    W = widths[0] // 2
    small = lambda a: pl.BlockSpec(a.shape, lambda b, t: (0, 0))
    return pl.pallas_call(
        functools.partial(_inproj_kernel, widths, layer),
        grid=(B, T // tm),
        in_specs=[
            pl.BlockSpec((1, tm, D), lambda b, t: (b, t, 0)),
            pl.BlockSpec((1, 6, D), lambda b, t: (b, 0, 0)),
            pl.BlockSpec((1, D), lambda b, t: (0, 0)),
            pl.BlockSpec(memory_space=pl.ANY),
            small(conv_w), small(vecs), small(wg_bf), small(g01),
        ],
        out_specs=[pl.BlockSpec((1, tm, W), lambda b, t: (b, t, 0)),
                   pl.BlockSpec((1, tm, widths[1]), lambda b, t: (b, t, 0)),
                   pl.BlockSpec((1, tm, widths[2]), lambda b, t: (b, t, 0))],
        out_shape=[jax.ShapeDtypeStruct((B, T, W), BF16),
                   jax.ShapeDtypeStruct((B, T, widths[1]), F32),
                   jax.ShapeDtypeStruct((B, T, widths[2]), F32)],
        scratch_shapes=[pltpu.VMEM((SUBLANES, W), F32), pltpu.VMEM((SUBLANES, W), F32),
                        pltpu.VMEM((D, P), BF16), pltpu.VMEM((W_SLOTS, W_CHUNK, P), F32),
                        pltpu.SemaphoreType.DMA((W_SLOTS,))],
        compiler_params=pltpu.CompilerParams(
            dimension_semantics=("arbitrary", "arbitrary"), vmem_limit_bytes=VMEM_LIMIT),
        name="in_projection",
    )(x, mod_l, gain, w_in, conv_w, vecs, wg_bf, g01)


def _drive(n_groups, group, elem_steps, mm_stages, scan_chunk):
    vals = []
    for _ in elem_steps(0, vals):
        yield
    ready = []
    for gi in range(n_groups):
        nxt = elem_steps(gi + 1, vals) if gi + 1 < n_groups else iter(())
        for si, _ in enumerate(mm_stages(gi, vals[gi])):
            next(nxt, None)
            if ready and si % SCAN_EVERY == SCAN_EVERY - 1:
                scan_chunk(ready.pop(0))
            yield
        for _ in nxt:
            yield
        ready += [gi * group + g for g in range(group)]
    for ci in ready:
        scan_chunk(ci)
        yield


def _hgrn_program(layer, pb_ref, lb_ref, beta_ref, wst_ref, g_ref, o_ref,
                  s_ref, qin_ref, oi_ref, kv_ref, dec_ref):
    C = CHUNK
    R = wst_ref.shape[1]
    G = R // C
    TB = pb_ref.shape[1]
    W = o_ref.shape[2]
    n_pairs = W // LANES

    @pl.when(pl.program_id(1) == 0)
    def _():
        s_ref[...] = jnp.zeros_like(s_ref)

    lbr = lb_ref[...]
    n_layers = lbr.shape[0]
    mx = lbr[0:1, :]
    for i in range(1, n_layers):
        mx = jnp.maximum(mx, lbr[i:i + 1, :])
    es = [jnp.exp(lbr[i:i + 1, :] - mx) for i in range(n_layers)]
    tot = es[0]
    for i in range(1, n_layers):
        tot = tot + es[i]
    sm = [e / tot for e in es]
    cum = sm[0]
    for i in range(1, layer + 1):
        cum = cum + sm[i]
    lb = cum - sm[0]
    log_lb = jnp.log(lb)
    log_1mlb = jnp.log(1.0 - lb)

    beta = beta_ref[...]
    g01 = g_ref[...]
    wst = wst_ref[...]
    lane = lax.broadcasted_iota(jnp.int32, (C, LANES), 1)
    m0 = lane < HEAD
    tw = lax.broadcasted_iota(jnp.int32, (C, 2 * C), 0)
    sw = lax.broadcasted_iota(jnp.int32, (C, 2 * C), 1) & (C - 1)
    eye = tw == sw
    level = []
    for lev in range(N_LEVELS):
        th = tw >> (N_LEVELS - 1 - lev)
        sh = sw >> (N_LEVELS - 1 - lev)
        level.append((th == sh + 1) & ((sh & 1) == 0))
    r2 = lax.broadcasted_iota(jnp.int32, (LANES, LANES), 0)
    c2 = lax.broadcasted_iota(jnp.int32, (LANES, LANES), 1)
    same_head = (r2 >> HEAD_BITS) == (c2 >> HEAD_BITS)

    odd_row = (lax.broadcasted_iota(jnp.int32, (C, LANES), 0) & 1) == 1
    n_kinds = wst_ref.shape[0] // R

    def elem_steps(gi, out):
        t0 = gi * R
        q = pb_ref[0, t0:t0 + R, 0:W] * (HEAD ** -0.5)
        fl = pb_ref[0, t0:t0 + R, W:2 * W]
        v = pb_ref[0, t0:t0 + R, 2 * W:3 * W]
        lc = log_1mlb + (jnp.minimum(fl, 0.0) - _log1p_exp_neg_abs(fl))
        log_f = jnp.maximum(log_lb, lc) + _log1p_exp_neg_abs(log_lb - lc)
        expo = _mm(wst, log_f, NN, na=1, nb=2)
        yield
        f = jnp.exp(log_f)
        kg = _neg_expm1(log_f, f)
        ex = jnp.exp(expo)
        qin_ref[t0:t0 + R, :] = q * ex[0:R]
        for g in range(G):
            dec_ref[gi * G + g] = jnp.broadcast_to(ex[(g + 1) * C - 1:(g + 1) * C, :], (8, W))
        out.append((q, kg, v, f, ex))

    def mm_stages(gi, vals):
        q, kg, v, f, ex = vals
        items = [(g, p) for g in range(G) for p in range(n_pairs)]
        n = range(len(items))

        def piece(x, g, p, base=0):
            return x[base + g * C:base + (g + 1) * C, p * LANES:(p + 1) * LANES]

        q_p = [piece(q, g, p) for g, p in items]
        v_p = [piece(v, g, p) for g, p in items]
        k_s = [_stack(piece(kg, g, p), m0) for g, p in items]
        q_b = [x.astype(BF16) for x in q_p]
        k_b = [x.astype(BF16) for x in k_s]
        amat = [jnp.where(eye, _mm(q_b[j], k_b[j], NT), 0.0) for j in n]
        yield
        for lev in range(N_LEVELS):
            if 2 + lev < n_kinds:
                es = [piece(ex, g, p, (2 + lev) * R) for g, p in items]
            else:
                es = [jnp.where(odd_row, piece(f, g, p), 1.0) for g, p in items]
            es = [e.astype(BF16) for e in es]
            amat = [jnp.where(level[lev],
                              _mm(q_b[j] * es[j],
                                  k_b[j] * jnp.concatenate([es[j], es[j]], axis=0), NT),
                              amat[j]) for j in n]
            yield
        oi = [_mm(amat[j], _stack(v_p[j], m0), NN, *P_HAPP) for j in n]
        yield
        k_dec = kg * ex[R:2 * R]
        kv = [_mm(v_p[j].T, piece(k_dec, g, p), NN, *P_HAPP) for j, (g, p) in enumerate(items)]
        for g in range(G):
            t0 = (gi * G + g) * C
            oi_ref[t0:t0 + C, :] = jnp.concatenate(
                [oi[g * n_pairs + p] for p in range(n_pairs)], axis=1)
        for j, (g, p) in enumerate(items):
            kv_ref[(gi * G + g) * n_pairs + p] = jnp.where(same_head, kv[j], 0.0)
        yield

    def scan_chunk(ci):
        t0 = ci * C
        dec = dec_ref[ci]
        outs = []
        for p in range(n_pairs):
            sl = slice(p * LANES, (p + 1) * LANES)
            st = s_ref[p]
            outs.append(_mm(qin_ref[t0:t0 + C, sl], st, NT, *P_HAPP))
            s_ref[p] = st * dec[0:1, sl] + kv_ref[ci * n_pairs + p]
        oi_ref[t0:t0 + C, :] = oi_ref[t0:t0 + C, :] + jnp.concatenate(outs, axis=1)

    yield from _drive(TB // R, G, elem_steps, mm_stages, scan_chunk)

    o = oi_ref[...]
    g = pb_ref[0, :, 3 * W:4 * W]
    ms = _segsum(o * o, g01) * (1.0 / HEAD)
    o = o * lax.rsqrt(ms + NORM_EPS) * (g * _sigmoid(g)) * beta
    o_ref[0] = o.astype(o_ref.dtype)


def _hgrn_weights(group):
    C = CHUNK
    R = group * C
    w = np.zeros(((2 + N_LEVELS - 1) * R, R), np.float32)
    for g in range(group):
        o = g * C
        for r in range(C):
            w[o + r, o:o + r + 1] = 1.0
            w[R + o + r, o + r + 1:o + C] = 1.0
            for lev in range(N_LEVELS - 1):
                n = C >> lev
                m = (r // n) * n + n // 2 - 1
                if r % n >= n // 2:
                    w[(2 + lev) * R + o + r, o + m + 1:o + r + 1] = 1.0
                else:
                    w[(2 + lev) * R + o + r, o + r + 1:o + m + 1] = 1.0
    return w


def _rwkv_program(pc_ref, mu_ref, vec_ref, lora_ref, gup_ref, tri_ref, g_ref, o_ref,
                  s_ref, prev_ref, rp_ref, y0_ref, mx_ref, nx_ref, bonus_ref, gate_ref, y_ref):
    C = CHUNK
    R = tri_ref.shape[0]
    G = R // C
    TB = pc_ref.shape[1]
    W = o_ref.shape[2]
    n_pairs = W // LANES

    @pl.when(pl.program_id(1) == 0)
    def _():
        s_ref[...] = jnp.zeros_like(s_ref)
        prev_ref[...] = jnp.zeros_like(prev_ref)

    mu = mu_ref[...]
    w0 = vec_ref[0:1, :]
    a0 = vec_ref[1:2, :]
    k_k = vec_ref[2:3, :]
    k_a = vec_ref[3:4, :]
    r_k = vec_ref[4:5, :]
    lnx_w = vec_ref[5:6, :]
    lnx_b = vec_ref[6:7, :]
    beta = vec_ref[7:8, :]
    lora = _split(lora_ref[...], 1)
    gup = _split(gup_ref[...], 1)
    tri = tri_ref[...]
    g01 = g_ref[...]
    lane = lax.broadcasted_iota(jnp.int32, (C, LANES), 1)
    lane_r = lax.broadcasted_iota(jnp.int32, (R, LANES), 1)
    m0 = lane < HEAD
    tw = lax.broadcasted_iota(jnp.int32, (C, 2 * C), 0)
    sw = lax.broadcasted_iota(jnp.int32, (C, 2 * C), 1) & (C - 1)
    strict = tw > sw
    incl = tw >= sw
    eye_w = jnp.where(tw == sw, 1.0, 0.0)
    r2 = lax.broadcasted_iota(jnp.int32, (2 * C, 2 * C), 0)
    c2 = lax.broadcasted_iota(jnp.int32, (2 * C, 2 * C), 1)
    same_head = (r2 >> HEAD_BITS) == (c2 >> HEAD_BITS)
    eye_f = jnp.where(r2 == c2, 1.0, 0.0)
    o_w = 3 * W
    o_g = o_w + LORA_W + LORA_A

    def mm_parts(x, parts):
        return jnp.dot(x.astype(BF16), parts[0], preferred_element_type=F32)

    def elem_steps(gi, out):
        t0 = gi * R
        pc = pc_ref[0, t0:t0 + R, :]
        prev8 = prev_ref[...] if gi == 0 else pc_ref[0, t0 - 8:t0, :]
        prev = pltpu.roll(jnp.concatenate([prev8, pc], axis=0), 1, 0)[8:]
        ps = pc + (prev - pc) * mu
        r = ps[:, 0:W]
        k = ps[:, W:2 * W]
        v = ps[:, 2 * W:3 * W]
        z = ps[:, o_w:o_g]
        z = jnp.where(lane_r < LORA_W, jnp.tanh(z), z)
        lo = mm_parts(z, lora)
        yield
        gate = mm_parts(_sigmoid(ps[:, o_g:o_g + LORA_G]), gup)
        gate_ref[t0:t0 + R, :] = gate * beta
        yield
        w_raw = -_softplus(-(w0 + lo[:, 0:W])) - 0.5
        lw = -jnp.exp(w_raw)
        gam = _mm(tri, lw, NN, na=1, nb=2)
        yield
        a_sig = _sigmoid(a0 + lo[:, W:2 * W])
        kk = k * k_k
        kk = kk * lax.rsqrt(_segsum(kk * kk, g01) + 1e-12)
        yield
        k2 = k * (1.0 + (a_sig - 1.0) * k_a)
        bonus_ref[t0:t0 + R, :] = _segsum(r * k2 * r_k, g01) * v
        yield
        e_in = jnp.exp(gam)
        e_ex = jnp.exp(gam - lw)
        e_neg = jnp.exp(-gam)
        out.append((-kk * e_ex, r * e_in, kk * a_sig * e_neg, k2 * e_neg, v, e_in))

    def mm_stages(gi, vals):
        at, rt, bt, kt, v, e_in = vals
        items = [(g, p) for g in range(G) for p in range(n_pairs)]
        n = range(len(items))

        def pieces(x):
            return [x[g * C:(g + 1) * C, p * LANES:(p + 1) * LANES] for g, p in items]

        def st(x):
            return _stack(x, m0)

        a_p, r_p, b_p, k_p, v_p = pieces(at), pieces(rt), pieces(bt), pieces(kt), pieces(v)
        v_s = [st(v_p[j]) for j in n]
        att = [_mm(jnp.concatenate([a_p[j], r_p[j]], axis=0),
                   jnp.concatenate([st(b_p[j]), st(k_p[j])], axis=0), NT, *P_ATT)
               for j in n]
        yield
        lmat = [jnp.where(strict, att[j][0:C, 0:2 * C], 0.0) for j in n]
        ak = [jnp.where(strict, att[j][0:C, 2 * C:4 * C], 0.0) for j in n]
        rb = [jnp.where(incl, att[j][C:2 * C, 0:2 * C], 0.0) for j in n]
        rk = [jnp.where(incl, att[j][C:2 * C, 2 * C:4 * C], 0.0) for j in n]
        akv = [_mm(ak[j], v_s[j], NN, *P_APP) for j in n]
        yield
        pw = [_mm(lmat[j], st(lmat[j]), NN, *P_INV) for j in n]
        xs = [eye_w + lmat[j] for j in n]
        yield
        for _ in range(N_LEVELS - 2):
            z = [_mm(pw[j], jnp.concatenate([st(xs[j]), st(pw[j])], axis=1), NN, *P_INV)
                 for j in n]
            xs = [xs[j] + z[j][:, 0:2 * C] for j in n]
            pw = [z[j][:, 2 * C:4 * C] for j in n]
            yield
        xs = [xs[j] + _mm(pw[j], st(xs[j]), NN, *P_INV) for j in n]
        yield
        au = [_mm(xs[j], jnp.concatenate([st(a_p[j]), st(akv[j])], axis=1), NN, *P_APP)
              for j in n]
        ap = [au[j][:, 0:LANES] for j in n]
        u0 = [au[j][:, LANES:2 * LANES] for j in n]
        yield
        ry = [_mm(rb[j], jnp.concatenate([st(ap[j]), st(u0[j])], axis=1), NN, *P_APP)
              for j in n]
        rkv = [_mm(rk[j], v_s[j], NN, *P_APP) for j in n]
        yield
        mx = [_mm(ap[j].T, b_p[j], NN, *P_APP) for j in n]
        nx = [_mm(jnp.concatenate([u0[j], v_p[j]], axis=0).T,
                  jnp.concatenate([b_p[j], k_p[j]], axis=0), NN, *P_APP) for j in n]
        for j, (g, p) in enumerate(items):
            idx = (gi * G + g) * n_pairs + p
            glast = e_in[(g + 1) * C - 1:(g + 1) * C, p * LANES:(p + 1) * LANES]
            rp_ref[idx] = r_p[j] + ry[j][:, 0:LANES]
            y0_ref[idx] = ry[j][:, LANES:2 * LANES] + rkv[j]
            mx_ref[idx] = (eye_f + jnp.where(same_head, mx[j], 0.0)) * glast
            nx_ref[idx] = jnp.where(same_head, nx[j], 0.0) * glast
        yield

    def scan_chunk(ci):
        pairs = range(n_pairs)
        sts = [s_ref[p] for p in pairs]
        ys = [_mm(rp_ref[ci * n_pairs + p], sts[p], NT, *P_STATE) for p in pairs]
        sn = [_mm(sts[p], mx_ref[ci * n_pairs + p], NN, *P_STATE) for p in pairs]
        for p in pairs:
            s_ref[p] = sn[p] + nx_ref[ci * n_pairs + p]
        y_ref[ci * C:(ci + 1) * C, :] = jnp.concatenate(
            [ys[p] + y0_ref[ci * n_pairs + p] for p in pairs], axis=1)

    yield from _drive(TB // R, G, elem_steps, mm_stages, scan_chunk)
    prev_ref[...] = pc_ref[0, TB - 8:TB, :]

    y = y_ref[...]
    mean = _segsum(y, g01) * (1.0 / HEAD)
    d = y - mean
    var = _segsum(d * d, g01) * (1.0 / HEAD)
    yn = d * lax.rsqrt(var + GN_EPS) * lnx_w + lnx_b
    o_ref[0] = ((yn + bonus_ref[...]) * gate_ref[...]).astype(o_ref.dtype)


N_HGRN_IN, N_HGRN_SCRATCH = 5, 5
N_RWKV_IN, N_RWKV_SCRATCH = 7, 9


def _mixers_kernel(layer, *refs):
    it = iter(refs)
    take = lambda k: [next(it) for _ in range(k)]
    h_in, r_in = take(N_HGRN_IN), take(N_RWKV_IN)
    h_out, r_out = take(1), take(1)
    h_scr, r_scr = take(N_HGRN_SCRATCH), take(N_RWKV_SCRATCH)
    programs = [_hgrn_program(layer, *h_in, *h_out, *h_scr),
                _rwkv_program(*r_in, *r_out, *r_scr)]
    while programs:
        for prog in list(programs):
            if next(prog, programs) is programs:
                programs.remove(prog)


def _mixers(pb, pc, hgrn_lb, beta_b, wst, mu, vecs, lora_w, g_up, tri, g01, layer, tb):
    B, T, W4 = pb.shape
    W = W4 // 4
    PC = pc.shape[2]
    n_mats = (tb // CHUNK) * (W // LANES)
    mats = pltpu.VMEM((n_mats, LANES, LANES), F32)
    rows = pltpu.VMEM((n_mats, CHUNK, LANES), F32)
    block = pltpu.VMEM((tb, W), F32)
    state = pltpu.VMEM((W // LANES, LANES, LANES), F32)
    small = lambda a: pl.BlockSpec(a.shape, lambda b, t: (0, 0))
    out = pl.BlockSpec((1, tb, W), lambda b, t: (b, t, 0))
    return pl.pallas_call(
        functools.partial(_mixers_kernel, layer),
        grid=(B, T // tb),
        in_specs=[
            pl.BlockSpec((1, tb, W4), lambda b, t: (b, t, 0)),
            small(hgrn_lb), small(beta_b), small(wst), small(g01),
            pl.BlockSpec((1, tb, PC), lambda b, t: (b, t, 0)),
            small(mu), small(vecs), small(lora_w), small(g_up), small(tri), small(g01),
        ],
        out_specs=[out, out],
        out_shape=[jax.ShapeDtypeStruct((B, T, W), BF16)] * 2,
        scratch_shapes=[state, block, block, mats, pltpu.VMEM((tb // CHUNK, 8, W), F32),
                        state, pltpu.VMEM((8, PC), F32), rows, rows, mats, mats,
                        block, block, block],
        compiler_params=pltpu.CompilerParams(
            dimension_semantics=("parallel", "arbitrary"), vmem_limit_bytes=VMEM_LIMIT),
        name="mixers",
    )(pb, hgrn_lb, beta_b, wst, g01, pc, mu, vecs, lora_w, g_up, tri, g01)


def _ffn_kernel(final, layer, widths, x_ref, ya_ref, yb_ref, yc_ref, mod_ref, g_ref, wo_hbm,
                wg_hbm, wu_hbm, wd_hbm, fg_ref, o_ref,
                wo_ref, wg_ref, wu_ref, wd_ref, stage_ref, sem_ref):
    @pl.when(_is_first_step())
    def _():
        for src, dst in ((wo_hbm, wo_ref), (wg_hbm, wg_ref), (wu_hbm, wu_ref),
                         (wd_hbm, wd_ref)):
            _load_weight_bf16(src, layer, dst, stage_ref, sem_ref)

    x = x_ref[0]
    acc = None
    off = 0
    for ref, wd in zip((ya_ref, yb_ref, yc_ref), widths):
        d = jnp.dot(ref[0], wo_ref[off:off + wd, :], preferred_element_type=F32)
        acc = d if acc is None else acc + d
        off += wd
    x1 = x + mod_ref[0, 2:3, :] * acc
    ms = jnp.mean(x1 * x1, axis=-1, keepdims=True)
    h = x1 * lax.rsqrt(ms + NORM_EPS) * g_ref[...]
    h = h * (1.0 + mod_ref[0, 4:5, :]) + mod_ref[0, 3:4, :]
    hb = h.astype(BF16)
    gt = jnp.dot(hb, wg_ref[...], preferred_element_type=F32)
    up = jnp.dot(hb, wu_ref[...], preferred_element_type=F32)
    act = (gt * _sigmoid(gt) * up).astype(BF16)
    dn = jnp.dot(act, wd_ref[...], preferred_element_type=F32)
    x2 = x1 + mod_ref[0, 5:6, :] * dn
    if final:
        ms2 = jnp.mean(x2 * x2, axis=-1, keepdims=True)
        x2 = x2 * lax.rsqrt(ms2 + NORM_EPS) * fg_ref[...]
    o_ref[0] = x2


def _out_ffn(x, ya, yb, yc, mod_l, gain, wo, wg, wu, wd, final_g, layer, tm):
    B, T, D = x.shape
    L, _, FF = wg.shape
    widths = (ya.shape[2], yb.shape[2], yc.shape[2])
    hbm = pl.BlockSpec(memory_space=pl.ANY)
    return pl.pallas_call(
        functools.partial(_ffn_kernel, layer == L - 1, layer, widths),
        grid=(B, T // tm),
        in_specs=[
            pl.BlockSpec((1, tm, D), lambda b, t: (b, t, 0)),
            pl.BlockSpec((1, tm, widths[0]), lambda b, t: (b, t, 0)),
            pl.BlockSpec((1, tm, widths[1]), lambda b, t: (b, t, 0)),
            pl.BlockSpec((1, tm, widths[2]), lambda b, t: (b, t, 0)),
            pl.BlockSpec((1, 6, D), lambda b, t: (b, 0, 0)),
            pl.BlockSpec((1, D), lambda b, t: (0, 0)),
            hbm, hbm, hbm, hbm,
            pl.BlockSpec((1, D), lambda b, t: (0, 0)),
        ],
        out_specs=pl.BlockSpec((1, tm, D), lambda b, t: (b, t, 0)),
        out_shape=jax.ShapeDtypeStruct((B, T, D), F32),
        scratch_shapes=[pltpu.VMEM((D, D), BF16), pltpu.VMEM((D, FF), BF16),
                        pltpu.VMEM((D, FF), BF16), pltpu.VMEM((FF, D), BF16),
                        pltpu.VMEM((W_SLOTS, W_CHUNK, max(D, FF)), F32),
                        pltpu.SemaphoreType.DMA((W_SLOTS,))],
        compiler_params=pltpu.CompilerParams(
            dimension_semantics=("arbitrary", "arbitrary"), vmem_limit_bytes=VMEM_LIMIT),
        name="out_ffn",
    )(x, ya, yb, yc, mod_l, gain, wo, wg, wu, wd, final_g)


def _block_diag(w):
    G, n, _ = w.shape
    eye = jnp.eye(G, dtype=w.dtype)
    return (eye[:, None, :, None] * w[:, :, None, :]).reshape(G * n, G * n)


def _seg_ones():
    idx = np.arange(LANES) // HEAD
    return jnp.asarray((idx[:, None] == idx[None, :]).astype(np.float32), dtype=BF16)


def kernel(x, c, norm1_g, norm2_g, ada_w, ada_b, w_in, conv_w, conv_b, rg_w, rg_b, ig_w, ig_b, lru_lam, hgrn_lb, rwkv_mu, rwkv_w0, rwkv_w_up, rwkv_a0, rwkv_a_up, rwkv_g_up, rwkv_k_k, rwkv_k_a, rwkv_r_k, rwkv_lnx_w, rwkv_lnx_b, mix_beta, w_out, ffn_w_gate, ffn_w_up, ffn_w_down, final_g):
    B, T, D = x.shape
    L = w_in.shape[0]
    WA = conv_w.shape[2]
    WB = hgrn_lb.shape[1]
    WC = rwkv_w0.shape[1]
    widths_p = (2 * WA, 4 * WB, rwkv_mu.shape[1])

    mod = _modulation(c, ada_w, ada_b).reshape(L, B, 6, D)
    wst = jnp.asarray(_hgrn_weights(HGRN_GROUP), dtype=BF16)
    tri = jnp.asarray(np.kron(np.eye(RWKV_GROUP, dtype=np.float32),
                              np.tril(np.ones((CHUNK, CHUNK), np.float32))), dtype=BF16)
    g_a = _seg_ones()

    for l in range(L):
        beta = mix_beta[l]
        vec_a = jnp.stack([conv_b[l], rg_b[l], ig_b[l], lru_lam[l], beta[:WA],
                           jnp.zeros_like(beta[:WA]), jnp.zeros_like(beta[:WA]),
                           jnp.zeros_like(beta[:WA])])
        wg_a = jnp.concatenate([_block_diag(rg_w[l]), _block_diag(ig_w[l])], axis=1).astype(BF16)
        ya, pb, pc = _in_projection(x, mod[l], norm1_g[l][None], w_in, conv_w[l], vec_a, wg_a,
                                    g_a, widths_p, l, IN_ROWS)
        vec_c = jnp.stack([rwkv_w0[l], rwkv_a0[l], rwkv_k_k[l], rwkv_k_a[l],
                           rwkv_r_k[l].reshape(-1), rwkv_lnx_w[l], rwkv_lnx_b[l],
                           beta[WA + WB:]])
        zw = jnp.zeros_like(rwkv_w_up[l])
        lora_w = jnp.concatenate(
            [jnp.concatenate([rwkv_w_up[l], zw], axis=1),
             jnp.concatenate([zw, rwkv_a_up[l]], axis=1)], axis=0)
        yb, yc = _mixers(pb, pc, hgrn_lb, beta[None, WA:WA + WB], wst, rwkv_mu[l][None], vec_c,
                         lora_w, rwkv_g_up[l], tri, g_a, l, MIXER_ROWS)
        x = _out_ffn(x, ya, yb, yc, mod[l], norm2_g[l][None], w_out, ffn_w_gate, ffn_w_up,
                     ffn_w_down, final_g[None], l, FFN_ROWS)
    return x
```

```python
import functools
import math

import numpy as np
import jax
import jax.numpy as jnp
from jax import lax
from jax.experimental import pallas as pl
from jax.experimental.pallas import tpu as pltpu

F32 = jnp.float32
BF16 = jnp.bfloat16

HEAD = 64
HEAD_BITS = 6
LANES = 128
SUBLANES = 8
W_CHUNK = 128
W_SLOTS = 4
CONV_W = 4
LRU_C = 8.0
NORM_EPS = 1e-6
GN_EPS = 64e-5
F_MIN = 1e-37
CHUNK = 64
N_LEVELS = 6
RWKV_GROUP = 4
SCAN_EVERY = 2
IN_ROWS = 512
FFN_ROWS = 512
MIXER_ROWS = 1024
HGRN_GROUP = 2
LORA_W = 64
LORA_A = 64
LORA_G = 128
VMEM_LIMIT = 56 * 1024 * 1024

NN = (((1,), (0,)), ((), ()))
NT = (((1,), (1,)), ((), ()))

P_ATT = (1, 1)
P_INV = (1, 1)
P_APP = (1, 1)
P_STATE = (1, 1)
P_HAPP = (1, 1)


def _split(x, n):
    if x.dtype == BF16:
        return [x]
    parts = []
    r = x
    for i in range(n):
        p = r.astype(BF16)
        parts.append(p)
        if i + 1 < n:
            r = r - p.astype(F32)
    return parts


def _mm(a, b, dims=NN, na=2, nb=2):
    pa = _split(a, na)
    pb = _split(b, nb)
    order = max(len(pa), len(pb))
    acc = None
    for i, x in enumerate(pa):
        for j, y in enumerate(pb):
            if i + j < order:
                d = lax.dot_general(x, y, dims, preferred_element_type=F32)
                acc = d if acc is None else acc + d
    return acc


def _segsum(x, g01):
    xb = x.astype(BF16)
    return jnp.concatenate(
        [jnp.dot(xb[:, o:o + LANES], g01, preferred_element_type=F32)
         for o in range(0, x.shape[1], LANES)], axis=1)


def _stack(x, m0):
    return jnp.concatenate([jnp.where(m0, x, 0.0), jnp.where(m0, 0.0, x)], axis=0)


def _sigmoid(x):
    return 0.5 * jnp.tanh(0.5 * x) + 0.5


def _log1p_exp_neg_abs(x):
    return jnp.log(1.0 + jnp.exp(-jnp.abs(x)))


def _neg_expm1(x, ex):
    return -jnp.tanh(0.5 * x) * (ex + 1.0)


def _softplus(x):
    return jnp.maximum(x, 0.0) + _log1p_exp_neg_abs(x)


def _load_weight_bf16(w_hbm, layer, dst_ref, stage_ref, sem_ref):
    K, N = dst_ref.shape
    depth = stage_ref.shape[0]

    def copy(c):
        return pltpu.make_async_copy(
            w_hbm.at[layer, pl.ds(c * W_CHUNK, W_CHUNK), :],
            stage_ref.at[c % depth, :, pl.ds(0, N)],
            sem_ref.at[c % depth])

    n = K // W_CHUNK
    for c in range(min(depth, n)):
        copy(c).start()
    for c in range(n):
        copy(c).wait()
        dst_ref[c * W_CHUNK:(c + 1) * W_CHUNK, :] = stage_ref[c % depth, :, 0:N].astype(BF16)
        if c + depth < n:
            copy(c + depth).start()


def _is_first_step():
    return (pl.program_id(0) == 0) & (pl.program_id(1) == 0)


def _mod_kernel(c_ref, w_ref, b_ref, o_ref):
    c = c_ref[...]
    cs = c * _sigmoid(c)
    o_ref[0] = _mm(cs, w_ref[0], NN, na=2, nb=1) + b_ref[0]


def _modulation(c, ada_w, ada_b):
    L, D, N = ada_w.shape
    B = c.shape[0]
    tn = N // 4
    return pl.pallas_call(
        _mod_kernel,
        grid=(L, N // tn),
        in_specs=[
            pl.BlockSpec((B, D), lambda l, j: (0, 0)),
            pl.BlockSpec((1, D, tn), lambda l, j: (l, 0, j)),
            pl.BlockSpec((1, 1, tn), lambda l, j: (l, 0, j)),
        ],
        out_specs=pl.BlockSpec((1, B, tn), lambda l, j: (l, 0, j)),
        out_shape=jax.ShapeDtypeStruct((L, B, N), F32),
        compiler_params=pltpu.CompilerParams(
            dimension_semantics=("parallel", "parallel"), vmem_limit_bytes=VMEM_LIMIT),
        name="modulation",
    )(c, ada_w, ada_b.reshape(L, 1, N))


def _inproj_kernel(widths, layer, x_ref, mod_ref, g_ref, w_hbm, cw_ref, vec_ref, wg_ref,
                   seg_ref, ya_ref, pb_ref, pc_ref, xc_ref, hc_ref, w_ref, stage_ref, sem_ref):
    wa, wb, wc = widths
    W = wa // 2
    tm = x_ref.shape[1]
    @pl.when(_is_first_step())
    def _():
        _load_weight_bf16(w_hbm, layer, w_ref, stage_ref, sem_ref)

    @pl.when(pl.program_id(1) == 0)
    def _():
        xc_ref[...] = jnp.zeros_like(xc_ref)
        hc_ref[...] = jnp.zeros_like(hc_ref)

    x = x_ref[0]
    ms = jnp.mean(x * x, axis=-1, keepdims=True)
    h = x * lax.rsqrt(ms + NORM_EPS) * g_ref[...]
    h = h * (1.0 + mod_ref[0, 1:2, :]) + mod_ref[0, 0:1, :]
    hb = h.astype(BF16)

    conv_b = vec_ref[0:1, :]
    rg_b = vec_ref[1:2, :]
    ig_b = vec_ref[2:3, :]
    sp_lam = _softplus(-vec_ref[3:4, :])
    beta = vec_ref[4:5, :]
    row = lax.broadcasted_iota(jnp.int32, (tm, W), 0)

    pa = jnp.dot(hb, w_ref[:, 0:wa], preferred_element_type=F32)
    xa = pa[:, 0:W]
    ya = pa[:, W:wa]
    win = jnp.concatenate([xc_ref[...], xa], axis=0)
    xc_ref[...] = xa[tm - SUBLANES:tm, :]
    u = conv_b + cw_ref[CONV_W - 1:CONV_W, :] * xa
    for j in range(1, CONV_W):
        u = u + cw_ref[CONV_W - 1 - j:CONV_W - j, :] * pltpu.roll(win, j, 0)[SUBLANES:]
    pb_ref[0] = jnp.dot(hb, w_ref[:, wa:wa + wb], preferred_element_type=F32)
    z = jnp.dot(u.astype(BF16), wg_ref[...], preferred_element_type=F32)
    pc_ref[0] = jnp.dot(hb, w_ref[:, wa + wb:wa + wb + wc], preferred_element_type=F32)

    r = _sigmoid(z[:, 0:W] + rg_b)
    ig = _sigmoid(z[:, W:2 * W] + ig_b)
    log_a = (-LRU_C) * r * sp_lam
    a = jnp.exp(log_a)
    mult = jnp.sqrt(_neg_expm1(2.0 * log_a, a * a))
    mult = jnp.where(row + pl.program_id(1) * tm == 0, 1.0, mult)
    bv = mult * (ig * u)
    s = 1
    while s < SUBLANES:
        keep = (row & (SUBLANES - 1)) >= s
        a_s = jnp.where(keep, pltpu.roll(a, s, 0), 1.0)
        b_s = jnp.where(keep, pltpu.roll(bv, s, 0), 0.0)
        bv = a * b_s + bv
        a = a * a_s
        s *= 2
    hprev = hc_ref[0:1, :]
    hs = []
    for j in range(tm // SUBLANES):
        blk = a[j * SUBLANES:(j + 1) * SUBLANES] * hprev + bv[j * SUBLANES:(j + 1) * SUBLANES]
        hs.append(blk)
        hprev = blk[SUBLANES - 1:SUBLANES, :]
    hc_ref[...] = jnp.broadcast_to(hprev, hc_ref.shape)
    y = jnp.concatenate(hs, axis=0) * jax.nn.gelu(ya)
    ms_a = _segsum(y * y, seg_ref[...]) * (1.0 / HEAD)
    ya_ref[0] = (y * lax.rsqrt(ms_a + NORM_EPS) * beta).astype(ya_ref.dtype)


def _in_projection(x, mod_l, gain, w_in, conv_w, vecs, wg_bf, g01, widths, layer, tm):
    B, T, D = x.shape
    P = w_in.shape[2]
    W = widths[0] // 2
    small = lambda a: pl.BlockSpec(a.shape, lambda b, t: (0, 0))
    return pl.pallas_call(
        functools.partial(_inproj_kernel, widths, layer),
        grid=(B, T // tm),
        in_specs=[
            pl.BlockSpec((1, tm, D), lambda b, t: (b, t, 0)),
            pl.BlockSpec((1, 6, D), lambda b, t: (b, 0, 0)),
            pl.BlockSpec((1, D), lambda b, t: (0, 0)),
            pl.BlockSpec(memory_space=pl.ANY),
            small(conv_w), small(vecs), small(wg_bf), small(g01),
        ],
        out_specs=[pl.BlockSpec((1, tm, W), lambda b, t: (b, t, 0)),
                   pl.BlockSpec((1, tm, widths[1]), lambda b, t: (b, t, 0)),
                   pl.BlockSpec((1, tm, widths[2]), lambda b, t: (b, t, 0))],
        out_shape=[jax.ShapeDtypeStruct((B, T, W), BF16),
                   jax.ShapeDtypeStruct((B, T, widths[1]), F32),
                   jax.ShapeDtypeStruct((B, T, widths[2]), F32)],
        scratch_shapes=[pltpu.VMEM((SUBLANES, W), F32), pltpu.VMEM((SUBLANES, W), F32),
                        pltpu.VMEM((D, P), BF16), pltpu.VMEM((W_SLOTS, W_CHUNK, P), F32),
                        pltpu.SemaphoreType.DMA((W_SLOTS,))],
        compiler_params=pltpu.CompilerParams(
            dimension_semantics=("arbitrary", "arbitrary"), vmem_limit_bytes=VMEM_LIMIT),
        name="in_projection",
    )(x, mod_l, gain, w_in, conv_w, vecs, wg_bf, g01)


def _drive(n_groups, group, elem_steps, mm_stages, scan_chunk):
    vals = []
    for _ in elem_steps(0, vals):
        yield
    ready = []
    for gi in range(n_groups):
        nxt = elem_steps(gi + 1, vals) if gi + 1 < n_groups else iter(())
        for si, _ in enumerate(mm_stages(gi, vals[gi])):
            next(nxt, None)
            if ready and si % SCAN_EVERY == SCAN_EVERY - 1:
                scan_chunk(ready.pop(0))
            yield
        for _ in nxt:
            yield
        ready += [gi * group + g for g in range(group)]
    for ci in ready:
        scan_chunk(ci)
        yield


def _hgrn_program(layer, pb_ref, lb_ref, beta_ref, wst_ref, g_ref, o_ref,
                  s_ref, qin_ref, oi_ref, kv_ref, dec_ref):
    C = CHUNK
    R = wst_ref.shape[1]
    G = R // C
    TB = pb_ref.shape[1]
    W = o_ref.shape[2]
    n_pairs = W // LANES

    @pl.when(pl.program_id(1) == 0)
    def _():
        s_ref[...] = jnp.zeros_like(s_ref)

    lbr = lb_ref[...]
    n_layers = lbr.shape[0]
    mx = lbr[0:1, :]
    for i in range(1, n_layers):
        mx = jnp.maximum(mx, lbr[i:i + 1, :])
    es = [jnp.exp(lbr[i:i + 1, :] - mx) for i in range(n_layers)]
    tot = es[0]
    for i in range(1, n_layers):
        tot = tot + es[i]
    sm = [e / tot for e in es]
    cum = sm[0]
    for i in range(1, layer + 1):
        cum = cum + sm[i]
    lb = cum - sm[0]
    one_m_lb = 1.0 - lb

    beta = beta_ref[...]
    g01 = g_ref[...]
    wst = wst_ref[...]
    lane = lax.broadcasted_iota(jnp.int32, (C, LANES), 1)
    m0 = lane < HEAD
    tw = lax.broadcasted_iota(jnp.int32, (C, 2 * C), 0)
    sw = lax.broadcasted_iota(jnp.int32, (C, 2 * C), 1) & (C - 1)
    eye = tw == sw
    level = []
    for lev in range(N_LEVELS):
        th = tw >> (N_LEVELS - 1 - lev)
        sh = sw >> (N_LEVELS - 1 - lev)
        level.append((th == sh + 1) & ((sh & 1) == 0))
    r2 = lax.broadcasted_iota(jnp.int32, (LANES, LANES), 0)
    c2 = lax.broadcasted_iota(jnp.int32, (LANES, LANES), 1)
    same_head = (r2 >> HEAD_BITS) == (c2 >> HEAD_BITS)

    odd_row = (lax.broadcasted_iota(jnp.int32, (C, LANES), 0) & 1) == 1
    n_kinds = wst_ref.shape[0] // R

    def elem_steps(gi, out):
        t0 = gi * R
        q = pb_ref[0, t0:t0 + R, 0:W] * (HEAD ** -0.5)
        fl = pb_ref[0, t0:t0 + R, W:2 * W]
        v = pb_ref[0, t0:t0 + R, 2 * W:3 * W]
        th = 0.5 * jnp.tanh(0.5 * fl)
        f = lb + one_m_lb * (0.5 + th)
        kg = one_m_lb * (0.5 - th)
        log_f = jnp.log(jnp.maximum(f, F_MIN))
        expo = _mm(wst, log_f, NN, na=1, nb=2)
        yield
        ex = jnp.exp(expo)
        qin_ref[t0:t0 + R, :] = q * ex[0:R]
        for g in range(G):
            dec_ref[gi * G + g] = jnp.broadcast_to(ex[(g + 1) * C - 1:(g + 1) * C, :], (8, W))
        out.append((q, kg, v, f, ex))

    def mm_stages(gi, vals):
        q, kg, v, f, ex = vals
        items = [(g, p) for g in range(G) for p in range(n_pairs)]
        n = range(len(items))

        def piece(x, g, p, base=0):
            return x[base + g * C:base + (g + 1) * C, p * LANES:(p + 1) * LANES]

        q_p = [piece(q, g, p) for g, p in items]
        v_p = [piece(v, g, p) for g, p in items]
        k_s = [_stack(piece(kg, g, p), m0) for g, p in items]
        q_b = [x.astype(BF16) for x in q_p]
        k_b = [x.astype(BF16) for x in k_s]
        amat = [jnp.where(eye, _mm(q_b[j], k_b[j], NT), 0.0) for j in n]
        yield
        for lev in range(N_LEVELS):
            if 2 + lev < n_kinds:
                es = [piece(ex, g, p, (2 + lev) * R) for g, p in items]
            else:
                es = [jnp.where(odd_row, piece(f, g, p), 1.0) for g, p in items]
            es = [e.astype(BF16) for e in es]
            amat = [jnp.where(level[lev],
                              _mm(q_b[j] * es[j],
                                  k_b[j] * jnp.concatenate([es[j], es[j]], axis=0), NT),
                              amat[j]) for j in n]
            yield
        oi = [_mm(amat[j], _stack(v_p[j], m0), NN, *P_HAPP) for j in n]
        yield
        k_dec = kg * ex[R:2 * R]
        kv = [_mm(v_p[j].T, piece(k_dec, g, p), NN, *P_HAPP) for j, (g, p) in enumerate(items)]
        for g in range(G):
            t0 = (gi * G + g) * C
            oi_ref[t0:t0 + C, :] = jnp.concatenate(
                [oi[g * n_pairs + p] for p in range(n_pairs)], axis=1)
        for j, (g, p) in enumerate(items):
            kv_ref[(gi * G + g) * n_pairs + p] = jnp.where(same_head, kv[j], 0.0)
        yield

    def scan_chunk(ci):
        t0 = ci * C
        dec = dec_ref[ci]
        outs = []
        for p in range(n_pairs):
            sl = slice(p * LANES, (p + 1) * LANES)
            st = s_ref[p]
            outs.append(_mm(qin_ref[t0:t0 + C, sl], st, NT, *P_HAPP))
            s_ref[p] = st * dec[0:1, sl] + kv_ref[ci * n_pairs + p]
        oi_ref[t0:t0 + C, :] = oi_ref[t0:t0 + C, :] + jnp.concatenate(outs, axis=1)

    yield from _drive(TB // R, G, elem_steps, mm_stages, scan_chunk)

    o = oi_ref[...]
    g = pb_ref[0, :, 3 * W:4 * W]
    ms = _segsum(o * o, g01) * (1.0 / HEAD)
    o = o * lax.rsqrt(ms + NORM_EPS) * (g * _sigmoid(g)) * beta
    o_ref[0] = o.astype(o_ref.dtype)


def _hgrn_weights(group):
    C = CHUNK
    R = group * C
    w = np.zeros(((2 + N_LEVELS - 1) * R, R), np.float32)
    for g in range(group):
        o = g * C
        for r in range(C):
            w[o + r, o:o + r + 1] = 1.0
            w[R + o + r, o + r + 1:o + C] = 1.0
            for lev in range(N_LEVELS - 1):
                n = C >> lev
                m = (r // n) * n + n // 2 - 1
                if r % n >= n // 2:
                    w[(2 + lev) * R + o + r, o + m + 1:o + r + 1] = 1.0
                else:
                    w[(2 + lev) * R + o + r, o + r + 1:o + m + 1] = 1.0
    return w


def _rwkv_program(pc_ref, mu_ref, vec_ref, lora_ref, gup_ref, tri_ref, g_ref, o_ref,
                  s_ref, prev_ref, rp_ref, y0_ref, mx_ref, nx_ref, bonus_ref, gate_ref, y_ref):
    C = CHUNK
    R = tri_ref.shape[0]
    G = R // C
    TB = pc_ref.shape[1]
    W = o_ref.shape[2]
    n_pairs = W // LANES

    @pl.when(pl.program_id(1) == 0)
    def _():
        s_ref[...] = jnp.zeros_like(s_ref)
        prev_ref[...] = jnp.zeros_like(prev_ref)

    mu = mu_ref[...]
    w0 = vec_ref[0:1, :]
    a0 = vec_ref[1:2, :]
    k_k = vec_ref[2:3, :]
    k_a = vec_ref[3:4, :]
    r_k = vec_ref[4:5, :]
    lnx_w = vec_ref[5:6, :]
    lnx_b = vec_ref[6:7, :]
    beta = vec_ref[7:8, :]
    lora = _split(lora_ref[...], 1)
    gup = _split(gup_ref[...], 1)
    tri = tri_ref[...]
    g01 = g_ref[...]
    lane = lax.broadcasted_iota(jnp.int32, (C, LANES), 1)
    lane_r = lax.broadcasted_iota(jnp.int32, (R, LANES), 1)
    m0 = lane < HEAD
    tw = lax.broadcasted_iota(jnp.int32, (C, 2 * C), 0)
    sw = lax.broadcasted_iota(jnp.int32, (C, 2 * C), 1) & (C - 1)
    strict = tw > sw
    incl = tw >= sw
    eye_w = jnp.where(tw == sw, 1.0, 0.0)
    r2 = lax.broadcasted_iota(jnp.int32, (2 * C, 2 * C), 0)
    c2 = lax.broadcasted_iota(jnp.int32, (2 * C, 2 * C), 1)
    same_head = (r2 >> HEAD_BITS) == (c2 >> HEAD_BITS)
    eye_f = jnp.where(r2 == c2, 1.0, 0.0)
    o_w = 3 * W
    o_g = o_w + LORA_W + LORA_A

    def mm_parts(x, parts):
        return jnp.dot(x.astype(BF16), parts[0], preferred_element_type=F32)

    def elem_steps(gi, out):
        t0 = gi * R
        pc = pc_ref[0, t0:t0 + R, :]
        prev8 = prev_ref[...] if gi == 0 else pc_ref[0, t0 - 8:t0, :]
        prev = pltpu.roll(jnp.concatenate([prev8, pc], axis=0), 1, 0)[8:]
        ps = pc + (prev - pc) * mu
        r = ps[:, 0:W]
        k = ps[:, W:2 * W]
        v = ps[:, 2 * W:3 * W]
        z = ps[:, o_w:o_g]
        z = jnp.where(lane_r < LORA_W, jnp.tanh(z), z)
        lo = mm_parts(z, lora)
        yield
        gate = mm_parts(_sigmoid(ps[:, o_g:o_g + LORA_G]), gup)
        gate_ref[t0:t0 + R, :] = gate * beta
        yield
        lw = (-math.exp(-0.5)) * _sigmoid(w0 + lo[:, 0:W])
        gam = _mm(tri, lw, NN, na=1, nb=2)
        yield
        a_sig = _sigmoid(a0 + lo[:, W:2 * W])
        kk = k * k_k
        kk = kk * lax.rsqrt(_segsum(kk * kk, g01) + 1e-12)
        yield
        k2 = k * (1.0 + (a_sig - 1.0) * k_a)
        bonus_ref[t0:t0 + R, :] = _segsum(r * k2 * r_k, g01) * v
        yield
        e_in = jnp.exp(gam)
        e_ex = jnp.exp(gam - lw)
        e_neg = jnp.exp(-gam)
        out.append((-kk * e_ex, r * e_in, kk * a_sig * e_neg, k2 * e_neg, v, e_in))

    def mm_stages(gi, vals):
        at, rt, bt, kt, v, e_in = vals
        items = [(g, p) for g in range(G) for p in range(n_pairs)]
        n = range(len(items))

        def pieces(x):
            return [x[g * C:(g + 1) * C, p * LANES:(p + 1) * LANES] for g, p in items]

        def st(x):
            return _stack(x, m0)

        a_p, r_p, b_p, k_p, v_p = pieces(at), pieces(rt), pieces(bt), pieces(kt), pieces(v)
        v_s = [st(v_p[j]) for j in n]
        att = [_mm(jnp.concatenate([a_p[j], r_p[j]], axis=0),
                   jnp.concatenate([st(b_p[j]), st(k_p[j])], axis=0), NT, *P_ATT)
               for j in n]
        yield
        lmat = [jnp.where(strict, att[j][0:C, 0:2 * C], 0.0) for j in n]
        ak = [jnp.where(strict, att[j][0:C, 2 * C:4 * C], 0.0) for j in n]
        rb = [jnp.where(incl, att[j][C:2 * C, 0:2 * C], 0.0) for j in n]
        rk = [jnp.where(incl, att[j][C:2 * C, 2 * C:4 * C], 0.0) for j in n]
        akv = [_mm(ak[j], v_s[j], NN, *P_APP) for j in n]
        yield
        pw = [_mm(lmat[j], st(lmat[j]), NN, *P_INV) for j in n]
        xs = [eye_w + lmat[j] for j in n]
        yield
        for _ in range(N_LEVELS - 2):
            z = [_mm(pw[j], jnp.concatenate([st(xs[j]), st(pw[j])], axis=1), NN, *P_INV)
                 for j in n]
            xs = [xs[j] + z[j][:, 0:2 * C] for j in n]
            pw = [z[j][:, 2 * C:4 * C] for j in n]
            yield
        xs = [xs[j] + _mm(pw[j], st(xs[j]), NN, *P_INV) for j in n]
        yield
        au = [_mm(xs[j], jnp.concatenate([st(a_p[j]), st(akv[j])], axis=1), NN, *P_APP)
              for j in n]
        ap = [au[j][:, 0:LANES] for j in n]
        u0 = [au[j][:, LANES:2 * LANES] for j in n]
        yield
        ry = [_mm(rb[j], jnp.concatenate([st(ap[j]), st(u0[j])], axis=1), NN, *P_APP)
              for j in n]
        rkv = [_mm(rk[j], v_s[j], NN, *P_APP) for j in n]
        yield
        mx = [_mm(ap[j].T, b_p[j], NN, *P_APP) for j in n]
        nx = [_mm(jnp.concatenate([u0[j], v_p[j]], axis=0).T,
                  jnp.concatenate([b_p[j], k_p[j]], axis=0), NN, *P_APP) for j in n]
        for j, (g, p) in enumerate(items):
            idx = (gi * G + g) * n_pairs + p
            glast = e_in[(g + 1) * C - 1:(g + 1) * C, p * LANES:(p + 1) * LANES]
            rp_ref[idx] = r_p[j] + ry[j][:, 0:LANES]
            y0_ref[idx] = ry[j][:, LANES:2 * LANES] + rkv[j]
            mx_ref[idx] = (eye_f + jnp.where(same_head, mx[j], 0.0)) * glast
            nx_ref[idx] = jnp.where(same_head, nx[j], 0.0) * glast
        yield

    def scan_chunk(ci):
        pairs = range(n_pairs)
        sts = [s_ref[p] for p in pairs]
        ys = [_mm(rp_ref[ci * n_pairs + p], sts[p], NT, *P_STATE) for p in pairs]
        sn = [_mm(sts[p], mx_ref[ci * n_pairs + p], NN, *P_STATE) for p in pairs]
        for p in pairs:
            s_ref[p] = sn[p] + nx_ref[ci * n_pairs + p]
        y_ref[ci * C:(ci + 1) * C, :] = jnp.concatenate(
            [ys[p] + y0_ref[ci * n_pairs + p] for p in pairs], axis=1)

    yield from _drive(TB // R, G, elem_steps, mm_stages, scan_chunk)
    prev_ref[...] = pc_ref[0, TB - 8:TB, :]

    y = y_ref[...]
    mean = _segsum(y, g01) * (1.0 / HEAD)
    d = y - mean
    var = _segsum(d * d, g01) * (1.0 / HEAD)
    yn = d * lax.rsqrt(var + GN_EPS) * lnx_w + lnx_b
    o_ref[0] = ((yn + bonus_ref[...]) * gate_ref[...]).astype(o_ref.dtype)


MIX_TICKS = (1, 1)
N_HGRN_IN, N_HGRN_SCRATCH = 5, 5
N_RWKV_IN, N_RWKV_SCRATCH = 7, 9


def _mixers_kernel(layer, *refs):
    it = iter(refs)
    take = lambda k: [next(it) for _ in range(k)]
    h_in, r_in = take(N_HGRN_IN), take(N_RWKV_IN)
    h_out, r_out = take(1), take(1)
    h_scr, r_scr = take(N_HGRN_SCRATCH), take(N_RWKV_SCRATCH)
    programs = [(_hgrn_program(layer, *h_in, *h_out, *h_scr), MIX_TICKS[0]),
                (_rwkv_program(*r_in, *r_out, *r_scr), MIX_TICKS[1])]
    while programs:
        for entry in list(programs):
            prog, ticks = entry
            for _ in range(ticks):
                if next(prog, programs) is programs:
                    programs.remove(entry)
                    break


def _mixers(pb, pc, hgrn_lb, beta_b, wst, mu, vecs, lora_w, g_up, tri, g01, layer, tb):
    B, T, W4 = pb.shape
    W = W4 // 4
    PC = pc.shape[2]
    n_mats = (tb // CHUNK) * (W // LANES)
    mats = pltpu.VMEM((n_mats, LANES, LANES), F32)
    rows = pltpu.VMEM((n_mats, CHUNK, LANES), F32)
    block = pltpu.VMEM((tb, W), F32)
    state = pltpu.VMEM((W // LANES, LANES, LANES), F32)
    small = lambda a: pl.BlockSpec(a.shape, lambda b, t: (0, 0))
    out = pl.BlockSpec((1, tb, W), lambda b, t: (b, t, 0))
    return pl.pallas_call(
        functools.partial(_mixers_kernel, layer),
        grid=(B, T // tb),
        in_specs=[
            pl.BlockSpec((1, tb, W4), lambda b, t: (b, t, 0)),
            small(hgrn_lb), small(beta_b), small(wst), small(g01),
            pl.BlockSpec((1, tb, PC), lambda b, t: (b, t, 0)),
            small(mu), small(vecs), small(lora_w), small(g_up), small(tri), small(g01),
        ],
        out_specs=[out, out],
        out_shape=[jax.ShapeDtypeStruct((B, T, W), BF16)] * 2,
        scratch_shapes=[state, block, block, mats, pltpu.VMEM((tb // CHUNK, 8, W), F32),
                        state, pltpu.VMEM((8, PC), F32), rows, rows, mats, mats,
                        block, block, block],
        compiler_params=pltpu.CompilerParams(
            dimension_semantics=("parallel", "arbitrary"), vmem_limit_bytes=VMEM_LIMIT),
        name="mixers",
    )(pb, hgrn_lb, beta_b, wst, g01, pc, mu, vecs, lora_w, g_up, tri, g01)


def _ffn_kernel(final, layer, widths, x_ref, ya_ref, yb_ref, yc_ref, mod_ref, g_ref, wo_hbm,
                wg_hbm, wu_hbm, wd_hbm, fg_ref, o_ref,
                wo_ref, wg_ref, wu_ref, wd_ref, stage_ref, sem_ref):
    @pl.when(_is_first_step())
    def _():
        for src, dst in ((wo_hbm, wo_ref), (wg_hbm, wg_ref), (wu_hbm, wu_ref),
                         (wd_hbm, wd_ref)):
            _load_weight_bf16(src, layer, dst, stage_ref, sem_ref)

    x = x_ref[0]
    acc = None
    off = 0
    for ref, wd in zip((ya_ref, yb_ref, yc_ref), widths):
        d = jnp.dot(ref[0], wo_ref[off:off + wd, :], preferred_element_type=F32)
        acc = d if acc is None else acc + d
        off += wd
    x1 = x + mod_ref[0, 2:3, :] * acc
    ms = jnp.mean(x1 * x1, axis=-1, keepdims=True)
    h = x1 * lax.rsqrt(ms + NORM_EPS) * g_ref[...]
    h = h * (1.0 + mod_ref[0, 4:5, :]) + mod_ref[0, 3:4, :]
    hb = h.astype(BF16)
    gt = jnp.dot(hb, wg_ref[...], preferred_element_type=F32)
    up = jnp.dot(hb, wu_ref[...], preferred_element_type=F32)
    act = (gt * _sigmoid(gt) * up).astype(BF16)
    dn = jnp.dot(act, wd_ref[...], preferred_element_type=F32)
    x2 = x1 + mod_ref[0, 5:6, :] * dn
    if final:
        ms2 = jnp.mean(x2 * x2, axis=-1, keepdims=True)
        x2 = x2 * lax.rsqrt(ms2 + NORM_EPS) * fg_ref[...]
    o_ref[0] = x2


def _out_ffn(x, ya, yb, yc, mod_l, gain, wo, wg, wu, wd, final_g, layer, tm):
    B, T, D = x.shape
    L, _, FF = wg.shape
    widths = (ya.shape[2], yb.shape[2], yc.shape[2])
    hbm = pl.BlockSpec(memory_space=pl.ANY)
    return pl.pallas_call(
        functools.partial(_ffn_kernel, layer == L - 1, layer, widths),
        grid=(B, T // tm),
        in_specs=[
            pl.BlockSpec((1, tm, D), lambda b, t: (b, t, 0)),
            pl.BlockSpec((1, tm, widths[0]), lambda b, t: (b, t, 0)),
            pl.BlockSpec((1, tm, widths[1]), lambda b, t: (b, t, 0)),
            pl.BlockSpec((1, tm, widths[2]), lambda b, t: (b, t, 0)),
            pl.BlockSpec((1, 6, D), lambda b, t: (b, 0, 0)),
            pl.BlockSpec((1, D), lambda b, t: (0, 0)),
            hbm, hbm, hbm, hbm,
            pl.BlockSpec((1, D), lambda b, t: (0, 0)),
        ],
        out_specs=pl.BlockSpec((1, tm, D), lambda b, t: (b, t, 0)),
        out_shape=jax.ShapeDtypeStruct((B, T, D), F32),
        scratch_shapes=[pltpu.VMEM((D, D), BF16), pltpu.VMEM((D, FF), BF16),
                        pltpu.VMEM((D, FF), BF16), pltpu.VMEM((FF, D), BF16),
                        pltpu.VMEM((W_SLOTS, W_CHUNK, max(D, FF)), F32),
                        pltpu.SemaphoreType.DMA((W_SLOTS,))],
        compiler_params=pltpu.CompilerParams(
            dimension_semantics=("arbitrary", "arbitrary"), vmem_limit_bytes=VMEM_LIMIT),
        name="out_ffn",
    )(x, ya, yb, yc, mod_l, gain, wo, wg, wu, wd, final_g)


def _block_diag(w):
    G, n, _ = w.shape
    eye = jnp.eye(G, dtype=w.dtype)
    return (eye[:, None, :, None] * w[:, :, None, :]).reshape(G * n, G * n)


def _seg_ones():
    idx = np.arange(LANES) // HEAD
    return jnp.asarray((idx[:, None] == idx[None, :]).astype(np.float32), dtype=BF16)


def kernel(x, c, norm1_g, norm2_g, ada_w, ada_b, w_in, conv_w, conv_b, rg_w, rg_b, ig_w, ig_b, lru_lam, hgrn_lb, rwkv_mu, rwkv_w0, rwkv_w_up, rwkv_a0, rwkv_a_up, rwkv_g_up, rwkv_k_k, rwkv_k_a, rwkv_r_k, rwkv_lnx_w, rwkv_lnx_b, mix_beta, w_out, ffn_w_gate, ffn_w_up, ffn_w_down, final_g):
    B, T, D = x.shape
    L = w_in.shape[0]
    WA = conv_w.shape[2]
    WB = hgrn_lb.shape[1]
    WC = rwkv_w0.shape[1]
    widths_p = (2 * WA, 4 * WB, rwkv_mu.shape[1])

    mod = _modulation(c, ada_w, ada_b).reshape(L, B, 6, D)
    wst = jnp.asarray(_hgrn_weights(HGRN_GROUP), dtype=BF16)
    tri = jnp.asarray(np.kron(np.eye(RWKV_GROUP, dtype=np.float32),
                              np.tril(np.ones((CHUNK, CHUNK), np.float32))), dtype=BF16)
    g_a = _seg_ones()

    for l in range(L):
        beta = mix_beta[l]
        vec_a = jnp.stack([conv_b[l], rg_b[l], ig_b[l], lru_lam[l], beta[:WA],
                           jnp.zeros_like(beta[:WA]), jnp.zeros_like(beta[:WA]),
                           jnp.zeros_like(beta[:WA])])
        wg_a = jnp.concatenate([_block_diag(rg_w[l]), _block_diag(ig_w[l])], axis=1).astype(BF16)
        ya, pb, pc = _in_projection(x, mod[l], norm1_g[l][None], w_in, conv_w[l], vec_a, wg_a,
                                    g_a, widths_p, l, IN_ROWS)
        vec_c = jnp.stack([rwkv_w0[l], rwkv_a0[l], rwkv_k_k[l], rwkv_k_a[l],
                           rwkv_r_k[l].reshape(-1), rwkv_lnx_w[l], rwkv_lnx_b[l],
                           beta[WA + WB:]])
        zw = jnp.zeros_like(rwkv_w_up[l])
        lora_w = jnp.concatenate(
            [jnp.concatenate([rwkv_w_up[l], zw], axis=1),
             jnp.concatenate([zw, rwkv_a_up[l]], axis=1)], axis=0)
        yb, yc = _mixers(pb, pc, hgrn_lb, beta[None, WA:WA + WB], wst, rwkv_mu[l][None], vec_c,
                         lora_w, rwkv_g_up[l], tri, g_a, l, MIXER_ROWS)
        x = _out_ffn(x, ya, yb, yc, mod[l], norm2_g[l][None], w_out, ffn_w_gate, ffn_w_up,
                     ffn_w_down, final_g[None], l, FFN_ROWS)
    return x
```

```python
import functools
import math

import numpy as np
import jax
import jax.numpy as jnp
from jax import lax
from jax.experimental import pallas as pl
from jax.experimental.pallas import tpu as pltpu

F32 = jnp.float32
BF16 = jnp.bfloat16

HEAD = 64
HEAD_BITS = 6
LANES = 128
SUBLANES = 8
VMEM_LIMIT = 56 * 1024 * 1024

CONV_W = 4
LRU_C = 8.0
NORM_EPS = 1e-6
GN_EPS = 64e-5
LORA_W = 64
LORA_A = 64
LORA_G = 128
F_MIN = 1e-37

IN_ROWS = 512
FFN_ROWS = 512
MIXER_ROWS = 1024
CHUNK = 64
N_LEVELS = 6
HGRN_GROUP = 2
RWKV_GROUP = 4
SCAN_EVERY = 2
W_CHUNK = 256
W_SLOTS = 4

NN = (((1,), (0,)), ((), ()))
NT = (((1,), (1,)), ((), ()))


def _split(x, n):
    if x.dtype == BF16:
        return [x]
    parts = []
    r = x
    for i in range(n):
        p = r.astype(BF16)
        parts.append(p)
        if i + 1 < n:
            r = r - p.astype(F32)
    return parts


def _mm(a, b, dims=NN, na=1, nb=1):
    pa = _split(a, na)
    pb = _split(b, nb)
    order = max(len(pa), len(pb))
    acc = None
    for i, x in enumerate(pa):
        for j, y in enumerate(pb):
            if i + j < order:
                d = lax.dot_general(x, y, dims, preferred_element_type=F32)
                acc = d if acc is None else acc + d
    return acc


def _segsum(x, g01):
    xb = x.astype(BF16)
    return jnp.concatenate(
        [jnp.dot(xb[:, o:o + LANES], g01, preferred_element_type=F32)
         for o in range(0, x.shape[1], LANES)], axis=1)


def _stack(x, m0):
    return jnp.concatenate([jnp.where(m0, x, 0.0), jnp.where(m0, 0.0, x)], axis=0)


def _sigmoid(x):
    return 0.5 * jnp.tanh(0.5 * x) + 0.5


def _log1p_exp_neg_abs(x):
    return jnp.log(1.0 + jnp.exp(-jnp.abs(x)))


def _neg_expm1(x, ex):
    return -jnp.tanh(0.5 * x) * (ex + 1.0)


def _softplus(x):
    return jnp.maximum(x, 0.0) + _log1p_exp_neg_abs(x)


def _load_weight_bf16(w_hbm, layer, dst_ref, stage_ref, sem_ref):
    K, N = dst_ref.shape
    depth = stage_ref.shape[0]

    def copy(c):
        return pltpu.make_async_copy(
            w_hbm.at[layer, pl.ds(c * W_CHUNK, W_CHUNK), :],
            stage_ref.at[c % depth, :, pl.ds(0, N)],
            sem_ref.at[c % depth])

    n = K // W_CHUNK
    for c in range(min(depth, n)):
        copy(c).start()
    for c in range(n):
        copy(c).wait()
        dst_ref[c * W_CHUNK:(c + 1) * W_CHUNK, :] = stage_ref[c % depth, :, 0:N].astype(BF16)
        if c + depth < n:
            copy(c + depth).start()


def _is_first_step():
    return (pl.program_id(0) == 0) & (pl.program_id(1) == 0)


def _mod_kernel(c_ref, w_ref, b_ref, o_ref):
    c = c_ref[...]
    cs = c * _sigmoid(c)
    o_ref[0] = _mm(cs, w_ref[0], NN, na=2, nb=1) + b_ref[0]


def _modulation(c, ada_w, ada_b):
    L, D, N = ada_w.shape
    B = c.shape[0]
    tn = N // 4
    return pl.pallas_call(
        _mod_kernel,
        grid=(L, N // tn),
        in_specs=[
            pl.BlockSpec((B, D), lambda l, j: (0, 0)),
            pl.BlockSpec((1, D, tn), lambda l, j: (l, 0, j)),
            pl.BlockSpec((1, 1, tn), lambda l, j: (l, 0, j)),
        ],
        out_specs=pl.BlockSpec((1, B, tn), lambda l, j: (l, 0, j)),
        out_shape=jax.ShapeDtypeStruct((L, B, N), F32),
        compiler_params=pltpu.CompilerParams(
            dimension_semantics=("parallel", "parallel"), vmem_limit_bytes=VMEM_LIMIT),
        name="modulation",
    )(c, ada_w, ada_b.reshape(L, 1, N))


def _inproj_kernel(widths, layer, x_ref, mod_ref, g_ref, w_hbm, cw_ref, vec_ref, wg_ref,
                   seg_ref, ya_ref, pb_ref, pc_ref, xc_ref, hc_ref, w_ref, stage_ref, sem_ref):
    wa, wb, wc = widths
    W = wa // 2
    tm = x_ref.shape[1]
    @pl.when(_is_first_step())
    def _():
        _load_weight_bf16(w_hbm, layer, w_ref, stage_ref, sem_ref)

    @pl.when(pl.program_id(1) == 0)
    def _():
        xc_ref[...] = jnp.zeros_like(xc_ref)
        hc_ref[...] = jnp.zeros_like(hc_ref)

    x = x_ref[0]
    ms = jnp.mean(x * x, axis=-1, keepdims=True)
    h = x * lax.rsqrt(ms + NORM_EPS) * g_ref[...]
    h = h * (1.0 + mod_ref[0, 1:2, :]) + mod_ref[0, 0:1, :]
    hb = h.astype(BF16)

    conv_b = vec_ref[0:1, :]
    rg_b = vec_ref[1:2, :]
    ig_b = vec_ref[2:3, :]
    sp_lam = _softplus(-vec_ref[3:4, :])
    beta = vec_ref[4:5, :]
    row = lax.broadcasted_iota(jnp.int32, (tm, W), 0)

    pa = jnp.dot(hb, w_ref[:, 0:wa], preferred_element_type=F32)
    xa = pa[:, 0:W]
    ya = pa[:, W:wa]
    win = jnp.concatenate([xc_ref[...], xa], axis=0)
    xc_ref[...] = xa[tm - SUBLANES:tm, :]
    u = conv_b + cw_ref[CONV_W - 1:CONV_W, :] * xa
    for j in range(1, CONV_W):
        u = u + cw_ref[CONV_W - 1 - j:CONV_W - j, :] * pltpu.roll(win, j, 0)[SUBLANES:]
    pb_ref[0] = jnp.dot(hb, w_ref[:, wa:wa + wb], preferred_element_type=F32)
    z = jnp.dot(u.astype(BF16), wg_ref[...], preferred_element_type=F32)
    pc_ref[0] = jnp.dot(hb, w_ref[:, wa + wb:wa + wb + wc], preferred_element_type=F32)

    r = _sigmoid(z[:, 0:W] + rg_b)
    ig = _sigmoid(z[:, W:2 * W] + ig_b)
    log_a = (-LRU_C) * r * sp_lam
    a = jnp.exp(log_a)
    mult = jnp.sqrt(_neg_expm1(2.0 * log_a, a * a))
    mult = jnp.where(row + pl.program_id(1) * tm == 0, 1.0, mult)
    bv = mult * (ig * u)
    s = 1
    while s < SUBLANES:
        keep = (row & (SUBLANES - 1)) >= s
        a_s = jnp.where(keep, pltpu.roll(a, s, 0), 1.0)
        b_s = jnp.where(keep, pltpu.roll(bv, s, 0), 0.0)
        bv = a * b_s + bv
        a = a * a_s
        s *= 2
    hprev = hc_ref[0:1, :]
    hs = []
    for j in range(tm // SUBLANES):
        blk = a[j * SUBLANES:(j + 1) * SUBLANES] * hprev + bv[j * SUBLANES:(j + 1) * SUBLANES]
        hs.append(blk)
        hprev = blk[SUBLANES - 1:SUBLANES, :]
    hc_ref[...] = jnp.broadcast_to(hprev, hc_ref.shape)
    y = jnp.concatenate(hs, axis=0) * jax.nn.gelu(ya)
    ms_a = _segsum(y * y, seg_ref[...]) * (1.0 / HEAD)
    ya_ref[0] = (y * lax.rsqrt(ms_a + NORM_EPS) * beta).astype(ya_ref.dtype)


def _in_projection(x, mod_l, gain, w_in, conv_w, vecs, wg_bf, g01, widths, layer, tm):
    B, T, D = x.shape
    P = w_in.shape[2]
    W = widths[0] // 2
    small = lambda a: pl.BlockSpec(a.shape, lambda b, t: (0, 0))
    return pl.pallas_call(
        functools.partial(_inproj_kernel, widths, layer),
        grid=(B, T // tm),
        in_specs=[
            pl.BlockSpec((1, tm, D), lambda b, t: (b, t, 0)),
            pl.BlockSpec((1, 6, D), lambda b, t: (b, 0, 0)),
            pl.BlockSpec((1, D), lambda b, t: (0, 0)),
            pl.BlockSpec(memory_space=pl.ANY),
            small(conv_w), small(vecs), small(wg_bf), small(g01),
        ],
        out_specs=[pl.BlockSpec((1, tm, W), lambda b, t: (b, t, 0)),
                   pl.BlockSpec((1, tm, widths[1]), lambda b, t: (b, t, 0)),
                   pl.BlockSpec((1, tm, widths[2]), lambda b, t: (b, t, 0))],
        out_shape=[jax.ShapeDtypeStruct((B, T, W), BF16),
                   jax.ShapeDtypeStruct((B, T, widths[1]), F32),
                   jax.ShapeDtypeStruct((B, T, widths[2]), F32)],
        scratch_shapes=[pltpu.VMEM((SUBLANES, W), F32), pltpu.VMEM((SUBLANES, W), F32),
                        pltpu.VMEM((D, P), BF16), pltpu.VMEM((W_SLOTS, W_CHUNK, P), F32),
                        pltpu.SemaphoreType.DMA((W_SLOTS,))],
        compiler_params=pltpu.CompilerParams(
            dimension_semantics=("arbitrary", "arbitrary"), vmem_limit_bytes=VMEM_LIMIT),
        name="in_projection",
    )(x, mod_l, gain, w_in, conv_w, vecs, wg_bf, g01)


def _drive(n_groups, group, elem_steps, mm_stages, scan_chunk):
    vals = []
    for _ in elem_steps(0, vals):
        yield
    ready = []
    for gi in range(n_groups):
        nxt = elem_steps(gi + 1, vals) if gi + 1 < n_groups else iter(())
        for si, _ in enumerate(mm_stages(gi, vals[gi])):
            next(nxt, None)
            if ready and si % SCAN_EVERY == SCAN_EVERY - 1:
                scan_chunk(ready.pop(0))
            yield
        for _ in nxt:
            yield
        ready += [gi * group + g for g in range(group)]
    for ci in ready:
        scan_chunk(ci)
        yield


def _hgrn_program(layer, pb_ref, lb_ref, beta_ref, wst_ref, g_ref, o_ref,
                  s_ref, qin_ref, oi_ref, kv_ref, dec_ref):
    C = CHUNK
    R = wst_ref.shape[1]
    G = R // C
    TB = pb_ref.shape[1]
    W = o_ref.shape[2]
    n_pairs = W // LANES

    @pl.when(pl.program_id(1) == 0)
    def _():
        s_ref[...] = jnp.zeros_like(s_ref)

    lbr = lb_ref[...]
    n_layers = lbr.shape[0]
    mx = lbr[0:1, :]
    for i in range(1, n_layers):
        mx = jnp.maximum(mx, lbr[i:i + 1, :])
    es = [jnp.exp(lbr[i:i + 1, :] - mx) for i in range(n_layers)]
    tot = es[0]
    for i in range(1, n_layers):
        tot = tot + es[i]
    sm = [e / tot for e in es]
    cum = sm[0]
    for i in range(1, layer + 1):
        cum = cum + sm[i]
    lb = cum - sm[0]
    one_m_lb = 1.0 - lb

    beta = beta_ref[...]
    g01 = g_ref[...]
    wst = wst_ref[...]
    lane = lax.broadcasted_iota(jnp.int32, (C, LANES), 1)
    m0 = lane < HEAD
    tw = lax.broadcasted_iota(jnp.int32, (C, 2 * C), 0)
    sw = lax.broadcasted_iota(jnp.int32, (C, 2 * C), 1) & (C - 1)
    eye = tw == sw
    level = []
    for lev in range(N_LEVELS):
        th = tw >> (N_LEVELS - 1 - lev)
        sh = sw >> (N_LEVELS - 1 - lev)
        level.append((th == sh + 1) & ((sh & 1) == 0))
    r2 = lax.broadcasted_iota(jnp.int32, (LANES, LANES), 0)
    c2 = lax.broadcasted_iota(jnp.int32, (LANES, LANES), 1)
    same_head = (r2 >> HEAD_BITS) == (c2 >> HEAD_BITS)

    odd_row = (lax.broadcasted_iota(jnp.int32, (C, LANES), 0) & 1) == 1
    n_kinds = wst_ref.shape[0] // R

    def elem_steps(gi, out):
        t0 = gi * R
        q = pb_ref[0, t0:t0 + R, 0:W] * (HEAD ** -0.5)
        fl = pb_ref[0, t0:t0 + R, W:2 * W]
        v = pb_ref[0, t0:t0 + R, 2 * W:3 * W]
        th = 0.5 * jnp.tanh(0.5 * fl)
        f = lb + one_m_lb * (0.5 + th)
        kg = one_m_lb * (0.5 - th)
        log_f = jnp.log(jnp.maximum(f, F_MIN))
        expo = _mm(wst, log_f, NN, na=1, nb=2)
        yield
        ex = jnp.exp(expo)
        qin_ref[t0:t0 + R, :] = q * ex[0:R]
        for g in range(G):
            dec_ref[gi * G + g] = jnp.broadcast_to(ex[(g + 1) * C - 1:(g + 1) * C, :], (8, W))
        out.append((q, kg, v, f, ex))

    def mm_stages(gi, vals):
        q, kg, v, f, ex = vals
        items = [(g, p) for g in range(G) for p in range(n_pairs)]
        n = range(len(items))

        def piece(x, g, p, base=0):
            return x[base + g * C:base + (g + 1) * C, p * LANES:(p + 1) * LANES]

        q_p = [piece(q, g, p) for g, p in items]
        v_p = [piece(v, g, p) for g, p in items]
        k_s = [_stack(piece(kg, g, p), m0) for g, p in items]
        q_b = [x.astype(BF16) for x in q_p]
        k_b = [x.astype(BF16) for x in k_s]
        amat = [jnp.where(eye, _mm(q_b[j], k_b[j], NT), 0.0) for j in n]
        yield
        for lev in range(N_LEVELS):
            if 2 + lev < n_kinds:
                es = [piece(ex, g, p, (2 + lev) * R) for g, p in items]
            else:
                es = [jnp.where(odd_row, piece(f, g, p), 1.0) for g, p in items]
            es = [e.astype(BF16) for e in es]
            amat = [jnp.where(level[lev],
                              _mm(q_b[j] * es[j],
                                  k_b[j] * jnp.concatenate([es[j], es[j]], axis=0), NT),
                              amat[j]) for j in n]
            yield
        oi = [_mm(amat[j], _stack(v_p[j], m0), NN) for j in n]
        yield
        k_dec = kg * ex[R:2 * R]
        kv = [_mm(v_p[j].T, piece(k_dec, g, p), NN) for j, (g, p) in enumerate(items)]
        for g in range(G):
            t0 = (gi * G + g) * C
            oi_ref[t0:t0 + C, :] = jnp.concatenate(
                [oi[g * n_pairs + p] for p in range(n_pairs)], axis=1)
        for j, (g, p) in enumerate(items):
            kv_ref[(gi * G + g) * n_pairs + p] = jnp.where(same_head, kv[j], 0.0)
        yield

    def scan_chunk(ci):
        t0 = ci * C
        dec = dec_ref[ci]
        outs = []
        for p in range(n_pairs):
            sl = slice(p * LANES, (p + 1) * LANES)
            st = s_ref[p]
            outs.append(_mm(qin_ref[t0:t0 + C, sl], st, NT))
            s_ref[p] = st * dec[0:1, sl] + kv_ref[ci * n_pairs + p]
        oi_ref[t0:t0 + C, :] = oi_ref[t0:t0 + C, :] + jnp.concatenate(outs, axis=1)

    yield from _drive(TB // R, G, elem_steps, mm_stages, scan_chunk)

    o = oi_ref[...]
    g = pb_ref[0, :, 3 * W:4 * W]
    ms = _segsum(o * o, g01) * (1.0 / HEAD)
    o = o * lax.rsqrt(ms + NORM_EPS) * (g * _sigmoid(g)) * beta
    o_ref[0] = o.astype(o_ref.dtype)


def _hgrn_weights(group):
    C = CHUNK
    R = group * C
    w = np.zeros(((2 + N_LEVELS - 1) * R, R), np.float32)
    for g in range(group):
        o = g * C
        for r in range(C):
            w[o + r, o:o + r + 1] = 1.0
            w[R + o + r, o + r + 1:o + C] = 1.0
            for lev in range(N_LEVELS - 1):
                n = C >> lev
                m = (r // n) * n + n // 2 - 1
                if r % n >= n // 2:
                    w[(2 + lev) * R + o + r, o + m + 1:o + r + 1] = 1.0
                else:
                    w[(2 + lev) * R + o + r, o + r + 1:o + m + 1] = 1.0
    return w


def _rwkv_program(pc_ref, mu_ref, vec_ref, lora_ref, gup_ref, tri_ref, g_ref, o_ref,
                  s_ref, prev_ref, rp_ref, y0_ref, mx_ref, nx_ref, bonus_ref, gate_ref, y_ref):
    C = CHUNK
    R = tri_ref.shape[0]
    G = R // C
    TB = pc_ref.shape[1]
    W = o_ref.shape[2]
    n_pairs = W // LANES

    @pl.when(pl.program_id(1) == 0)
    def _():
        s_ref[...] = jnp.zeros_like(s_ref)
        prev_ref[...] = jnp.zeros_like(prev_ref)

    mu = mu_ref[...]
    w0 = vec_ref[0:1, :]
    a0 = vec_ref[1:2, :]
    k_k = vec_ref[2:3, :]
    k_a = vec_ref[3:4, :]
    r_k = vec_ref[4:5, :]
    lnx_w = vec_ref[5:6, :]
    lnx_b = vec_ref[6:7, :]
    beta = vec_ref[7:8, :]
    lora = lora_ref[...].astype(BF16)
    gup = gup_ref[...].astype(BF16)
    tri = tri_ref[...]
    g01 = g_ref[...]
    lane = lax.broadcasted_iota(jnp.int32, (C, LANES), 1)
    lane_r = lax.broadcasted_iota(jnp.int32, (R, LANES), 1)
    m0 = lane < HEAD
    tw = lax.broadcasted_iota(jnp.int32, (C, 2 * C), 0)
    sw = lax.broadcasted_iota(jnp.int32, (C, 2 * C), 1) & (C - 1)
    strict = tw > sw
    incl = tw >= sw
    eye_w = jnp.where(tw == sw, 1.0, 0.0)
    r2 = lax.broadcasted_iota(jnp.int32, (2 * C, 2 * C), 0)
    c2 = lax.broadcasted_iota(jnp.int32, (2 * C, 2 * C), 1)
    same_head = (r2 >> HEAD_BITS) == (c2 >> HEAD_BITS)
    eye_f = jnp.where(r2 == c2, 1.0, 0.0)
    o_w = 3 * W
    o_g = o_w + LORA_W + LORA_A

    def elem_steps(gi, out):
        t0 = gi * R
        pc = pc_ref[0, t0:t0 + R, :]
        prev8 = prev_ref[...] if gi == 0 else pc_ref[0, t0 - 8:t0, :]
        prev = pltpu.roll(jnp.concatenate([prev8, pc], axis=0), 1, 0)[8:]
        ps = pc + (prev - pc) * mu
        r = ps[:, 0:W]
        k = ps[:, W:2 * W]
        v = ps[:, 2 * W:3 * W]
        z = ps[:, o_w:o_g]
        z = jnp.where(lane_r < LORA_W, jnp.tanh(z), z)
        lo = _mm(z, lora)
        yield
        gate = _mm(_sigmoid(ps[:, o_g:o_g + LORA_G]), gup)
        gate_ref[t0:t0 + R, :] = gate * beta
        yield
        lw = (-math.exp(-0.5)) * _sigmoid(w0 + lo[:, 0:W])
        gam = _mm(tri, lw, NN, na=1, nb=2)
        yield
        a_sig = _sigmoid(a0 + lo[:, W:2 * W])
        kk = k * k_k
        kk = kk * lax.rsqrt(_segsum(kk * kk, g01) + 1e-12)
        yield
        k2 = k * (1.0 + (a_sig - 1.0) * k_a)
        bonus_ref[t0:t0 + R, :] = _segsum(r * k2 * r_k, g01) * v
        yield
        e_in = jnp.exp(gam)
        e_ex = jnp.exp(gam - lw)
        e_neg = jnp.exp(-gam)
        out.append((-kk * e_ex, r * e_in, kk * a_sig * e_neg, k2 * e_neg, v, e_in))

    def mm_stages(gi, vals):
        at, rt, bt, kt, v, e_in = vals
        items = [(g, p) for g in range(G) for p in range(n_pairs)]
        n = range(len(items))

        def pieces(x):
            return [x[g * C:(g + 1) * C, p * LANES:(p + 1) * LANES] for g, p in items]

        def st(x):
            return _stack(x, m0)

        a_p, r_p, b_p, k_p, v_p = pieces(at), pieces(rt), pieces(bt), pieces(kt), pieces(v)
        v_s = [st(v_p[j]) for j in n]
        att = [_mm(jnp.concatenate([a_p[j], r_p[j]], axis=0),
                   jnp.concatenate([st(b_p[j]), st(k_p[j])], axis=0), NT)
               for j in n]
        yield
        lmat = [jnp.where(strict, att[j][0:C, 0:2 * C], 0.0) for j in n]
        ak = [jnp.where(strict, att[j][0:C, 2 * C:4 * C], 0.0) for j in n]
        rb = [jnp.where(incl, att[j][C:2 * C, 0:2 * C], 0.0) for j in n]
        rk = [jnp.where(incl, att[j][C:2 * C, 2 * C:4 * C], 0.0) for j in n]
        akv = [_mm(ak[j], v_s[j], NN) for j in n]
        yield
        pw = [_mm(lmat[j], st(lmat[j]), NN) for j in n]
        xs = [eye_w + lmat[j] for j in n]
        yield
        for _ in range(N_LEVELS - 2):
            z = [_mm(pw[j], jnp.concatenate([st(xs[j]), st(pw[j])], axis=1), NN)
                 for j in n]
            xs = [xs[j] + z[j][:, 0:2 * C] for j in n]
            pw = [z[j][:, 2 * C:4 * C] for j in n]
            yield
        xs = [xs[j] + _mm(pw[j], st(xs[j]), NN) for j in n]
        yield
        au = [_mm(xs[j], jnp.concatenate([st(a_p[j]), st(akv[j])], axis=1), NN)
              for j in n]
        ap = [au[j][:, 0:LANES] for j in n]
        u0 = [au[j][:, LANES:2 * LANES] for j in n]
        yield
        ry = [_mm(rb[j], jnp.concatenate([st(ap[j]), st(u0[j])], axis=1), NN)
              for j in n]
        rkv = [_mm(rk[j], v_s[j], NN) for j in n]
        yield
        mx = [_mm(ap[j].T, b_p[j], NN) for j in n]
        nx = [_mm(jnp.concatenate([u0[j], v_p[j]], axis=0).T,
                  jnp.concatenate([b_p[j], k_p[j]], axis=0), NN) for j in n]
        for j, (g, p) in enumerate(items):
            idx = (gi * G + g) * n_pairs + p
            glast = e_in[(g + 1) * C - 1:(g + 1) * C, p * LANES:(p + 1) * LANES]
            rp_ref[idx] = r_p[j] + ry[j][:, 0:LANES]
            y0_ref[idx] = ry[j][:, LANES:2 * LANES] + rkv[j]
            mx_ref[idx] = (eye_f + jnp.where(same_head, mx[j], 0.0)) * glast
            nx_ref[idx] = jnp.where(same_head, nx[j], 0.0) * glast
        yield

    def scan_chunk(ci):
        pairs = range(n_pairs)
        sts = [s_ref[p] for p in pairs]
        ys = [_mm(rp_ref[ci * n_pairs + p], sts[p], NT) for p in pairs]
        sn = [_mm(sts[p], mx_ref[ci * n_pairs + p], NN) for p in pairs]
        for p in pairs:
            s_ref[p] = sn[p] + nx_ref[ci * n_pairs + p]
        y_ref[ci * C:(ci + 1) * C, :] = jnp.concatenate(
            [ys[p] + y0_ref[ci * n_pairs + p] for p in pairs], axis=1)

    yield from _drive(TB // R, G, elem_steps, mm_stages, scan_chunk)
    prev_ref[...] = pc_ref[0, TB - 8:TB, :]

    y = y_ref[...]
    mean = _segsum(y, g01) * (1.0 / HEAD)
    d = y - mean
    var = _segsum(d * d, g01) * (1.0 / HEAD)
    yn = d * lax.rsqrt(var + GN_EPS) * lnx_w + lnx_b
    o_ref[0] = ((yn + bonus_ref[...]) * gate_ref[...]).astype(o_ref.dtype)


MIX_TICKS = (1, 1)
N_HGRN_IN, N_HGRN_SCRATCH = 5, 5
N_RWKV_IN, N_RWKV_SCRATCH = 7, 9


def _mixers_kernel(layer, *refs):
    it = iter(refs)
    take = lambda k: [next(it) for _ in range(k)]
    h_in, r_in = take(N_HGRN_IN), take(N_RWKV_IN)
    h_out, r_out = take(1), take(1)
    h_scr, r_scr = take(N_HGRN_SCRATCH), take(N_RWKV_SCRATCH)
    programs = [(_hgrn_program(layer, *h_in, *h_out, *h_scr), MIX_TICKS[0]),
                (_rwkv_program(*r_in, *r_out, *r_scr), MIX_TICKS[1])]
    while programs:
        for entry in list(programs):
            prog, ticks = entry
            for _ in range(ticks):
                if next(prog, programs) is programs:
                    programs.remove(entry)
                    break


def _mixers(pb, pc, hgrn_lb, beta_b, wst, mu, vecs, lora_w, g_up, tri, g01, layer, tb):
    B, T, W4 = pb.shape
    W = W4 // 4
    PC = pc.shape[2]
    n_mats = (tb // CHUNK) * (W // LANES)
    mats = pltpu.VMEM((n_mats, LANES, LANES), F32)
    rows = pltpu.VMEM((n_mats, CHUNK, LANES), F32)
    block = pltpu.VMEM((tb, W), F32)
    state = pltpu.VMEM((W // LANES, LANES, LANES), F32)
    small = lambda a: pl.BlockSpec(a.shape, lambda b, t: (0, 0))
    out = pl.BlockSpec((1, tb, W), lambda b, t: (b, t, 0))
    return pl.pallas_call(
        functools.partial(_mixers_kernel, layer),
        grid=(B, T // tb),
        in_specs=[
            pl.BlockSpec((1, tb, W4), lambda b, t: (b, t, 0)),
            small(hgrn_lb), small(beta_b), small(wst), small(g01),
            pl.BlockSpec((1, tb, PC), lambda b, t: (b, t, 0)),
            small(mu), small(vecs), small(lora_w), small(g_up), small(tri), small(g01),
        ],
        out_specs=[out, out],
        out_shape=[jax.ShapeDtypeStruct((B, T, W), BF16)] * 2,
        scratch_shapes=[state, block, block, mats, pltpu.VMEM((tb // CHUNK, 8, W), F32),
                        state, pltpu.VMEM((8, PC), F32), rows, rows, mats, mats,
                        block, block, block],
        compiler_params=pltpu.CompilerParams(
            dimension_semantics=("parallel", "arbitrary"), vmem_limit_bytes=VMEM_LIMIT),
        name="mixers",
    )(pb, hgrn_lb, beta_b, wst, g01, pc, mu, vecs, lora_w, g_up, tri, g01)


def _ffn_kernel(final, layer, widths, x_ref, ya_ref, yb_ref, yc_ref, mod_ref, g_ref, wo_hbm,
                wg_hbm, wu_hbm, wd_hbm, fg_ref, o_ref,
                wo_ref, wg_ref, wu_ref, wd_ref, stage_ref, sem_ref):
    @pl.when(_is_first_step())
    def _():
        for src, dst in ((wo_hbm, wo_ref), (wg_hbm, wg_ref), (wu_hbm, wu_ref),
                         (wd_hbm, wd_ref)):
            _load_weight_bf16(src, layer, dst, stage_ref, sem_ref)

    x = x_ref[0]
    acc = None
    off = 0
    for ref, wd in zip((ya_ref, yb_ref, yc_ref), widths):
        d = jnp.dot(ref[0], wo_ref[off:off + wd, :], preferred_element_type=F32)
        acc = d if acc is None else acc + d
        off += wd
    x1 = x + mod_ref[0, 2:3, :] * acc
    ms = jnp.mean(x1 * x1, axis=-1, keepdims=True)
    h = x1 * lax.rsqrt(ms + NORM_EPS) * g_ref[...]
    h = h * (1.0 + mod_ref[0, 4:5, :]) + mod_ref[0, 3:4, :]
    hb = h.astype(BF16)
    gt = jnp.dot(hb, wg_ref[...], preferred_element_type=F32)
    up = jnp.dot(hb, wu_ref[...], preferred_element_type=F32)
    act = (gt * _sigmoid(gt) * up).astype(BF16)
    dn = jnp.dot(act, wd_ref[...], preferred_element_type=F32)
    x2 = x1 + mod_ref[0, 5:6, :] * dn
    if final:
        ms2 = jnp.mean(x2 * x2, axis=-1, keepdims=True)
        x2 = x2 * lax.rsqrt(ms2 + NORM_EPS) * fg_ref[...]
    o_ref[0] = x2


def _out_ffn(x, ya, yb, yc, mod_l, gain, wo, wg, wu, wd, final_g, layer, tm):
    B, T, D = x.shape
    L, _, FF = wg.shape
    widths = (ya.shape[2], yb.shape[2], yc.shape[2])
    hbm = pl.BlockSpec(memory_space=pl.ANY)
    return pl.pallas_call(
        functools.partial(_ffn_kernel, layer == L - 1, layer, widths),
        grid=(B, T // tm),
        in_specs=[
            pl.BlockSpec((1, tm, D), lambda b, t: (b, t, 0)),
            pl.BlockSpec((1, tm, widths[0]), lambda b, t: (b, t, 0)),
            pl.BlockSpec((1, tm, widths[1]), lambda b, t: (b, t, 0)),
            pl.BlockSpec((1, tm, widths[2]), lambda b, t: (b, t, 0)),
            pl.BlockSpec((1, 6, D), lambda b, t: (b, 0, 0)),
            pl.BlockSpec((1, D), lambda b, t: (0, 0)),
            hbm, hbm, hbm, hbm,
            pl.BlockSpec((1, D), lambda b, t: (0, 0)),
        ],
        out_specs=pl.BlockSpec((1, tm, D), lambda b, t: (b, t, 0)),
        out_shape=jax.ShapeDtypeStruct((B, T, D), F32),
        scratch_shapes=[pltpu.VMEM((D, D), BF16), pltpu.VMEM((D, FF), BF16),
                        pltpu.VMEM((D, FF), BF16), pltpu.VMEM((FF, D), BF16),
                        pltpu.VMEM((W_SLOTS, W_CHUNK, max(D, FF)), F32),
                        pltpu.SemaphoreType.DMA((W_SLOTS,))],
        compiler_params=pltpu.CompilerParams(
            dimension_semantics=("arbitrary", "arbitrary"), vmem_limit_bytes=VMEM_LIMIT),
        name="out_ffn",
    )(x, ya, yb, yc, mod_l, gain, wo, wg, wu, wd, final_g)


def _block_diag(w):
    G, n, _ = w.shape
    eye = jnp.eye(G, dtype=w.dtype)
    return (eye[:, None, :, None] * w[:, :, None, :]).reshape(G * n, G * n)


def _seg_ones():
    idx = np.arange(LANES) // HEAD
    return jnp.asarray((idx[:, None] == idx[None, :]).astype(np.float32), dtype=BF16)


def kernel(x, c, norm1_g, norm2_g, ada_w, ada_b, w_in, conv_w, conv_b, rg_w, rg_b, ig_w, ig_b, lru_lam, hgrn_lb, rwkv_mu, rwkv_w0, rwkv_w_up, rwkv_a0, rwkv_a_up, rwkv_g_up, rwkv_k_k, rwkv_k_a, rwkv_r_k, rwkv_lnx_w, rwkv_lnx_b, mix_beta, w_out, ffn_w_gate, ffn_w_up, ffn_w_down, final_g):
    B, T, D = x.shape
    L = w_in.shape[0]
    WA = conv_w.shape[2]
    WB = hgrn_lb.shape[1]
    WC = rwkv_w0.shape[1]
    widths_p = (2 * WA, 4 * WB, rwkv_mu.shape[1])

    mod = _modulation(c, ada_w, ada_b).reshape(L, B, 6, D)
    wst = jnp.asarray(_hgrn_weights(HGRN_GROUP), dtype=BF16)
    tri = jnp.asarray(np.kron(np.eye(RWKV_GROUP, dtype=np.float32),
                              np.tril(np.ones((CHUNK, CHUNK), np.float32))), dtype=BF16)
    g_a = _seg_ones()

    for l in range(L):
        beta = mix_beta[l]
        vec_a = jnp.stack([conv_b[l], rg_b[l], ig_b[l], lru_lam[l], beta[:WA],
                           jnp.zeros_like(beta[:WA]), jnp.zeros_like(beta[:WA]),
                           jnp.zeros_like(beta[:WA])])
        wg_a = jnp.concatenate([_block_diag(rg_w[l]), _block_diag(ig_w[l])], axis=1).astype(BF16)
        ya, pb, pc = _in_projection(x, mod[l], norm1_g[l][None], w_in, conv_w[l], vec_a, wg_a,
                                    g_a, widths_p, l, IN_ROWS)
        vec_c = jnp.stack([rwkv_w0[l], rwkv_a0[l], rwkv_k_k[l], rwkv_k_a[l],
                           rwkv_r_k[l].reshape(-1), rwkv_lnx_w[l], rwkv_lnx_b[l],
                           beta[WA + WB:]])
        zw = jnp.zeros_like(rwkv_w_up[l])
        lora_w = jnp.concatenate(
            [jnp.concatenate([rwkv_w_up[l], zw], axis=1),
             jnp.concatenate([zw, rwkv_a_up[l]], axis=1)], axis=0)
        yb, yc = _mixers(pb, pc, hgrn_lb, beta[None, WA:WA + WB], wst, rwkv_mu[l][None], vec_c,
                         lora_w, rwkv_g_up[l], tri, g_a, l, MIXER_ROWS)
        x = _out_ffn(x, ya, yb, yc, mod[l], norm2_g[l][None], w_out, ffn_w_gate, ffn_w_up,
                     ffn_w_down, final_g[None], l, FFN_ROWS)
    return x
```

```python
import functools
import math

import numpy as np
import jax
import jax.numpy as jnp
from jax import lax
from jax.experimental import pallas as pl
from jax.experimental.pallas import tpu as pltpu

F32 = jnp.float32
BF16 = jnp.bfloat16

HEAD = 64
HEAD_BITS = 6
LANES = 128
SUBLANES = 8
VMEM_LIMIT = 56 * 1024 * 1024

CONV_W = 4
LRU_C = 8.0
NORM_EPS = 1e-6
GN_EPS = 64e-5
LORA_W = 64
LORA_A = 64
LORA_G = 128
F_MIN = 1e-37

IN_ROWS = 512
FFN_ROWS = 512
MIXER_ROWS = 1024
CHUNK = 64
N_LEVELS = 6
HGRN_GROUP = 2
RWKV_GROUP = 4
SCAN_EVERY = 2
W_CHUNK = 256
W_SLOTS = 4

NN = (((1,), (0,)), ((), ()))
NT = (((1,), (1,)), ((), ()))


def _split(x, n):
    if x.dtype == BF16:
        return [x]
    parts = []
    r = x
    for i in range(n):
        p = r.astype(BF16)
        parts.append(p)
        if i + 1 < n:
            r = r - p.astype(F32)
    return parts


def _mm(a, b, dims=NN, na=1, nb=1):
    pa = _split(a, na)
    pb = _split(b, nb)
    order = max(len(pa), len(pb))
    acc = None
    for i, x in enumerate(pa):
        for j, y in enumerate(pb):
            if i + j < order:
                d = lax.dot_general(x, y, dims, preferred_element_type=F32)
                acc = d if acc is None else acc + d
    return acc


def _segsum(x, g01):
    xb = x.astype(BF16)
    return jnp.concatenate(
        [jnp.dot(xb[:, o:o + LANES], g01, preferred_element_type=F32)
         for o in range(0, x.shape[1], LANES)], axis=1)


def _stack(x, m0):
    return jnp.concatenate([jnp.where(m0, x, 0.0), jnp.where(m0, 0.0, x)], axis=0)


def _sigmoid(x):
    return 0.5 * jnp.tanh(0.5 * x) + 0.5


def _log1p_exp_neg_abs(x):
    return jnp.log(1.0 + jnp.exp(-jnp.abs(x)))


def _neg_expm1(x, ex):
    return -jnp.tanh(0.5 * x) * (ex + 1.0)


def _softplus(x):
    return jnp.maximum(x, 0.0) + _log1p_exp_neg_abs(x)


def _load_weight_bf16(w_hbm, layer, dst_ref, stage_ref, sem_ref):
    K, N = dst_ref.shape
    depth = stage_ref.shape[0]

    def copy(c):
        return pltpu.make_async_copy(
            w_hbm.at[layer, pl.ds(c * W_CHUNK, W_CHUNK), :],
            stage_ref.at[c % depth, :, pl.ds(0, N)],
            sem_ref.at[c % depth])

    n = K // W_CHUNK
    for c in range(min(depth, n)):
        copy(c).start()
    for c in range(n):
        copy(c).wait()
        dst_ref[c * W_CHUNK:(c + 1) * W_CHUNK, :] = stage_ref[c % depth, :, 0:N].astype(BF16)
        if c + depth < n:
            copy(c + depth).start()


def _is_first_step():
    return (pl.program_id(0) == 0) & (pl.program_id(1) == 0)


def _mod_kernel(c_ref, w_ref, b_ref, o_ref):
    c = c_ref[...]
    cs = c * _sigmoid(c)
    o_ref[0] = _mm(cs, w_ref[0], NN, na=2, nb=1) + b_ref[0]


def _modulation(c, ada_w, ada_b):
    L, D, N = ada_w.shape
    B = c.shape[0]
    tn = N // 4
    return pl.pallas_call(
        _mod_kernel,
        grid=(L, N // tn),
        in_specs=[
            pl.BlockSpec((B, D), lambda l, j: (0, 0)),
            pl.BlockSpec((1, D, tn), lambda l, j: (l, 0, j)),
            pl.BlockSpec((1, 1, tn), lambda l, j: (l, 0, j)),
        ],
        out_specs=pl.BlockSpec((1, B, tn), lambda l, j: (l, 0, j)),
        out_shape=jax.ShapeDtypeStruct((L, B, N), F32),
        compiler_params=pltpu.CompilerParams(
            dimension_semantics=("parallel", "parallel"), vmem_limit_bytes=VMEM_LIMIT),
        name="modulation",
    )(c, ada_w, ada_b.reshape(L, 1, N))


def _inproj_kernel(widths, layer, x0_ref, mod0_ref, xn_ref, modn_ref, g_ref, w_hbm, cw_ref,
                   vec_ref, wg_ref, seg_ref, ya_ref, pb_ref, pc_ref, xc_ref, hc_ref, w_ref,
                   stage_ref, sem_ref, hb_ref):
    wa, wb, wc = widths
    W = wa // 2
    tm = xn_ref.shape[1]

    def normed(x, mod_ref):
        ms = jnp.mean(x * x, axis=-1, keepdims=True)
        h = x * lax.rsqrt(ms + NORM_EPS) * g_ref[...]
        return (h * (1.0 + mod_ref[0, 1:2, :]) + mod_ref[0, 0:1, :]).astype(BF16)

    @pl.when(_is_first_step())
    def _():
        _load_weight_bf16(w_hbm, layer, w_ref, stage_ref, sem_ref)
        hb_ref[...] = normed(x0_ref[0], mod0_ref)

    @pl.when(pl.program_id(1) == 0)
    def _():
        xc_ref[...] = jnp.zeros_like(xc_ref)
        hc_ref[...] = jnp.zeros_like(hc_ref)

    hb = hb_ref[...]

    conv_b = vec_ref[0:1, :]
    rg_b = vec_ref[1:2, :]
    ig_b = vec_ref[2:3, :]
    sp_lam = _softplus(-vec_ref[3:4, :])
    beta = vec_ref[4:5, :]
    row = lax.broadcasted_iota(jnp.int32, (tm, W), 0)

    pa = jnp.dot(hb, w_ref[:, 0:wa], preferred_element_type=F32)
    xa = pa[:, 0:W]
    ya = pa[:, W:wa]
    win = jnp.concatenate([xc_ref[...], xa], axis=0)
    xc_ref[...] = xa[tm - SUBLANES:tm, :]
    u = conv_b + cw_ref[CONV_W - 1:CONV_W, :] * xa
    for j in range(1, CONV_W):
        u = u + cw_ref[CONV_W - 1 - j:CONV_W - j, :] * pltpu.roll(win, j, 0)[SUBLANES:]
    pb_ref[0] = jnp.dot(hb, w_ref[:, wa:wa + wb], preferred_element_type=F32)
    hb_next = normed(xn_ref[0], modn_ref)
    z = jnp.dot(u.astype(BF16), wg_ref[...], preferred_element_type=F32)
    pc_ref[0] = jnp.dot(hb, w_ref[:, wa + wb:wa + wb + wc], preferred_element_type=F32)

    r = _sigmoid(z[:, 0:W] + rg_b)
    ig = _sigmoid(z[:, W:2 * W] + ig_b)
    log_a = (-LRU_C) * r * sp_lam
    a = jnp.exp(log_a)
    mult = jnp.sqrt(_neg_expm1(2.0 * log_a, a * a))
    mult = jnp.where(row + pl.program_id(1) * tm == 0, 1.0, mult)
    bv = mult * (ig * u)
    s = 1
    while s < SUBLANES:
        keep = (row & (SUBLANES - 1)) >= s
        a_s = jnp.where(keep, pltpu.roll(a, s, 0), 1.0)
        b_s = jnp.where(keep, pltpu.roll(bv, s, 0), 0.0)
        bv = a * b_s + bv
        a = a * a_s
        s *= 2
    hprev = hc_ref[0:1, :]
    hs = []
    for j in range(tm // SUBLANES):
        blk = a[j * SUBLANES:(j + 1) * SUBLANES] * hprev + bv[j * SUBLANES:(j + 1) * SUBLANES]
        hs.append(blk)
        hprev = blk[SUBLANES - 1:SUBLANES, :]
    hc_ref[...] = jnp.broadcast_to(hprev, hc_ref.shape)
    y = jnp.concatenate(hs, axis=0) * jax.nn.gelu(ya)
    ms_a = _segsum(y * y, seg_ref[...]) * (1.0 / HEAD)
    ya_ref[0] = (y * lax.rsqrt(ms_a + NORM_EPS) * beta).astype(ya_ref.dtype)
    hb_ref[...] = hb_next


def _in_projection(x, mod_l, gain, w_in, conv_w, vecs, wg_bf, g01, widths, layer, tm):
    B, T, D = x.shape
    P = w_in.shape[2]
    W = widths[0] // 2
    small = lambda a: pl.BlockSpec(a.shape, lambda b, t: (0, 0))
    nt = T // tm

    def next_b(b, t):
        return jnp.minimum((b * nt + t + 1) // nt, B - 1)

    def next_t(b, t):
        return (b * nt + t + 1) % nt

    return pl.pallas_call(
        functools.partial(_inproj_kernel, widths, layer),
        grid=(B, nt),
        in_specs=[
            pl.BlockSpec((1, tm, D), lambda b, t: (0, 0, 0)),
            pl.BlockSpec((1, 6, D), lambda b, t: (0, 0, 0)),
            pl.BlockSpec((1, tm, D), lambda b, t: (next_b(b, t), next_t(b, t), 0)),
            pl.BlockSpec((1, 6, D), lambda b, t: (next_b(b, t), 0, 0)),
            pl.BlockSpec((1, D), lambda b, t: (0, 0)),
            pl.BlockSpec(memory_space=pl.ANY),
            small(conv_w), small(vecs), small(wg_bf), small(g01),
        ],
        out_specs=[pl.BlockSpec((1, tm, W), lambda b, t: (b, t, 0)),
                   pl.BlockSpec((1, tm, widths[1]), lambda b, t: (b, t, 0)),
                   pl.BlockSpec((1, tm, widths[2]), lambda b, t: (b, t, 0))],
        out_shape=[jax.ShapeDtypeStruct((B, T, W), BF16),
                   jax.ShapeDtypeStruct((B, T, widths[1]), F32),
                   jax.ShapeDtypeStruct((B, T, widths[2]), F32)],
        scratch_shapes=[pltpu.VMEM((SUBLANES, W), F32), pltpu.VMEM((SUBLANES, W), F32),
                        pltpu.VMEM((D, P), BF16), pltpu.VMEM((W_SLOTS, W_CHUNK, P), F32),
                        pltpu.SemaphoreType.DMA((W_SLOTS,)), pltpu.VMEM((tm, D), BF16)],
        compiler_params=pltpu.CompilerParams(
            dimension_semantics=("arbitrary", "arbitrary"), vmem_limit_bytes=VMEM_LIMIT),
        name="in_projection",
    )(x, mod_l, x, mod_l, gain, w_in, conv_w, vecs, wg_bf, g01)


def _drive(n_groups, group, elem_steps, mm_stages, scan_chunk):
    vals = []
    for _ in elem_steps(0, vals):
        yield
    ready = []
    for gi in range(n_groups):
        nxt = elem_steps(gi + 1, vals) if gi + 1 < n_groups else iter(())
        for si, _ in enumerate(mm_stages(gi, vals[gi])):
            next(nxt, None)
            if ready and si % SCAN_EVERY == SCAN_EVERY - 1:
                scan_chunk(ready.pop(0))
            yield
        for _ in nxt:
            yield
        ready += [gi * group + g for g in range(group)]
    for ci in ready:
        scan_chunk(ci)
        yield


def _hgrn_program(layer, pb_ref, lb_ref, beta_ref, wst_ref, g_ref, o_ref,
                  s_ref, qin_ref, oi_ref, kv_ref, dec_ref):
    C = CHUNK
    R = wst_ref.shape[1]
    G = R // C
    TB = pb_ref.shape[1]
    W = o_ref.shape[2]
    n_pairs = W // LANES

    @pl.when(pl.program_id(1) == 0)
    def _():
        s_ref[...] = jnp.zeros_like(s_ref)

    lbr = lb_ref[...]
    n_layers = lbr.shape[0]
    mx = lbr[0:1, :]
    for i in range(1, n_layers):
        mx = jnp.maximum(mx, lbr[i:i + 1, :])
    es = [jnp.exp(lbr[i:i + 1, :] - mx) for i in range(n_layers)]
    tot = es[0]
    for i in range(1, n_layers):
        tot = tot + es[i]
    sm = [e / tot for e in es]
    cum = sm[0]
    for i in range(1, layer + 1):
        cum = cum + sm[i]
    lb = cum - sm[0]
    one_m_lb = 1.0 - lb

    beta = beta_ref[...]
    g01 = g_ref[...]
    wst = wst_ref[...]
    lane = lax.broadcasted_iota(jnp.int32, (C, LANES), 1)
    m0 = lane < HEAD
    tw = lax.broadcasted_iota(jnp.int32, (C, 2 * C), 0)
    sw = lax.broadcasted_iota(jnp.int32, (C, 2 * C), 1) & (C - 1)
    eye = tw == sw
    level = []
    for lev in range(N_LEVELS):
        th = tw >> (N_LEVELS - 1 - lev)
        sh = sw >> (N_LEVELS - 1 - lev)
        level.append((th == sh + 1) & ((sh & 1) == 0))
    r2 = lax.broadcasted_iota(jnp.int32, (LANES, LANES), 0)
    c2 = lax.broadcasted_iota(jnp.int32, (LANES, LANES), 1)
    same_head = (r2 >> HEAD_BITS) == (c2 >> HEAD_BITS)

    odd_row = (lax.broadcasted_iota(jnp.int32, (C, LANES), 0) & 1) == 1
    n_kinds = wst_ref.shape[0] // R

    def elem_steps(gi, out):
        t0 = gi * R
        q = pb_ref[0, t0:t0 + R, 0:W] * (HEAD ** -0.5)
        fl = pb_ref[0, t0:t0 + R, W:2 * W]
        v = pb_ref[0, t0:t0 + R, 2 * W:3 * W]
        th = 0.5 * jnp.tanh(0.5 * fl)
        f = lb + one_m_lb * (0.5 + th)
        kg = one_m_lb * (0.5 - th)
        log_f = jnp.log(jnp.maximum(f, F_MIN))
        expo = _mm(wst, log_f, NN, na=1, nb=2)
        yield
        ex = jnp.exp(expo)
        qin_ref[t0:t0 + R, :] = q * ex[0:R]
        for g in range(G):
            dec_ref[gi * G + g] = jnp.broadcast_to(ex[(g + 1) * C - 1:(g + 1) * C, :], (8, W))
        out.append((q, kg, v, f, ex))

    def mm_stages(gi, vals):
        q, kg, v, f, ex = vals
        items = [(g, p) for g in range(G) for p in range(n_pairs)]
        n = range(len(items))

        def piece(x, g, p, base=0):
            return x[base + g * C:base + (g + 1) * C, p * LANES:(p + 1) * LANES]

        q_p = [piece(q, g, p) for g, p in items]
        v_p = [piece(v, g, p) for g, p in items]
        k_s = [_stack(piece(kg, g, p), m0) for g, p in items]
        q_b = [x.astype(BF16) for x in q_p]
        k_b = [x.astype(BF16) for x in k_s]
        amat = [jnp.where(eye, _mm(q_b[j], k_b[j], NT), 0.0) for j in n]
        yield
        for lev in range(N_LEVELS):
            if 2 + lev < n_kinds:
                es = [piece(ex, g, p, (2 + lev) * R) for g, p in items]
            else:
                es = [jnp.where(odd_row, piece(f, g, p), 1.0) for g, p in items]
            es = [e.astype(BF16) for e in es]
            amat = [jnp.where(level[lev],
                              _mm(q_b[j] * es[j],
                                  k_b[j] * jnp.concatenate([es[j], es[j]], axis=0), NT),
                              amat[j]) for j in n]
            yield
        oi = [_mm(amat[j], _stack(v_p[j], m0), NN) for j in n]
        yield
        k_dec = kg * ex[R:2 * R]
        kv = [_mm(v_p[j].T, piece(k_dec, g, p), NN) for j, (g, p) in enumerate(items)]
        for g in range(G):
            t0 = (gi * G + g) * C
            oi_ref[t0:t0 + C, :] = jnp.concatenate(
                [oi[g * n_pairs + p] for p in range(n_pairs)], axis=1)
        for j, (g, p) in enumerate(items):
            kv_ref[(gi * G + g) * n_pairs + p] = jnp.where(same_head, kv[j], 0.0)
        yield

    def scan_chunk(ci):
        t0 = ci * C
        dec = dec_ref[ci]
        outs = []
        for p in range(n_pairs):
            sl = slice(p * LANES, (p + 1) * LANES)
            st = s_ref[p]
            outs.append(_mm(qin_ref[t0:t0 + C, sl], st, NT))
            s_ref[p] = st * dec[0:1, sl] + kv_ref[ci * n_pairs + p]
        oi_ref[t0:t0 + C, :] = oi_ref[t0:t0 + C, :] + jnp.concatenate(outs, axis=1)

    yield from _drive(TB // R, G, elem_steps, mm_stages, scan_chunk)

    o = oi_ref[...]
    g = pb_ref[0, :, 3 * W:4 * W]
    ms = _segsum(o * o, g01) * (1.0 / HEAD)
    o = o * lax.rsqrt(ms + NORM_EPS) * (g * _sigmoid(g)) * beta
    o_ref[0] = o.astype(o_ref.dtype)


def _hgrn_weights(group):
    C = CHUNK
    R = group * C
    w = np.zeros(((2 + N_LEVELS - 1) * R, R), np.float32)
    for g in range(group):
        o = g * C
        for r in range(C):
            w[o + r, o:o + r + 1] = 1.0
            w[R + o + r, o + r + 1:o + C] = 1.0
            for lev in range(N_LEVELS - 1):
                n = C >> lev
                m = (r // n) * n + n // 2 - 1
                if r % n >= n // 2:
                    w[(2 + lev) * R + o + r, o + m + 1:o + r + 1] = 1.0
                else:
                    w[(2 + lev) * R + o + r, o + r + 1:o + m + 1] = 1.0
    return w


def _rwkv_program(pc_ref, mu_ref, vec_ref, lora_ref, gup_ref, tri_ref, g_ref, o_ref,
                  s_ref, prev_ref, rp_ref, y0_ref, mx_ref, nx_ref, bonus_ref, gate_ref, y_ref):
    C = CHUNK
    R = tri_ref.shape[0]
    G = R // C
    TB = pc_ref.shape[1]
    W = o_ref.shape[2]
    n_pairs = W // LANES

    @pl.when(pl.program_id(1) == 0)
    def _():
        s_ref[...] = jnp.zeros_like(s_ref)
        prev_ref[...] = jnp.zeros_like(prev_ref)

    mu = mu_ref[...]
    w0 = vec_ref[0:1, :]
    a0 = vec_ref[1:2, :]
    k_k = vec_ref[2:3, :]
    k_a = vec_ref[3:4, :]
    r_k = vec_ref[4:5, :]
    lnx_w = vec_ref[5:6, :]
    lnx_b = vec_ref[6:7, :]
    beta = vec_ref[7:8, :]
    lora = lora_ref[...].astype(BF16)
    gup = gup_ref[...].astype(BF16)
    tri = tri_ref[...]
    g01 = g_ref[...]
    lane = lax.broadcasted_iota(jnp.int32, (C, LANES), 1)
    lane_r = lax.broadcasted_iota(jnp.int32, (R, LANES), 1)
    m0 = lane < HEAD
    tw = lax.broadcasted_iota(jnp.int32, (C, 2 * C), 0)
    sw = lax.broadcasted_iota(jnp.int32, (C, 2 * C), 1) & (C - 1)
    strict = tw > sw
    incl = tw >= sw
    eye_w = jnp.where(tw == sw, 1.0, 0.0)
    r2 = lax.broadcasted_iota(jnp.int32, (2 * C, 2 * C), 0)
    c2 = lax.broadcasted_iota(jnp.int32, (2 * C, 2 * C), 1)
    same_head = (r2 >> HEAD_BITS) == (c2 >> HEAD_BITS)
    eye_f = jnp.where(r2 == c2, 1.0, 0.0)
    o_w = 3 * W
    o_g = o_w + LORA_W + LORA_A

    def elem_steps(gi, out):
        t0 = gi * R
        pc = pc_ref[0, t0:t0 + R, :]
        prev8 = prev_ref[...] if gi == 0 else pc_ref[0, t0 - 8:t0, :]
        prev = pltpu.roll(jnp.concatenate([prev8, pc], axis=0), 1, 0)[8:]
        ps = pc + (prev - pc) * mu
        r = ps[:, 0:W]
        k = ps[:, W:2 * W]
        v = ps[:, 2 * W:3 * W]
        z = ps[:, o_w:o_g]
        z = jnp.where(lane_r < LORA_W, jnp.tanh(z), z)
        lo = _mm(z, lora)
        yield
        gate = _mm(_sigmoid(ps[:, o_g:o_g + LORA_G]), gup)
        gate_ref[t0:t0 + R, :] = gate * beta
        yield
        lw = (-math.exp(-0.5)) * _sigmoid(w0 + lo[:, 0:W])
        gam = _mm(tri, lw, NN, na=1, nb=2)
        yield
        a_sig = _sigmoid(a0 + lo[:, W:2 * W])
        kk = k * k_k
        kk = kk * lax.rsqrt(_segsum(kk * kk, g01) + 1e-12)
        yield
        k2 = k * (1.0 + (a_sig - 1.0) * k_a)
        bonus_ref[t0:t0 + R, :] = _segsum(r * k2 * r_k, g01) * v
        yield
        e_in = jnp.exp(gam)
        e_ex = jnp.exp(gam - lw)
        e_neg = jnp.exp(-gam)
        out.append((-kk * e_ex, r * e_in, kk * a_sig * e_neg, k2 * e_neg, v, e_in))

    def mm_stages(gi, vals):
        at, rt, bt, kt, v, e_in = vals
        items = [(g, p) for g in range(G) for p in range(n_pairs)]
        n = range(len(items))

        def pieces(x):
            return [x[g * C:(g + 1) * C, p * LANES:(p + 1) * LANES] for g, p in items]

        def st(x):
            return _stack(x, m0)

        a_p, r_p, b_p, k_p, v_p = pieces(at), pieces(rt), pieces(bt), pieces(kt), pieces(v)
        v_s = [st(v_p[j]) for j in n]
        att = [_mm(jnp.concatenate([a_p[j], r_p[j]], axis=0),
                   jnp.concatenate([st(b_p[j]), st(k_p[j])], axis=0), NT)
               for j in n]
        yield
        lmat = [jnp.where(strict, att[j][0:C, 0:2 * C], 0.0) for j in n]
        ak = [jnp.where(strict, att[j][0:C, 2 * C:4 * C], 0.0) for j in n]
        rb = [jnp.where(incl, att[j][C:2 * C, 0:2 * C], 0.0) for j in n]
        rk = [jnp.where(incl, att[j][C:2 * C, 2 * C:4 * C], 0.0) for j in n]
        akv = [_mm(ak[j], v_s[j], NN) for j in n]
        yield
        pw = [_mm(lmat[j], st(lmat[j]), NN) for j in n]
        xs = [eye_w + lmat[j] for j in n]
        yield
        for _ in range(N_LEVELS - 2):
            z = [_mm(pw[j], jnp.concatenate([st(xs[j]), st(pw[j])], axis=1), NN)
                 for j in n]
            xs = [xs[j] + z[j][:, 0:2 * C] for j in n]
            pw = [z[j][:, 2 * C:4 * C] for j in n]
            yield
        xs = [xs[j] + _mm(pw[j], st(xs[j]), NN) for j in n]
        yield
        au = [_mm(xs[j], jnp.concatenate([st(a_p[j]), st(akv[j])], axis=1), NN)
              for j in n]
        ap = [au[j][:, 0:LANES] for j in n]
        u0 = [au[j][:, LANES:2 * LANES] for j in n]
        yield
        ry = [_mm(rb[j], jnp.concatenate([st(ap[j]), st(u0[j])], axis=1), NN)
              for j in n]
        rkv = [_mm(rk[j], v_s[j], NN) for j in n]
        yield
        mx = [_mm(ap[j].T, b_p[j], NN) for j in n]
        nx = [_mm(jnp.concatenate([u0[j], v_p[j]], axis=0).T,
                  jnp.concatenate([b_p[j], k_p[j]], axis=0), NN) for j in n]
        for j, (g, p) in enumerate(items):
            idx = (gi * G + g) * n_pairs + p
            glast = e_in[(g + 1) * C - 1:(g + 1) * C, p * LANES:(p + 1) * LANES]
            rp_ref[idx] = r_p[j] + ry[j][:, 0:LANES]
            y0_ref[idx] = ry[j][:, LANES:2 * LANES] + rkv[j]
            mx_ref[idx] = (eye_f + jnp.where(same_head, mx[j], 0.0)) * glast
            nx_ref[idx] = jnp.where(same_head, nx[j], 0.0) * glast
        yield

    def scan_chunk(ci):
        pairs = range(n_pairs)
        sts = [s_ref[p] for p in pairs]
        ys = [_mm(rp_ref[ci * n_pairs + p], sts[p], NT) for p in pairs]
        sn = [_mm(sts[p], mx_ref[ci * n_pairs + p], NN) for p in pairs]
        for p in pairs:
            s_ref[p] = sn[p] + nx_ref[ci * n_pairs + p]
        y_ref[ci * C:(ci + 1) * C, :] = jnp.concatenate(
            [ys[p] + y0_ref[ci * n_pairs + p] for p in pairs], axis=1)

    yield from _drive(TB // R, G, elem_steps, mm_stages, scan_chunk)
    prev_ref[...] = pc_ref[0, TB - 8:TB, :]

    y = y_ref[...]
    mean = _segsum(y, g01) * (1.0 / HEAD)
    d = y - mean
    var = _segsum(d * d, g01) * (1.0 / HEAD)
    yn = d * lax.rsqrt(var + GN_EPS) * lnx_w + lnx_b
    o_ref[0] = ((yn + bonus_ref[...]) * gate_ref[...]).astype(o_ref.dtype)


MIX_TICKS = (1, 1)
N_HGRN_IN, N_HGRN_SCRATCH = 5, 5
N_RWKV_IN, N_RWKV_SCRATCH = 7, 9


def _mixers_kernel(layer, *refs):
    it = iter(refs)
    take = lambda k: [next(it) for _ in range(k)]
    h_in, r_in = take(N_HGRN_IN), take(N_RWKV_IN)
    h_out, r_out = take(1), take(1)
    h_scr, r_scr = take(N_HGRN_SCRATCH), take(N_RWKV_SCRATCH)
    programs = [(_hgrn_program(layer, *h_in, *h_out, *h_scr), MIX_TICKS[0]),
                (_rwkv_program(*r_in, *r_out, *r_scr), MIX_TICKS[1])]
    while programs:
        for entry in list(programs):
            prog, ticks = entry
            for _ in range(ticks):
                if next(prog, programs) is programs:
                    programs.remove(entry)
                    break


def _mixers(pb, pc, hgrn_lb, beta_b, wst, mu, vecs, lora_w, g_up, tri, g01, layer, tb):
    B, T, W4 = pb.shape
    W = W4 // 4
    PC = pc.shape[2]
    n_mats = (tb // CHUNK) * (W // LANES)
    mats = pltpu.VMEM((n_mats, LANES, LANES), F32)
    rows = pltpu.VMEM((n_mats, CHUNK, LANES), F32)
    block = pltpu.VMEM((tb, W), F32)
    state = pltpu.VMEM((W // LANES, LANES, LANES), F32)
    small = lambda a: pl.BlockSpec(a.shape, lambda b, t: (0, 0))
    out = pl.BlockSpec((1, tb, W), lambda b, t: (b, t, 0))
    return pl.pallas_call(
        functools.partial(_mixers_kernel, layer),
        grid=(B, T // tb),
        in_specs=[
            pl.BlockSpec((1, tb, W4), lambda b, t: (b, t, 0)),
            small(hgrn_lb), small(beta_b), small(wst), small(g01),
            pl.BlockSpec((1, tb, PC), lambda b, t: (b, t, 0)),
            small(mu), small(vecs), small(lora_w), small(g_up), small(tri), small(g01),
        ],
        out_specs=[out, out],
        out_shape=[jax.ShapeDtypeStruct((B, T, W), BF16)] * 2,
        scratch_shapes=[state, block, block, mats, pltpu.VMEM((tb // CHUNK, 8, W), F32),
                        state, pltpu.VMEM((8, PC), F32), rows, rows, mats, mats,
                        block, block, block],
        compiler_params=pltpu.CompilerParams(
            dimension_semantics=("parallel", "arbitrary"), vmem_limit_bytes=VMEM_LIMIT),
        name="mixers",
    )(pb, hgrn_lb, beta_b, wst, g01, pc, mu, vecs, lora_w, g_up, tri, g01)


def _ffn_kernel(final, layer, widths, x_ref, ya_ref, yb_ref, yc_ref, mod_ref, g_ref, wo_hbm,
                wg_hbm, wu_hbm, wd_hbm, fg_ref, o_ref,
                wo_ref, wg_ref, wu_ref, wd_ref, stage_ref, sem_ref):
    @pl.when(_is_first_step())
    def _():
        for src, dst in ((wo_hbm, wo_ref), (wg_hbm, wg_ref), (wu_hbm, wu_ref),
                         (wd_hbm, wd_ref)):
            _load_weight_bf16(src, layer, dst, stage_ref, sem_ref)

    x = x_ref[0]
    acc = None
    off = 0
    for ref, wd in zip((ya_ref, yb_ref, yc_ref), widths):
        d = jnp.dot(ref[0], wo_ref[off:off + wd, :], preferred_element_type=F32)
        acc = d if acc is None else acc + d
        off += wd
    x1 = x + mod_ref[0, 2:3, :] * acc
    ms = jnp.mean(x1 * x1, axis=-1, keepdims=True)
    h = x1 * lax.rsqrt(ms + NORM_EPS) * g_ref[...]
    h = h * (1.0 + mod_ref[0, 4:5, :]) + mod_ref[0, 3:4, :]
    hb = h.astype(BF16)
    gt = jnp.dot(hb, wg_ref[...], preferred_element_type=F32)
    up = jnp.dot(hb, wu_ref[...], preferred_element_type=F32)
    act = (gt * _sigmoid(gt) * up).astype(BF16)
    dn = jnp.dot(act, wd_ref[...], preferred_element_type=F32)
    x2 = x1 + mod_ref[0, 5:6, :] * dn
    if final:
        ms2 = jnp.mean(x2 * x2, axis=-1, keepdims=True)
        x2 = x2 * lax.rsqrt(ms2 + NORM_EPS) * fg_ref[...]
    o_ref[0] = x2


def _out_ffn(x, ya, yb, yc, mod_l, gain, wo, wg, wu, wd, final_g, layer, tm):
    B, T, D = x.shape
    L, _, FF = wg.shape
    widths = (ya.shape[2], yb.shape[2], yc.shape[2])
    hbm = pl.BlockSpec(memory_space=pl.ANY)
    return pl.pallas_call(
        functools.partial(_ffn_kernel, layer == L - 1, layer, widths),
        grid=(B, T // tm),
        in_specs=[
            pl.BlockSpec((1, tm, D), lambda b, t: (b, t, 0)),
            pl.BlockSpec((1, tm, widths[0]), lambda b, t: (b, t, 0)),
            pl.BlockSpec((1, tm, widths[1]), lambda b, t: (b, t, 0)),
            pl.BlockSpec((1, tm, widths[2]), lambda b, t: (b, t, 0)),
            pl.BlockSpec((1, 6, D), lambda b, t: (b, 0, 0)),
            pl.BlockSpec((1, D), lambda b, t: (0, 0)),
            hbm, hbm, hbm, hbm,
            pl.BlockSpec((1, D), lambda b, t: (0, 0)),
        ],
        out_specs=pl.BlockSpec((1, tm, D), lambda b, t: (b, t, 0)),
        out_shape=jax.ShapeDtypeStruct((B, T, D), F32),
        scratch_shapes=[pltpu.VMEM((D, D), BF16), pltpu.VMEM((D, FF), BF16),
                        pltpu.VMEM((D, FF), BF16), pltpu.VMEM((FF, D), BF16),
                        pltpu.VMEM((W_SLOTS, W_CHUNK, max(D, FF)), F32),
                        pltpu.SemaphoreType.DMA((W_SLOTS,))],
        compiler_params=pltpu.CompilerParams(
            dimension_semantics=("arbitrary", "arbitrary"), vmem_limit_bytes=VMEM_LIMIT),
        name="out_ffn",
    )(x, ya, yb, yc, mod_l, gain, wo, wg, wu, wd, final_g)


def _block_diag(w):
    G, n, _ = w.shape
    eye = jnp.eye(G, dtype=w.dtype)
    return (eye[:, None, :, None] * w[:, :, None, :]).reshape(G * n, G * n)


def _seg_ones():
    idx = np.arange(LANES) // HEAD
    return jnp.asarray((idx[:, None] == idx[None, :]).astype(np.float32), dtype=BF16)


def kernel(x, c, norm1_g, norm2_g, ada_w, ada_b, w_in, conv_w, conv_b, rg_w, rg_b, ig_w, ig_b, lru_lam, hgrn_lb, rwkv_mu, rwkv_w0, rwkv_w_up, rwkv_a0, rwkv_a_up, rwkv_g_up, rwkv_k_k, rwkv_k_a, rwkv_r_k, rwkv_lnx_w, rwkv_lnx_b, mix_beta, w_out, ffn_w_gate, ffn_w_up, ffn_w_down, final_g):
    B, T, D = x.shape
    L = w_in.shape[0]
    WA = conv_w.shape[2]
    WB = hgrn_lb.shape[1]
    WC = rwkv_w0.shape[1]
    widths_p = (2 * WA, 4 * WB, rwkv_mu.shape[1])

    mod = _modulation(c, ada_w, ada_b).reshape(L, B, 6, D)
    wst = jnp.asarray(_hgrn_weights(HGRN_GROUP), dtype=BF16)
    tri = jnp.asarray(np.kron(np.eye(RWKV_GROUP, dtype=np.float32),
                              np.tril(np.ones((CHUNK, CHUNK), np.float32))), dtype=BF16)
    g_a = _seg_ones()

    for l in range(L):
        beta = mix_beta[l]
        vec_a = jnp.stack([conv_b[l], rg_b[l], ig_b[l], lru_lam[l], beta[:WA],
                           jnp.zeros_like(beta[:WA]), jnp.zeros_like(beta[:WA]),
                           jnp.zeros_like(beta[:WA])])
        wg_a = jnp.concatenate([_block_diag(rg_w[l]), _block_diag(ig_w[l])], axis=1).astype(BF16)
        ya, pb, pc = _in_projection(x, mod[l], norm1_g[l][None], w_in, conv_w[l], vec_a, wg_a,
                                    g_a, widths_p, l, IN_ROWS)
        vec_c = jnp.stack([rwkv_w0[l], rwkv_a0[l], rwkv_k_k[l], rwkv_k_a[l],
                           rwkv_r_k[l].reshape(-1), rwkv_lnx_w[l], rwkv_lnx_b[l],
                           beta[WA + WB:]])
        zw = jnp.zeros_like(rwkv_w_up[l])
        lora_w = jnp.concatenate(
            [jnp.concatenate([rwkv_w_up[l], zw], axis=1),
             jnp.concatenate([zw, rwkv_a_up[l]], axis=1)], axis=0)
        yb, yc = _mixers(pb, pc, hgrn_lb, beta[None, WA:WA + WB], wst, rwkv_mu[l][None], vec_c,
                         lora_w, rwkv_g_up[l], tri, g_a, l, MIXER_ROWS)
        x = _out_ffn(x, ya, yb, yc, mod[l], norm2_g[l][None], w_out, ffn_w_gate, ffn_w_up,
                     ffn_w_down, final_g[None], l, FFN_ROWS)
    return x
```
